```python
import math
import jax, jax.numpy as jnp
from jax import lax
import numpy as np

D_MODEL = 4096
BATCH = 8
SEQ = 2048
DEPTH = 4

D_FF = 5504
NORM_EPS = 1e-6
ROPE_THETA = 500000.0
ROPE_FRACTION = 4
POOL_WINDOWS = (2, 4, 8, 16)
POOL_WIDTH = D_MODEL // 2
POOL_GROUP = POOL_WIDTH // len(POOL_WINDOWS)
SWA_HEAD_DIM = 64
SWA_HEADS = (D_MODEL // 2) // SWA_HEAD_DIM
SWA_KV_HEADS = SWA_HEADS // 8
SWA_GROUP = SWA_HEADS // SWA_KV_HEADS
SWA_WINDOW = 128
ATTN_BLOCK = 128
EVEN_PARTS = (POOL_WIDTH, SWA_HEADS * SWA_HEAD_DIM, SWA_KV_HEADS * SWA_HEAD_DIM, SWA_KV_HEADS * SWA_HEAD_DIM)
S5_WIDTH = D_MODEL // 4
S5_GROUP_CH = 16
S5_GROUPS = S5_WIDTH // S5_GROUP_CH
S5_STATE = 64
S5_DT_MIN = 1e-3
S5_DT_MAX = 1e-1
NSA_HEAD_DIM = 128
NSA_HEADS = (D_MODEL - S5_WIDTH) // NSA_HEAD_DIM
NSA_KV_HEADS = NSA_HEADS // 4
NSA_GROUP = NSA_HEADS // NSA_KV_HEADS
NSA_CMP_BLOCK = 32
NSA_CMP_STRIDE = 16
NSA_SLC_BLOCK = 64
NSA_TOP_N = 8
NSA_WINDOW = 512
NSA_SLC_CHUNK = 16
NSA_KV_WIDTH = NSA_KV_HEADS * NSA_HEAD_DIM
ODD_PARTS = (S5_WIDTH, NSA_HEADS * NSA_HEAD_DIM) + (NSA_KV_WIDTH,) * 6 + (3 * NSA_HEADS,)

kernel_name = 'hybrid_pool_swa_s5_nsa_macaron'

F32 = jnp.float32


def _split_points(parts):
    return [int(v) for v in np.cumsum(parts)[:-1]]


def rmsnorm(x, gain):
    xf = x.astype(F32)
    y = xf * lax.rsqrt(jnp.mean(xf * xf, axis=-1, keepdims=True) + NORM_EPS)
    return (y * gain.astype(F32)).astype(x.dtype)


def swiglu(h, w_gate, w_up, w_down):
    return (jax.nn.silu(h @ w_gate) * (h @ w_up)) @ w_down


def partial_rope(x, positions):
    d = x.shape[-1]
    rot = d // ROPE_FRACTION
    half = rot // 2
    inv_freq = jnp.power(ROPE_THETA, -jnp.arange(half, dtype=F32) * 2.0 / rot)
    ang = positions.astype(F32)[:, None] * inv_freq[None, :]
    cos = jnp.cos(ang)[None, :, None, :]
    sin = jnp.sin(ang)[None, :, None, :]
    xf = x.astype(F32)
    x1 = xf[..., :half]
    x2 = xf[..., half:rot]
    out = jnp.concatenate([x1 * cos - x2 * sin, x2 * cos + x1 * sin, xf[..., rot:]], axis=-1)
    return out.astype(x.dtype)


def banded_attention(q, k, v, window, sinks):
    B, S, Kh, G, d = q.shape
    nb = S // ATTN_BLOCK
    span = window + ATTN_BLOCK
    kp = jnp.pad(k, ((0, 0), (window, 0), (0, 0), (0, 0)))
    vp = jnp.pad(v, ((0, 0), (window, 0), (0, 0), (0, 0)))
    qb = q.reshape(B, nb, ATTN_BLOCK, Kh, G, d).swapaxes(0, 1)
    scale = d ** -0.5

    def one_block(args):
        n, qblk = args
        start = n * ATTN_BLOCK
        kb = lax.dynamic_slice_in_dim(kp, start, span, axis=1)
        vb = lax.dynamic_slice_in_dim(vp, start, span, axis=1)
        s = jnp.einsum('bqkgd,bskd->bkgqs', qblk, kb, preferred_element_type=F32) * scale
        t = start + jnp.arange(ATTN_BLOCK)
        sp = start - window + jnp.arange(span)
        diff = t[:, None] - sp[None, :]
        vis = (diff >= 0) & (diff < window) & (sp[None, :] >= 0)
        s = jnp.where(vis, s, -jnp.inf)
        m = jnp.max(s, axis=-1, keepdims=True)
        if sinks is not None:
            sk = sinks.astype(F32)[None, :, :, None, None]
            m = jnp.maximum(m, sk)
        e = jnp.exp(s - m)
        denom = jnp.sum(e, axis=-1, keepdims=True)
        if sinks is not None:
            denom = denom + jnp.exp(sk - m)
        p = (e / denom).astype(vb.dtype)
        return jnp.einsum('bkgqs,bskd->bqkgd', p, vb)

    out = lax.map(one_block, (jnp.arange(nb), qb))
    return out.swapaxes(0, 1).reshape(B, S, Kh, G, d)


def multiscale_pool(u, w_pool, pool_scale):
    B, S, _ = u.shape
    ug = u.astype(F32).reshape(B, S, len(POOL_WINDOWS), POOL_GROUP)
    csum = jnp.cumsum(ug, axis=1)
    pos = jnp.arange(S)
    means = []
    for gi, w in enumerate(POOL_WINDOWS):
        c = csum[:, :, gi]
        lagged = jnp.pad(c, ((0, 0), (w, 0), (0, 0)))[:, :S]
        count = jnp.minimum(pos + 1, w).astype(F32)[None, :, None]
        means.append((c - lagged) / count)
    z = (jnp.stack(means, axis=2) - ug).astype(u.dtype)
    z = jnp.einsum('bsgc,gce->bsge', z, w_pool)
    return z.reshape(B, S, POOL_WIDTH) * pool_scale


def _ssm_combine(left, right):
    a1r, a1i, b1r, b1i = left
    a2r, a2i, b2r, b2i = right
    return (a2r * a1r - a2i * a1i,
            a2r * a1i + a2i * a1r,
            a2r * b1r - a2i * b1i + b2r,
            a2r * b1i + a2i * b1r + b2i)


def s5_glu(u, a_re, a_im, log_dt, b_re, b_im, c_re, c_im, d_skip, glu_w, glu_b):
    B, S, _ = u.shape
    uf = u.astype(F32).reshape(B, S, S5_GROUPS, S5_GROUP_CH)
    lam_re = a_re.astype(F32)
    lam_im = a_im.astype(F32)
    dt = jnp.exp(log_dt.astype(F32))[:, None]
    decay = jnp.exp(lam_re * dt)
    abar_re = decay * jnp.cos(lam_im * dt)
    abar_im = decay * jnp.sin(lam_im * dt)
    inv_mag = 1.0 / (lam_re * lam_re + lam_im * lam_im)
    num_re = abar_re - 1.0
    f_re = (num_re * lam_re + abar_im * lam_im) * inv_mag
    f_im = (abar_im * lam_re - num_re * lam_im) * inv_mag
    br = b_re.astype(F32)
    bi = b_im.astype(F32)
    bbar_re = f_re[..., None] * br - f_im[..., None] * bi
    bbar_im = f_re[..., None] * bi + f_im[..., None] * br
    bu_re = jnp.einsum('bsgh,gnh->bsgn', uf, bbar_re)
    bu_im = jnp.einsum('bsgh,gnh->bsgn', uf, bbar_im)
    a_r = jnp.broadcast_to(abar_re[None, None], (1, S, S5_GROUPS, S5_STATE))
    a_i = jnp.broadcast_to(abar_im[None, None], (1, S, S5_GROUPS, S5_STATE))
    _, _, x_re, x_im = lax.associative_scan(_ssm_combine, (a_r, a_i, bu_re, bu_im), axis=1)
    y = (jnp.einsum('bsgn,ghn->bsgh', x_re, c_re.astype(F32))
         - jnp.einsum('bsgn,ghn->bsgh', x_im, c_im.astype(F32))
         + d_skip.astype(F32) * uf)
    y = jax.nn.gelu(y.reshape(B, S, S5_WIDTH))
    out = y * jax.nn.sigmoid(y @ glu_w.astype(F32) + glu_b.astype(F32))
    return out.astype(u.dtype)


def compress_blocks(x, pos_emb, w1, b1, w2):
    B, S, Kh, d = x.shape
    nchunk = S // NSA_CMP_STRIDE
    per = NSA_CMP_BLOCK // NSA_CMP_STRIDE
    nc = nchunk - per + 1
    ch = x.reshape(B, nchunk, NSA_CMP_STRIDE, Kh, d)
    blocks = jnp.concatenate([ch[:, i:nc + i] for i in range(per)], axis=2)
    blocks = blocks + pos_emb[None, None, :, None, :]
    h = jax.nn.gelu(jnp.einsum('bilkd,ldo->biko', blocks, w1) + b1)
    return jnp.einsum('biko,oe->bike', h, w2)


def selected_attention(q, k, v, idx):
    B, S, Kh, G, d = q.shape
    n = idx.shape[-1]
    nslc = S // NSA_SLC_BLOCK
    kb = k.reshape(B, nslc, NSA_SLC_BLOCK, Kh, d).transpose(0, 3, 1, 2, 4)
    vb = v.reshape(B, nslc, NSA_SLC_BLOCK, Kh, d).transpose(0, 3, 1, 2, 4)
    nq = S // NSA_SLC_CHUNK
    qc = q.reshape(B, nq, NSA_SLC_CHUNK, Kh, G, d).swapaxes(0, 1)
    ic = idx.reshape(B, nq, NSA_SLC_CHUNK, Kh, n).swapaxes(0, 1)
    b_ix = jnp.arange(B)[:, None, None, None]
    h_ix = jnp.arange(Kh)[None, None, :, None]
    offs = jnp.arange(NSA_SLC_BLOCK)
    scale = d ** -0.5

    def one_chunk(args):
        c, qblk, iblk = args
        kg = kb[b_ix, h_ix, iblk]
        vg = vb[b_ix, h_ix, iblk]
        s = jnp.einsum('bqkgd,bqknld->bqkgnl', qblk, kg, preferred_element_type=F32) * scale
        t = c * NSA_SLC_CHUNK + jnp.arange(NSA_SLC_CHUNK)
        kpos = iblk[..., None] * NSA_SLC_BLOCK + offs
        vis = kpos <= t[None, :, None, None, None]
        s = jnp.where(vis[:, :, :, None], s, -jnp.inf)
        p = jax.nn.softmax(s.reshape(s.shape[:-2] + (-1,)), axis=-1).reshape(s.shape)
        return jnp.einsum('bqkgnl,bqknld->bqkgd', p.astype(vg.dtype), vg)

    out = lax.map(one_chunk, (jnp.arange(nq), qc, ic))
    return out.swapaxes(0, 1).reshape(B, S, Kh, G, d)


def nsa_attention(q, k_cmp, v_cmp, k_slc, v_slc, k_win, v_win, gate_logits,
                  cmp_pos, cmp_w1, cmp_b1, cmp_w2):
    B, S, Kh, G, d = q.shape
    scale = d ** -0.5
    pos = jnp.arange(S)
    kc = compress_blocks(k_cmp, cmp_pos[0], cmp_w1[0], cmp_b1[0], cmp_w2[0])
    vc = compress_blocks(v_cmp, cmp_pos[1], cmp_w1[1], cmp_b1[1], cmp_w2[1])
    nc = kc.shape[1]
    s = jnp.einsum('bskgd,bikd->bskgi', q, kc, preferred_element_type=F32) * scale
    blk_end = jnp.arange(nc) * NSA_CMP_STRIDE + NSA_CMP_BLOCK - 1
    vis = (blk_end[None, :] <= pos[:, None])[None, :, None, None, :]
    s = jnp.where(vis, s, -jnp.inf)
    m = jnp.max(s, axis=-1, keepdims=True)
    e = jnp.exp(s - jnp.where(jnp.isfinite(m), m, 0.0))
    p_cmp = e / jnp.maximum(jnp.sum(e, axis=-1, keepdims=True), 1.0)
    o_cmp = jnp.einsum('bskgi,bikd->bskgd', p_cmp.astype(vc.dtype), vc)
    nslc = S // NSA_SLC_BLOCK
    cstart = jnp.arange(nc) * NSA_CMP_STRIDE
    sstart = jnp.arange(nslc) * NSA_SLC_BLOCK
    overlap = ((cstart[:, None] < sstart[None, :] + NSA_SLC_BLOCK)
               & (cstart[:, None] + NSA_CMP_BLOCK > sstart[None, :])).astype(F32)
    imp = jnp.einsum('bskgi,ij->bskj', p_cmp, overlap)
    blk = jnp.arange(nslc)[None, :]
    cur = (pos // NSA_SLC_BLOCK)[:, None]
    future = (blk > cur)[None, :, None, :]
    forced = ((blk == 0) | (blk == cur) | (blk == cur - 1))[None, :, None, :]
    score = jnp.where(future, -jnp.inf, jnp.where(forced, jnp.inf, imp))
    _, idx = lax.top_k(score, min(NSA_TOP_N, nslc))
    o_slc = selected_attention(q, k_slc, v_slc, idx)
    o_win = banded_attention(q, k_win, v_win, NSA_WINDOW, None)
    gate = jax.nn.sigmoid(gate_logits.astype(F32))
    o = gate[..., 0:1] * o_cmp + gate[..., 1:2] * o_slc + gate[..., 2:3] * o_win
    return o.astype(q.dtype)


def even_mixer(h, positions, w_in, w_out, pool_w, pool_scale, sinks):
    B, S, _ = h.shape
    z = h @ w_in
    u_pool, q, k, v = jnp.split(z, _split_points(EVEN_PARTS), axis=-1)
    q = partial_rope(q.reshape(B, S, SWA_HEADS, SWA_HEAD_DIM), positions)
    k = partial_rope(k.reshape(B, S, SWA_KV_HEADS, SWA_HEAD_DIM), positions)
    v = v.reshape(B, S, SWA_KV_HEADS, SWA_HEAD_DIM)
    q = q.reshape(B, S, SWA_KV_HEADS, SWA_GROUP, SWA_HEAD_DIM)
    o = banded_attention(q, k, v, SWA_WINDOW, sinks.reshape(SWA_KV_HEADS, SWA_GROUP))
    a = multiscale_pool(u_pool, pool_w, pool_scale)
    return jnp.concatenate([a, o.reshape(B, S, SWA_HEADS * SWA_HEAD_DIM)], axis=-1) @ w_out


def odd_mixer(h, positions, w_in, w_out, a_re, a_im, log_dt, b_re, b_im, c_re, c_im, d_skip,
              glu_w, glu_b, cmp_pos, cmp_w1, cmp_b1, cmp_w2):
    B, S, _ = h.shape
    z = h @ w_in
    u, q, kc, vc, ks, vs, kw, vw, gl = jnp.split(z, _split_points(ODD_PARTS), axis=-1)
    s5_out = s5_glu(u, a_re, a_im, log_dt, b_re, b_im, c_re, c_im, d_skip, glu_w, glu_b)

    def kv_heads(t):
        return t.reshape(B, S, NSA_KV_HEADS, NSA_HEAD_DIM)

    q = partial_rope(q.reshape(B, S, NSA_HEADS, NSA_HEAD_DIM), positions)
    q = q.reshape(B, S, NSA_KV_HEADS, NSA_GROUP, NSA_HEAD_DIM)
    o = nsa_attention(q,
                      partial_rope(kv_heads(kc), positions), kv_heads(vc),
                      partial_rope(kv_heads(ks), positions), kv_heads(vs),
                      partial_rope(kv_heads(kw), positions), kv_heads(vw),
                      gl.reshape(B, S, NSA_KV_HEADS, NSA_GROUP, 3),
                      cmp_pos, cmp_w1, cmp_b1, cmp_w2)
    return jnp.concatenate([s5_out, o.reshape(B, S, NSA_HEADS * NSA_HEAD_DIM)], axis=-1) @ w_out


def setup_inputs(seed: int = 0) -> dict:
    key = jax.random.key(seed)
    keys = list(jax.random.split(key, 32))
    n_even = (DEPTH + 1) // 2
    n_odd = DEPTH // 2

    def nrm(i, shape, scale):
        return jax.random.normal(keys[i], shape, F32) * scale

    even_in = sum(EVEN_PARTS)
    odd_in = sum(ODD_PARTS)
    mix_w = POOL_WIDTH + SWA_HEADS * SWA_HEAD_DIM
    odd_mix_w = S5_WIDTH + NSA_HEADS * NSA_HEAD_DIM
    a_im_init = jnp.broadcast_to(jnp.pi * jnp.arange(S5_STATE, dtype=F32), (n_odd, S5_GROUPS, S5_STATE))
    return {
        'x': nrm(0, (BATCH, SEQ, D_MODEL), 1.0),
        'norm_gains': 1.0 + nrm(1, (DEPTH, 6, D_MODEL), 0.05),
        'ffn1_w_gate': nrm(2, (DEPTH, D_MODEL, D_FF), D_MODEL ** -0.5),
        'ffn1_w_up': nrm(3, (DEPTH, D_MODEL, D_FF), D_MODEL ** -0.5),
        'ffn1_w_down': nrm(4, (DEPTH, D_FF, D_MODEL), D_FF ** -0.5),
        'ffn2_w_gate': nrm(5, (DEPTH, D_MODEL, D_FF), D_MODEL ** -0.5),
        'ffn2_w_up': nrm(6, (DEPTH, D_MODEL, D_FF), D_MODEL ** -0.5),
        'ffn2_w_down': nrm(7, (DEPTH, D_FF, D_MODEL), D_FF ** -0.5),
        'ev_w_in': nrm(8, (n_even, D_MODEL, even_in), D_MODEL ** -0.5),
        'ev_w_out': nrm(9, (n_even, mix_w, D_MODEL), mix_w ** -0.5),
        'pool_w': nrm(10, (n_even, len(POOL_WINDOWS), POOL_GROUP, POOL_GROUP), POOL_GROUP ** -0.5),
        'pool_scale': 1.0 + nrm(11, (n_even, POOL_WIDTH), 0.1),
        'swa_sinks': nrm(12, (n_even, SWA_HEADS), 0.5),
        'od_w_in': nrm(13, (n_odd, D_MODEL, odd_in), D_MODEL ** -0.5),
        'od_w_out': nrm(14, (n_odd, odd_mix_w, D_MODEL), odd_mix_w ** -0.5),
        's5_a_re': -0.5 + nrm(15, (n_odd, S5_GROUPS, S5_STATE), 0.01),
        's5_a_im': a_im_init + nrm(16, (n_odd, S5_GROUPS, S5_STATE), 0.01),
        's5_log_dt': jax.random.uniform(keys[17], (n_odd, S5_GROUPS), F32,
                                        minval=math.log(S5_DT_MIN), maxval=math.log(S5_DT_MAX)),
        's5_b_re': nrm(18, (n_odd, S5_GROUPS, S5_STATE, S5_GROUP_CH), (2.0 * S5_GROUP_CH) ** -0.5),
        's5_b_im': nrm(19, (n_odd, S5_GROUPS, S5_STATE, S5_GROUP_CH), (2.0 * S5_GROUP_CH) ** -0.5),
        's5_c_re': nrm(20, (n_odd, S5_GROUPS, S5_GROUP_CH, S5_STATE), (2.0 * S5_STATE) ** -0.5),
        's5_c_im': nrm(21, (n_odd, S5_GROUPS, S5_GROUP_CH, S5_STATE), (2.0 * S5_STATE) ** -0.5),
        's5_d': nrm(22, (n_odd, S5_GROUPS, S5_GROUP_CH), 1.0),
        's5_glu_w': nrm(23, (n_odd, S5_WIDTH, S5_WIDTH), S5_WIDTH ** -0.5),
        's5_glu_b': nrm(24, (n_odd, S5_WIDTH), 0.01),
        'nsa_cmp_pos': nrm(25, (n_odd, 2, NSA_CMP_BLOCK, NSA_HEAD_DIM), 0.02),
        'nsa_cmp_w1': nrm(26, (n_odd, 2, NSA_CMP_BLOCK, NSA_HEAD_DIM, NSA_HEAD_DIM),
                          (NSA_CMP_BLOCK * NSA_HEAD_DIM) ** -0.5),
        'nsa_cmp_b1': nrm(27, (n_odd, 2, NSA_HEAD_DIM), 0.01),
        'nsa_cmp_w2': nrm(28, (n_odd, 2, NSA_HEAD_DIM, NSA_HEAD_DIM), NSA_HEAD_DIM ** -0.5),
    }


def reference(x, norm_gains, ffn1_w_gate, ffn1_w_up, ffn1_w_down, ffn2_w_gate, ffn2_w_up, ffn2_w_down,
              ev_w_in, ev_w_out, pool_w, pool_scale, swa_sinks,
              od_w_in, od_w_out, s5_a_re, s5_a_im, s5_log_dt, s5_b_re, s5_b_im, s5_c_re, s5_c_im,
              s5_d, s5_glu_w, s5_glu_b, nsa_cmp_pos, nsa_cmp_w1, nsa_cmp_b1, nsa_cmp_w2):
    positions = jnp.arange(x.shape[1], dtype=jnp.int32)
    for layer in range(DEPTH):
        g = norm_gains[layer]
        h = rmsnorm(x, g[0])
        x = x + 0.5 * rmsnorm(swiglu(h, ffn1_w_gate[layer], ffn1_w_up[layer], ffn1_w_down[layer]), g[1])
        h = rmsnorm(x, g[2])
        i = layer // 2
        if layer % 2 == 0:
            m = even_mixer(h, positions, ev_w_in[i], ev_w_out[i], pool_w[i], pool_scale[i], swa_sinks[i])
        else:
            m = odd_mixer(h, positions, od_w_in[i], od_w_out[i], s5_a_re[i], s5_a_im[i], s5_log_dt[i],
                          s5_b_re[i], s5_b_im[i], s5_c_re[i], s5_c_im[i], s5_d[i], s5_glu_w[i],
                          s5_glu_b[i], nsa_cmp_pos[i], nsa_cmp_w1[i], nsa_cmp_b1[i], nsa_cmp_w2[i])
        x = x + rmsnorm(m, g[3])
        h = rmsnorm(x, g[4])
        x = x + 0.5 * rmsnorm(swiglu(h, ffn2_w_gate[layer], ffn2_w_up[layer], ffn2_w_down[layer]), g[5])
    return x
```

```python
import functools
import math

import numpy as np
import jax
import jax.numpy as jnp
from jax import lax
from jax.experimental import pallas as pl
from jax.experimental.pallas import tpu as pltpu

F32 = jnp.float32
BF16 = jnp.bfloat16

D_MODEL = 4096
D_FF = 5504
NORM_EPS = 1e-6
ROPE_THETA = 500000.0
ROPE_FRACTION = 4
POOL_WINDOWS = (2, 4, 8, 16)
POOL_WIDTH = D_MODEL // 2
POOL_GROUP = POOL_WIDTH // len(POOL_WINDOWS)
POOL_HALO = 16
SWA_HEAD_DIM = 64
SWA_HEADS = 32
SWA_KV_HEADS = 4
SWA_GROUP = SWA_HEADS // SWA_KV_HEADS
SWA_WINDOW = 128
ATTN_BLOCK = 128
S5_WIDTH = D_MODEL // 4
S5_GROUP_CH = 16
S5_GROUPS = S5_WIDTH // S5_GROUP_CH
S5_STATE = 64
NSA_HEAD_DIM = 128
NSA_HEADS = 24
NSA_KV_HEADS = 6
NSA_GROUP = 4
NSA_CMP_BLOCK = 32
NSA_CMP_STRIDE = 16
NSA_SLC_BLOCK = 64
NSA_TOP_N = 8
NSA_WINDOW = 512
NSA_KV_WIDTH = NSA_KV_HEADS * NSA_HEAD_DIM

LANES = 128
SUBLANES = 8
MXU_DIM = 256
VMEM_LIMIT_BYTES = 56 * 1024 * 1024

ROW_TILE = 512
FF_TILE = 256
D_FF_PAD = 5632
PROJ_TILE = 512
OUT_K_TILE = 512
NORM_ROWS = 32
NEG_BIG = -1e30


def _params(*sem):
    return pltpu.CompilerParams(dimension_semantics=sem, vmem_limit_bytes=VMEM_LIMIT_BYTES)


def _norm_rows(dst_ref, src_ref, g_ref, rows, res_ref=None, res_scale=1.0):
    g = g_ref[...]

    def body(c, carry):
        r = pl.multiple_of(c * NORM_ROWS, NORM_ROWS)
        v = src_ref[pl.ds(r, NORM_ROWS), :].astype(F32)
        ms = jnp.mean(v * v, axis=-1, keepdims=True)
        y = v * lax.rsqrt(ms + NORM_EPS) * g
        if res_ref is not None:
            y = res_ref[pl.ds(r, NORM_ROWS), :] + res_scale * y
        dst_ref[pl.ds(r, NORM_ROWS), :] = y.astype(dst_ref.dtype)
        return carry

    lax.fori_loop(0, rows // NORM_ROWS, body, 0)


def _ffn_kernel(x_ref, gpre_ref, wg_ref, wu_ref, wd_ref, gpost_ref, o_ref, h_ref, *, tm, nj):
    j = pl.program_id(1)

    @pl.when(j == 0)
    def _():
        _norm_rows(h_ref, x_ref, gpre_ref, tm)
        o_ref[...] = jnp.zeros_like(o_ref)

    h = h_ref[...]
    gate = jnp.dot(h, wg_ref[...], preferred_element_type=F32)
    up = jnp.dot(h, wu_ref[...], preferred_element_type=F32)
    act = (gate * jax.nn.sigmoid(gate) * up).astype(BF16)
    nc = 512
    for n in range(D_MODEL // nc):
        sl = slice(n * nc, (n + 1) * nc)
        o_ref[:, sl] += jnp.dot(act, wd_ref[:, sl], preferred_element_type=F32)

    @pl.when(j == nj - 1)
    def _():
        _norm_rows(o_ref, o_ref, gpost_ref, tm, res_ref=x_ref, res_scale=0.5)


def _ffn(x, g_pre, wg, wu, wd, g_post):
    m = x.shape[0]
    tm = min(ROW_TILE, m)
    nj = D_FF_PAD // FF_TILE
    kern = functools.partial(_ffn_kernel, tm=tm, nj=nj)
    return pl.pallas_call(
        kern,
        grid=(m // tm, nj),
        in_specs=[
            pl.BlockSpec((tm, D_MODEL), lambda i, j: (i, 0)),
            pl.BlockSpec((1, D_MODEL), lambda i, j: (0, 0)),
            pl.BlockSpec((D_MODEL, FF_TILE), lambda i, j: (0, j)),
            pl.BlockSpec((D_MODEL, FF_TILE), lambda i, j: (0, j)),
            pl.BlockSpec((FF_TILE, D_MODEL), lambda i, j: (j, 0)),
            pl.BlockSpec((1, D_MODEL), lambda i, j: (0, 0)),
        ],
        out_specs=pl.BlockSpec((tm, D_MODEL), lambda i, j: (i, 0)),
        out_shape=jax.ShapeDtypeStruct((m, D_MODEL), F32),
        scratch_shapes=[pltpu.VMEM((tm, D_MODEL), BF16)],
        compiler_params=_params("parallel", "arbitrary"),
        name="ffn",
    )(x, g_pre, wg, wu, wd, g_post)


def _proj_kernel(x_ref, g_ref, w_ref, cos_ref, sa_ref, sb_ref, o_ref, h_ref, *, tm, rope_lo, rope_hi, half):
    j = pl.program_id(1)

    @pl.when(j == 0)
    def _():
        _norm_rows(h_ref, x_ref, g_ref, tm)

    z = jnp.dot(h_ref[...], w_ref[...], preferred_element_type=F32)
    tn = z.shape[1]
    is_rope = jnp.logical_and(j >= rope_lo, j < rope_hi)

    @pl.when(is_rope)
    def _():
        zr = (z * cos_ref[...]
              + pltpu.roll(z, tn - half, 1) * sa_ref[...]
              + pltpu.roll(z, half, 1) * sb_ref[...])
        o_ref[...] = zr.astype(o_ref.dtype)

    @pl.when(jnp.logical_not(is_rope))
    def _():
        o_ref[...] = z.astype(o_ref.dtype)


def _rope_tables(seq, head_dim, n_rope_last):
    rot = head_dim // ROPE_FRACTION
    half = rot // 2
    inv_freq = jnp.power(ROPE_THETA, -jnp.arange(half, dtype=F32) * 2.0 / rot)
    ang = jnp.arange(seq, dtype=jnp.int32).astype(F32)[:, None] * inv_freq[None, :]
    cos, sin = jnp.cos(ang), jnp.sin(ang)
    one = jnp.ones((seq, head_dim - rot), F32)
    zero_h = jnp.zeros((seq, half), F32)
    zero_r = jnp.zeros((seq, head_dim - rot), F32)
    c_head = jnp.concatenate([cos, cos, one], axis=1)
    sa_head = jnp.concatenate([-sin, zero_h, zero_r], axis=1)
    sb_head = jnp.concatenate([zero_h, sin, zero_r], axis=1)
    reps = PROJ_TILE // head_dim
    col = jnp.arange(PROJ_TILE)[None, :]

    def kinds(t, ident):
        full = jnp.tile(t, (1, reps))
        part = jnp.where(col < n_rope_last, full, ident)
        return jnp.stack([full, part], axis=0)

    return kinds(c_head, 1.0), kinds(sa_head, 0.0), kinds(sb_head, 0.0), half


def _proj(x, g_pre, w, tables, seq, rope_lo, rope_hi):
    cos_t, sa_t, sb_t, half = tables
    m = x.shape[0]
    n = w.shape[1]
    tm = min(ROW_TILE, seq)
    tn = PROJ_TILE
    sblocks = seq // tm
    kern = functools.partial(_proj_kernel, tm=tm, rope_lo=rope_lo, rope_hi=rope_hi, half=half)

    def tab_map(i, j):
        return (jnp.where(j == rope_hi - 1, 1, 0), i % sblocks, 0)

    tab_spec = pl.BlockSpec((None, tm, tn), tab_map)
    return pl.pallas_call(
        kern,
        grid=(m // tm, n // tn),
        in_specs=[
            pl.BlockSpec((tm, D_MODEL), lambda i, j: (i, 0)),
            pl.BlockSpec((1, D_MODEL), lambda i, j: (0, 0)),
            pl.BlockSpec((D_MODEL, tn), lambda i, j: (0, j)),
            tab_spec, tab_spec, tab_spec,
        ],
        out_specs=pl.BlockSpec((tm, tn), lambda i, j: (i, j)),
        out_shape=jax.ShapeDtypeStruct((m, n), BF16),
        scratch_shapes=[pltpu.VMEM((tm, D_MODEL), BF16)],
        compiler_params=_params("parallel", "arbitrary"),
        name="proj",
    )(x, g_pre, w, cos_t, sa_t, sb_t)


def _outproj_kernel(a1_ref, a2_ref, w_ref, x_ref, g_ref, o_ref, *, tm, nk1, nk):
    k = pl.program_id(1)

    @pl.when(k == 0)
    def _():
        o_ref[...] = jnp.zeros_like(o_ref)

    def accumulate(a):
        nc = 512
        for n in range(D_MODEL // nc):
            sl = slice(n * nc, (n + 1) * nc)
            o_ref[:, sl] += jnp.dot(a, w_ref[:, sl], preferred_element_type=F32)

    @pl.when(k < nk1)
    def _():
        accumulate(a1_ref[...])

    @pl.when(k >= nk1)
    def _():
        accumulate(a2_ref[...])

    @pl.when(k == nk - 1)
    def _():
        _norm_rows(o_ref, o_ref, g_ref, tm, res_ref=x_ref, res_scale=1.0)


def _outproj(a1, a2, w, x, g_post):
    m = x.shape[0]
    tm = min(ROW_TILE, m)
    tk = OUT_K_TILE
    nk1 = a1.shape[1] // tk
    nk = w.shape[0] // tk
    kern = functools.partial(_outproj_kernel, tm=tm, nk1=nk1, nk=nk)
    return pl.pallas_call(
        kern,
        grid=(m // tm, nk),
        in_specs=[
            pl.BlockSpec((tm, tk), lambda i, k: (i, jnp.minimum(k, nk1 - 1))),
            pl.BlockSpec((tm, tk), lambda i, k: (i, jnp.maximum(k - nk1, 0))),
            pl.BlockSpec((tk, D_MODEL), lambda i, k: (k, 0)),
            pl.BlockSpec((tm, D_MODEL), lambda i, k: (i, 0)),
            pl.BlockSpec((1, D_MODEL), lambda i, k: (0, 0)),
        ],
        out_specs=pl.BlockSpec((tm, D_MODEL), lambda i, k: (i, 0)),
        out_shape=jax.ShapeDtypeStruct((m, D_MODEL), F32),
        compiler_params=_params("parallel", "arbitrary"),
        name="outproj",
    )(a1, a2, w, x, g_post)


def _swa_kernel(sink_ref, q_ref, kvp_ref, kvc_ref, o_ref):
    n = pl.program_id(1)
    blk = ATTN_BLOCK
    kvw = SWA_KV_HEADS * SWA_HEAD_DIM
    kv = jnp.concatenate([kvp_ref[...], kvc_ref[...]], axis=0)
    row = lax.broadcasted_iota(jnp.int32, (blk, 2 * blk), 0)
    col = lax.broadcasted_iota(jnp.int32, (blk, 2 * blk), 1)
    diff = row - col + blk
    vis = (diff >= 0) & (diff < SWA_WINDOW) & ((col >= blk) | (n > 0))
    scale = SWA_HEAD_DIM ** -0.5
    for kh in range(SWA_KV_HEADS):
        k_h = kv[:, kh * SWA_HEAD_DIM:(kh + 1) * SWA_HEAD_DIM]
        v_h = kv[:, kvw + kh * SWA_HEAD_DIM:kvw + (kh + 1) * SWA_HEAD_DIM]
        for g in range(SWA_GROUP):
            h = kh * SWA_GROUP + g
            q_h = q_ref[:, h * SWA_HEAD_DIM:(h + 1) * SWA_HEAD_DIM]
            s = lax.dot_general(q_h, k_h, (((1,), (1,)), ((), ())), preferred_element_type=F32) * scale
            s = jnp.where(vis, s, NEG_BIG)
            sk = sink_ref[h]
            mx = jnp.maximum(jnp.max(s, axis=-1, keepdims=True), sk)
            e = jnp.exp(s - mx)
            denom = jnp.sum(e, axis=-1, keepdims=True) + jnp.exp(sk - mx)
            p = (e / denom).astype(BF16)
            o_h = jnp.dot(p, v_h, preferred_element_type=F32)
            o_ref[:, h * SWA_HEAD_DIM:(h + 1) * SWA_HEAD_DIM] = o_h.astype(o_ref.dtype)


def _swa(z, sinks, batch, seq):
    blk = ATTN_BLOCK
    qw = SWA_HEADS * SWA_HEAD_DIM
    kvw2 = 2 * SWA_KV_HEADS * SWA_HEAD_DIM
    q_blk = POOL_WIDTH // qw
    kv_blk = (POOL_WIDTH + qw) // kvw2
    return pl.pallas_call(
        _swa_kernel,
        grid=(batch, seq // blk),
        in_specs=[
            pl.BlockSpec(memory_space=pltpu.SMEM),
            pl.BlockSpec((None, blk, qw), lambda b, n: (b, n, q_blk)),
            pl.BlockSpec((None, blk, kvw2), lambda b, n: (b, jnp.maximum(n - 1, 0), kv_blk)),
            pl.BlockSpec((None, blk, kvw2), lambda b, n: (b, n, kv_blk)),
        ],
        out_specs=pl.BlockSpec((None, blk, qw), lambda b, n: (b, n, 0)),
        out_shape=jax.ShapeDtypeStruct((batch, seq, qw), BF16),
        compiler_params=_params("parallel", "arbitrary"),
        name="swa",
    )(sinks, z, z, z)


def _pool_kernel(up_ref, uc_ref, w_ref, scale_ref, o_ref, *, ts):
    i = pl.program_id(1)
    has_prev = (i > 0).astype(F32)
    pos = i * ts + lax.broadcasted_iota(jnp.int32, (ts, 1), 0)
    for gi, win in enumerate(POOL_WINDOWS):
        sl = slice(gi * POOL_GROUP, (gi + 1) * POOL_GROUP)
        cur = uc_ref[:, sl].astype(F32)
        prev = up_ref[:, sl].astype(F32) * has_prev
        acc = jnp.concatenate([prev, cur], axis=0)
        d = 1
        while d < win:
            acc = acc + pltpu.roll(acc, d, 0)
            d *= 2
        wsum = acc[POOL_HALO:, :]
        count = jnp.minimum(pos + 1, win).astype(F32)
        zz = (wsum / count - cur).astype(BF16)
        a = jnp.dot(zz, w_ref[gi], preferred_element_type=F32) * scale_ref[:, sl]
        o_ref[:, sl] = a.astype(o_ref.dtype)


def _pool(z, w_pool, pool_scale, batch, seq):
    ts = min(256, seq)
    halo_blocks = ts // POOL_HALO
    kern = functools.partial(_pool_kernel, ts=ts)
    return pl.pallas_call(
        kern,
        grid=(batch, seq // ts),
        in_specs=[
            pl.BlockSpec((None, POOL_HALO, POOL_WIDTH), lambda b, i: (b, jnp.maximum(i * halo_blocks - 1, 0), 0)),
            pl.BlockSpec((None, ts, POOL_WIDTH), lambda b, i: (b, i, 0)),
            pl.BlockSpec((len(POOL_WINDOWS), POOL_GROUP, POOL_GROUP), lambda b, i: (0, 0, 0)),
            pl.BlockSpec((1, POOL_WIDTH), lambda b, i: (0, 0)),
        ],
        out_specs=pl.BlockSpec((None, ts, POOL_WIDTH), lambda b, i: (b, i, 0)),
        out_shape=jax.ShapeDtypeStruct((batch, seq, POOL_WIDTH), BF16),
        compiler_params=_params("parallel", "arbitrary"),
        name="pool",
    )(z, z, w_pool, pool_scale)


def _even_mixer(x, g_pre, g_post, w_in, w_out, w_pool, pool_scale, sinks, tables, batch, seq):
    m = batch * seq
    q_lo = POOL_WIDTH // PROJ_TILE
    rope_hi = (POOL_WIDTH + SWA_HEADS * SWA_HEAD_DIM) // PROJ_TILE + 1
    z = _proj(x, g_pre, w_in, tables, seq, q_lo, rope_hi)
    z3 = z.reshape(batch, seq, z.shape[1])
    o = _swa(z3, sinks, batch, seq)
    a = _pool(z3, w_pool, pool_scale, batch, seq)
    return _outproj(a.reshape(m, -1), o.reshape(m, -1), w_out, x, g_post)


S5_SLAB = LANES
S5_SLABS = S5_WIDTH // S5_SLAB
S5_SLAB_STATE = (S5_SLAB // S5_GROUP_CH) * S5_STATE
S5_SHIFTS = (1, 2, 4)


def _gelu(y):
    return 0.5 * y * (1.0 + jnp.tanh(math.sqrt(2.0 / math.pi) * (y + 0.044715 * (y * y * y))))


def _cmul_add(vre, vim, are, aim, sre, sim):
    return vre + are * sre - aim * sim, vim + are * sim + aim * sre


def _s5_kernel(u_ref, b_ref, tab_ref, c_ref, d_ref, gw_ref, gb_ref, o_ref, carry_ref, x_ref, y_ref, *, ts):
    i = pl.program_id(1)
    ns = S5_SLAB_STATE

    @pl.when(i == 0)
    def _():
        carry_ref[...] = jnp.zeros_like(carry_ref)

    for j in range(S5_SLABS):
        ch = slice(j * S5_SLAB, (j + 1) * S5_SLAB)
        u_j = u_ref[:, ch]
        x_ref[...] = jnp.dot(u_j, b_ref[j], preferred_element_type=F32)
        tabs = [tab_ref[j, k] for k in range(8)]
        p_re, p_im = tabs[6], tabs[7]

        def tile_body(t, carry):
            c_re, c_im = carry
            r = pl.multiple_of(t * SUBLANES, SUBLANES)
            v_re = x_ref[pl.ds(r, SUBLANES), 0:ns]
            v_im = x_ref[pl.ds(r, SUBLANES), ns:2 * ns]
            for k, d in enumerate(S5_SHIFTS):
                s_re = pltpu.roll(v_re, d, 0)
                s_im = pltpu.roll(v_im, d, 0)
                v_re, v_im = _cmul_add(v_re, v_im, tabs[2 * k], tabs[2 * k + 1], s_re, s_im)
            v_re, v_im = _cmul_add(v_re, v_im, p_re, p_im, c_re, c_im)
            x_ref[pl.ds(r, SUBLANES), 0:ns] = v_re
            x_ref[pl.ds(r, SUBLANES), ns:2 * ns] = v_im
            last = SUBLANES - 1
            return (jnp.broadcast_to(v_re[last:last + 1, :], (SUBLANES, ns)),
                    jnp.broadcast_to(v_im[last:last + 1, :], (SUBLANES, ns)))

        c_re, c_im = lax.fori_loop(0, ts // SUBLANES, tile_body,
                                   (carry_ref[j, :, 0:ns], carry_ref[j, :, ns:2 * ns]))
        carry_ref[j, :, 0:ns] = c_re
        carry_ref[j, :, ns:2 * ns] = c_im
        y_j = jnp.dot(x_ref[...].astype(BF16), c_ref[j], preferred_element_type=F32)
        y_ref[:, ch] = _gelu(y_j + d_ref[:, ch] * u_j.astype(F32))

    y = y_ref[...]
    gate = jnp.dot(y.astype(BF16), gw_ref[...], preferred_element_type=F32) + gb_ref[...]
    o_ref[...] = (y * jax.nn.sigmoid(gate)).astype(o_ref.dtype)


def _s5(z, p, batch, seq):
    ts = min(256, seq)
    kern = functools.partial(_s5_kernel, ts=ts)
    full = lambda shape: pl.BlockSpec(shape, lambda b, i: (0,) * len(shape))
    return pl.pallas_call(
        kern,
        grid=(batch, seq // ts),
        in_specs=[
            pl.BlockSpec((None, ts, S5_WIDTH), lambda b, i: (b, i, 0)),
            full((S5_SLABS, S5_SLAB, 2 * S5_SLAB_STATE)),
            full((S5_SLABS, 8, SUBLANES, S5_SLAB_STATE)),
            full((S5_SLABS, 2 * S5_SLAB_STATE, S5_SLAB)),
            full((1, S5_WIDTH)),
            full((S5_WIDTH, S5_WIDTH)),
            full((1, S5_WIDTH)),
        ],
        out_specs=pl.BlockSpec((None, ts, S5_WIDTH), lambda b, i: (b, i, 0)),
        out_shape=jax.ShapeDtypeStruct((batch, seq, S5_WIDTH), BF16),
        scratch_shapes=[
            pltpu.VMEM((S5_SLABS, SUBLANES, 2 * S5_SLAB_STATE), F32),
            pltpu.VMEM((ts, 2 * S5_SLAB_STATE), F32),
            pltpu.VMEM((ts, S5_WIDTH), F32),
        ],
        compiler_params=_params("parallel", "arbitrary"),
        name="s5",
    )(z, p["s5_b"], p["s5_tab"], p["s5_c"], p["s5_d"], p["glu_w"], p["glu_b"])


def _s5_params(a_re, a_im, log_dt, b_re, b_im, c_re, c_im, d_skip, glu_w, glu_b):
    lam_re = a_re.astype(F32)
    lam_im = a_im.astype(F32)
    dt = jnp.exp(log_dt.astype(F32))[:, None]
    decay = jnp.exp(lam_re * dt)
    abar_re = decay * jnp.cos(lam_im * dt)
    abar_im = decay * jnp.sin(lam_im * dt)
    inv_mag = 1.0 / (lam_re * lam_re + lam_im * lam_im)
    num_re = abar_re - 1.0
    f_re = (num_re * lam_re + abar_im * lam_im) * inv_mag
    f_im = (abar_im * lam_re - num_re * lam_im) * inv_mag
    br = b_re.astype(F32)
    bi = b_im.astype(F32)
    bbar_re = f_re[..., None] * br - f_im[..., None] * bi
    bbar_im = f_re[..., None] * bi + f_im[..., None] * br
    gps = S5_SLAB // S5_GROUP_CH
    eye = jnp.eye(gps, dtype=F32)

    def b_slab(t):
        t = t.reshape(S5_SLABS, gps, S5_STATE, S5_GROUP_CH)
        t = jnp.einsum('jgnh,gk->jghkn', t, eye)
        return t.reshape(S5_SLABS, S5_SLAB, S5_SLAB_STATE)

    def c_slab(t):
        t = t.astype(F32).reshape(S5_SLABS, gps, S5_GROUP_CH, S5_STATE)
        t = jnp.einsum('jghn,gk->jgnkh', t, eye)
        return t.reshape(S5_SLABS, S5_SLAB_STATE, S5_SLAB)

    b_mat = jnp.concatenate([b_slab(bbar_re), b_slab(bbar_im)], axis=2).astype(BF16)
    c_mat = jnp.concatenate([c_slab(c_re), -c_slab(c_im)], axis=1).astype(BF16)

    ar = abar_re.reshape(S5_SLABS, 1, S5_SLAB_STATE)
    ai = abar_im.reshape(S5_SLABS, 1, S5_SLAB_STATE)
    row = jnp.arange(SUBLANES)[None, :, None]
    tabs = []
    pw_re, pw_im = ar, ai
    pows = []
    for d in S5_SHIFTS:
        tabs.append(jnp.where(row >= d, pw_re, 0.0))
        tabs.append(jnp.where(row >= d, pw_im, 0.0))
        pows.append((pw_re, pw_im))
        pw_re, pw_im = pw_re * pw_re - pw_im * pw_im, 2.0 * pw_re * pw_im
    pows.append((pw_re, pw_im))
    p_re = jnp.ones((S5_SLABS, SUBLANES, S5_SLAB_STATE), F32)
    p_im = jnp.zeros((S5_SLABS, SUBLANES, S5_SLAB_STATE), F32)
    for bit, (q_re, q_im) in zip((1, 2, 4, 8), pows):
        use = ((row + 1) & bit) != 0
        n_re = p_re * q_re - p_im * q_im
        n_im = p_re * q_im + p_im * q_re
        p_re = jnp.where(use, n_re, p_re)
        p_im = jnp.where(use, n_im, p_im)
    tabs += [p_re, p_im]
    tab = jnp.stack([jnp.broadcast_to(t, p_re.shape) for t in tabs], axis=1)
    return {
        "s5_b": b_mat, "s5_c": c_mat, "s5_tab": tab,
        "s5_d": d_skip.astype(F32).reshape(1, S5_WIDTH),
        "glu_w": glu_w.astype(BF16), "glu_b": glu_b.astype(F32).reshape(1, S5_WIDTH),
    }


NSA_CMP_SLOTS = 128
NSA_Q_TILE = 128
NSA_K_TILE = 128
ODD_Q = S5_WIDTH // LANES
ODD_KC = ODD_Q + NSA_HEADS
ODD_KS = ODD_KC + NSA_KV_HEADS
ODD_KW = ODD_KS + NSA_KV_HEADS
ODD_VC = ODD_KW + NSA_KV_HEADS
ODD_VS = ODD_VC + NSA_KV_HEADS
ODD_VW = ODD_VS + NSA_KV_HEADS
ODD_GL = ODD_VW + NSA_KV_HEADS
ODD_END = ODD_GL + NSA_KV_HEADS
ODD_COLS = -(-ODD_END * LANES // PROJ_TILE) * PROJ_TILE


def _compress_kernel(x_ref, w1a_ref, w1b_ref, pos_ref, b1_ref, w2_ref, o_ref):
    half = NSA_CMP_STRIDE * NSA_HEAD_DIM
    xc = x_ref[...].astype(F32)
    xa = (xc + pos_ref[:, 0:half]).astype(BF16)
    xb = (xc + pos_ref[:, half:2 * half]).astype(BF16)
    p1 = jnp.dot(xa, w1a_ref[...], preferred_element_type=F32)
    p2 = jnp.dot(xb, w1b_ref[...], preferred_element_type=F32)
    n = p2.shape[0]
    rows = lax.broadcasted_iota(jnp.int32, p2.shape, 0)
    p2_next = jnp.where(rows < n - 1, pltpu.roll(p2, n - 1, 0), 0.0)
    h = _gelu(p1 + p2_next + b1_ref[...])
    o_ref[...] = jnp.dot(h.astype(BF16), w2_ref[...], preferred_element_type=F32).astype(o_ref.dtype)


def _compress(z, p, batch, seq):
    nchunk = seq // NSA_CMP_STRIDE
    kc = z[:, :, ODD_KC * LANES:(ODD_KC + NSA_KV_HEADS) * LANES]
    vc = z[:, :, ODD_VC * LANES:(ODD_VC + NSA_KV_HEADS) * LANES]
    xc = jnp.stack([kc, vc], axis=1).reshape(batch, 2, nchunk, NSA_CMP_STRIDE, NSA_KV_HEADS, NSA_HEAD_DIM)
    xc = xc.transpose(0, 1, 4, 2, 3, 5).reshape(batch, 2, NSA_KV_HEADS, nchunk, NSA_CMP_STRIDE * NSA_HEAD_DIM)
    half = NSA_CMP_STRIDE * NSA_HEAD_DIM
    kind = lambda shape: pl.BlockSpec((None,) + shape, lambda b, c, h: (c,) + (0,) * len(shape))
    return pl.pallas_call(
        _compress_kernel,
        grid=(batch, 2, NSA_KV_HEADS),
        in_specs=[
            pl.BlockSpec((None, None, None, nchunk, half), lambda b, c, h: (b, c, h, 0, 0)),
            kind((half, NSA_HEAD_DIM)), kind((half, NSA_HEAD_DIM)),
            kind((1, 2 * half)), kind((1, NSA_HEAD_DIM)), kind((NSA_HEAD_DIM, NSA_HEAD_DIM)),
        ],
        out_specs=pl.BlockSpec((None, None, None, nchunk, NSA_HEAD_DIM), lambda b, c, h: (b, c, h, 0, 0)),
        out_shape=jax.ShapeDtypeStruct((batch, 2, NSA_KV_HEADS, nchunk, NSA_HEAD_DIM), BF16),
        compiler_params=_params("parallel", "arbitrary", "arbitrary"),
        name="compress",
    )(xc, p["cmp_w1a"], p["cmp_w1b"], p["cmp_pos"], p["cmp_b1"], p["cmp_w2"])


def _nt_dot(a, b):
    return lax.dot_general(a, b, (((1,), (1,)), ((), ())), preferred_element_type=F32)


def _nsa_kernel(q_ref, kc_ref, vc_ref, ks_ref, vs_ref, kw_ref, vw_ref, gl_ref, o_ref, m_ref, l_ref, acc_ref, *, ncmp):
    i = pl.program_id(2)
    tq, tk, grp, d = NSA_Q_TILE, NSA_K_TILE, NSA_GROUP, NSA_HEAD_DIM
    rows = grp * tq
    scale = d ** -0.5
    t0 = i * tq
    q = jnp.concatenate([q_ref[:, g * d:(g + 1) * d] for g in range(grp)], axis=0)
    t_q = t0 + lax.broadcasted_iota(jnp.int32, (tq, 1), 0)
    t_all = jnp.concatenate([t_q] * grp, axis=0)
    lane = lax.broadcasted_iota(jnp.int32, (1, ncmp), 1)

    s = _nt_dot(q, kc_ref[...]) * scale
    vis = (lane * NSA_CMP_STRIDE + (NSA_CMP_BLOCK - 1)) <= t_all
    s = jnp.where(vis, s, NEG_BIG)
    mx = jnp.max(s, axis=-1, keepdims=True)
    e = jnp.where(vis, jnp.exp(s - mx), 0.0)
    p_cmp = e / jnp.maximum(jnp.sum(e, axis=-1, keepdims=True), 1.0)
    o_cmp = jnp.dot(p_cmp.astype(BF16), vc_ref[...], preferred_element_type=F32)

    p_sum = p_cmp[0:tq]
    for g in range(1, grp):
        p_sum = p_sum + p_cmp[g * tq:(g + 1) * tq]
    ci = lax.broadcasted_iota(jnp.int32, (ncmp, ncmp), 0) * NSA_CMP_STRIDE
    sj = lax.broadcasted_iota(jnp.int32, (ncmp, ncmp), 1) * NSA_SLC_BLOCK
    overlap = jnp.where((ci < sj + NSA_SLC_BLOCK) & (ci + NSA_CMP_BLOCK > sj), 1.0, 0.0).astype(BF16)
    p_hi = p_sum.astype(BF16)
    p_lo = (p_sum - p_hi.astype(F32)).astype(BF16)
    imp = (jnp.dot(p_hi, overlap, preferred_element_type=F32)
           + jnp.dot(p_lo, overlap, preferred_element_type=F32))

    blk = lax.broadcasted_iota(jnp.int32, (tq, ncmp), 1)
    cur = t_q // NSA_SLC_BLOCK
    forced = (blk == 0) | (blk == cur) | (blk == cur - 1)
    score = jnp.where(blk > cur, -jnp.inf, jnp.where(forced, jnp.inf, imp))
    sel = jnp.zeros((tq, ncmp), F32)
    for _ in range(NSA_TOP_N):
        best = jnp.max(score, axis=-1, keepdims=True)
        pick = jnp.min(jnp.where(score == best, blk, ncmp), axis=-1, keepdims=True)
        hit = blk == pick
        sel = jnp.where(hit, 1.0, sel)
        score = jnp.where(hit, -jnp.inf, score)
    sel = sel.astype(BF16)

    def flash_init():
        m_ref[...] = jnp.full_like(m_ref, NEG_BIG)
        l_ref[...] = jnp.zeros_like(l_ref)
        acc_ref[...] = jnp.zeros_like(acc_ref)

    def flash_step(k_t, v_t, mask_q):
        mask = jnp.concatenate([jnp.where(mask_q, 1.0, 0.0)] * grp, axis=0) > 0.5
        sc = jnp.where(mask, _nt_dot(q, k_t) * scale, NEG_BIG)
        m_old = m_ref[...]
        m_new = jnp.maximum(m_old, jnp.max(sc, axis=-1, keepdims=True))
        alpha = jnp.exp(m_old - m_new)
        pr = jnp.where(mask, jnp.exp(sc - m_new), 0.0)
        l_ref[...] = alpha * l_ref[...] + jnp.sum(pr, axis=-1, keepdims=True)
        acc_ref[...] = alpha * acc_ref[...] + jnp.dot(pr.astype(BF16), v_t, preferred_element_type=F32)
        m_ref[...] = m_new

    kcol = lax.broadcasted_iota(jnp.int32, (1, tk), 1)
    brow = lax.broadcasted_iota(jnp.int32, (ncmp, tk), 0)

    flash_init()

    def slc_body(jt, carry):
        k0 = pl.multiple_of(jt * tk, tk)
        kpos = k0 + kcol
        expand = jnp.where((kpos // NSA_SLC_BLOCK) == brow, 1.0, 0.0).astype(BF16)
        chosen = jnp.dot(sel, expand, preferred_element_type=F32) > 0.5
        mask = chosen & (kpos <= t_q)
        flash_step(ks_ref[pl.ds(k0, tk), :], vs_ref[pl.ds(k0, tk), :], mask)
        return carry

    lax.fori_loop(0, i + 1, slc_body, 0)
    o_slc = acc_ref[...] / l_ref[...]

    flash_init()

    def win_body(jt, carry):
        k0 = pl.multiple_of(jt * tk, tk)
        kpos = k0 + kcol
        mask = (kpos <= t_q) & (kpos > t_q - NSA_WINDOW)
        flash_step(kw_ref[pl.ds(k0, tk), :], vw_ref[pl.ds(k0, tk), :], mask)
        return carry

    lax.fori_loop(jnp.maximum(i - NSA_WINDOW // tk, 0), i + 1, win_body, 0)
    o_win = acc_ref[...] / l_ref[...]

    gates = jax.nn.sigmoid(gl_ref[...].astype(F32))
    for g in range(grp):
        rs = slice(g * tq, (g + 1) * tq)
        o_g = (gates[:, 3 * g:3 * g + 1] * o_cmp[rs]
               + gates[:, 3 * g + 1:3 * g + 2] * o_slc[rs]
               + gates[:, 3 * g + 2:3 * g + 3] * o_win[rs])
        o_ref[:, g * d:(g + 1) * d] = o_g.astype(o_ref.dtype)


def _nsa(z, kvc, batch, seq):
    tq, d, grp = NSA_Q_TILE, NSA_HEAD_DIM, NSA_GROUP
    ncmp = kvc.shape[3]
    kern = functools.partial(_nsa_kernel, ncmp=ncmp)
    qw = grp * d

    def seq_spec(col0):
        return pl.BlockSpec((None, seq, d), lambda b, h, i: (b, 0, col0 + h))

    def cmp_spec(c):
        return pl.BlockSpec((None, None, None, ncmp, d), lambda b, h, i: (b, c, h, 0, 0))

    return pl.pallas_call(
        kern,
        grid=(batch, NSA_KV_HEADS, seq // tq),
        in_specs=[
            pl.BlockSpec((None, tq, qw), lambda b, h, i: (b, i, ODD_Q * LANES // qw + h)),
            cmp_spec(0), cmp_spec(1),
            seq_spec(ODD_KS), seq_spec(ODD_VS), seq_spec(ODD_KW), seq_spec(ODD_VW),
            pl.BlockSpec((None, tq, LANES), lambda b, h, i: (b, i, ODD_GL + h)),
        ],
        out_specs=pl.BlockSpec((None, tq, qw), lambda b, h, i: (b, i, h)),
        out_shape=jax.ShapeDtypeStruct((batch, seq, NSA_HEADS * d), BF16),
        scratch_shapes=[
            pltpu.VMEM((grp * tq, 1), F32),
            pltpu.VMEM((grp * tq, 1), F32),
            pltpu.VMEM((grp * tq, d), F32),
        ],
        compiler_params=_params("parallel", "parallel", "arbitrary"),
        name="nsa",
    )(z, kvc, kvc, z, z, z, z, z)


def _odd_params(w_in, w_out, a_re, a_im, log_dt, b_re, b_im, c_re, c_im, d_skip, glu_w, glu_b,
                cmp_pos, cmp_w1, cmp_b1, cmp_w2, seq):
    kvw = NSA_KV_WIDTH
    qw = NSA_HEADS * NSA_HEAD_DIM
    o_q = S5_WIDTH
    o_kv = o_q + qw
    parts = {name: w_in[:, o_kv + n * kvw:o_kv + (n + 1) * kvw]
             for n, name in enumerate(("kc", "vc", "ks", "vs", "kw", "vw"))}
    w_gl = w_in[:, o_kv + 6 * kvw:].reshape(D_MODEL, NSA_KV_HEADS, 3 * NSA_GROUP)
    w_gl = jnp.pad(w_gl, ((0, 0), (0, 0), (0, LANES - 3 * NSA_GROUP))).reshape(D_MODEL, NSA_KV_HEADS * LANES)
    w = jnp.concatenate([w_in[:, :o_kv], parts["kc"], parts["ks"], parts["kw"],
                         parts["vc"], parts["vs"], parts["vw"], w_gl], axis=1).astype(BF16)
    w = jnp.pad(w, ((0, 0), (0, ODD_COLS - w.shape[1])))
    half = NSA_CMP_STRIDE * NSA_HEAD_DIM
    w1 = cmp_w1.astype(BF16).reshape(2, NSA_CMP_BLOCK * NSA_HEAD_DIM, NSA_HEAD_DIM)
    p = _s5_params(a_re, a_im, log_dt, b_re, b_im, c_re, c_im, d_skip, glu_w, glu_b)
    p.update({
        "w_in": w, "w_out": w_out.astype(BF16),
        "cmp_w1a": w1[:, :half], "cmp_w1b": w1[:, half:],
        "cmp_pos": cmp_pos.astype(F32).reshape(2, 1, NSA_CMP_BLOCK * NSA_HEAD_DIM),
        "cmp_b1": cmp_b1.astype(F32).reshape(2, 1, NSA_HEAD_DIM),
        "cmp_w2": cmp_w2.astype(BF16),
        "tables": _rope_tables(seq, NSA_HEAD_DIM, (ODD_VC * LANES) % PROJ_TILE),
    })
    return p


def _odd_mixer(x, g_pre, g_post, p, batch, seq):
    m = batch * seq
    rope_lo = ODD_Q * LANES // PROJ_TILE
    rope_hi = -(-ODD_VC * LANES // PROJ_TILE)
    z = _proj(x, g_pre, p["w_in"], p["tables"], seq, rope_lo, rope_hi)
    z3 = z.reshape(batch, seq, z.shape[1])
    s5_out = _s5(z3, p, batch, seq)
    kvc = _compress(z3, p, batch, seq)
    o = _nsa(z3, kvc, batch, seq)
    return _outproj(s5_out.reshape(m, -1), o.reshape(m, -1), p["w_out"], x, g_post)


def _pad_ff(w, axis):
    pad = [(0, 0), (0, 0)]
    pad[axis] = (0, D_FF_PAD - D_FF)
    return jnp.pad(w.astype(BF16), pad)


def kernel(x, norm_gains, ffn1_w_gate, ffn1_w_up, ffn1_w_down, ffn2_w_gate, ffn2_w_up, ffn2_w_down, ev_w_in, ev_w_out, pool_w, pool_scale, swa_sinks, od_w_in, od_w_out, s5_a_re, s5_a_im, s5_log_dt, s5_b_re, s5_b_im, s5_c_re, s5_c_im, s5_d, s5_glu_w, s5_glu_b, nsa_cmp_pos, nsa_cmp_w1, nsa_cmp_b1, nsa_cmp_w2):
    batch, seq, _ = x.shape
    m = batch * seq
    depth = norm_gains.shape[0]
    xs = x.reshape(m, D_MODEL)
    even_tables = _rope_tables(seq, SWA_HEAD_DIM, SWA_KV_HEADS * SWA_HEAD_DIM)
    for layer in range(depth):
        g = norm_gains[layer].astype(F32).reshape(6, 1, D_MODEL)
        i = layer // 2
        xs = _ffn(xs, g[0], _pad_ff(ffn1_w_gate[layer], 1), _pad_ff(ffn1_w_up[layer], 1),
                  _pad_ff(ffn1_w_down[layer], 0), g[1])
        if layer % 2 == 0:
            xs = _even_mixer(xs, g[2], g[3], ev_w_in[i].astype(BF16), ev_w_out[i].astype(BF16),
                             pool_w[i].astype(BF16), pool_scale[i].astype(F32).reshape(1, POOL_WIDTH),
                             swa_sinks[i].astype(F32), even_tables, batch, seq)
        else:
            p = _odd_params(od_w_in[i], od_w_out[i], s5_a_re[i], s5_a_im[i], s5_log_dt[i], s5_b_re[i], s5_b_im[i],
                            s5_c_re[i], s5_c_im[i], s5_d[i], s5_glu_w[i], s5_glu_b[i], nsa_cmp_pos[i],
                            nsa_cmp_w1[i], nsa_cmp_b1[i], nsa_cmp_w2[i], seq)
            xs = _odd_mixer(xs, g[2], g[3], p, batch, seq)
        xs = _ffn(xs, g[4], _pad_ff(ffn2_w_gate[layer], 1), _pad_ff(ffn2_w_up[layer], 1),
                  _pad_ff(ffn2_w_down[layer], 0), g[5])
    return xs.reshape(batch, seq, D_MODEL)
```

```python
import functools
import math

import numpy as np
import jax
import jax.numpy as jnp
from jax import lax
from jax.experimental import pallas as pl
from jax.experimental.pallas import tpu as pltpu

F32 = jnp.float32
BF16 = jnp.bfloat16

D_MODEL = 4096
D_FF = 5504
NORM_EPS = 1e-6
ROPE_THETA = 500000.0
ROPE_FRACTION = 4
POOL_WINDOWS = (2, 4, 8, 16)
POOL_WIDTH = D_MODEL // 2
POOL_GROUP = POOL_WIDTH // len(POOL_WINDOWS)
POOL_HALO = 16
SWA_HEAD_DIM = 64
SWA_HEADS = 32
SWA_KV_HEADS = 4
SWA_GROUP = SWA_HEADS // SWA_KV_HEADS
SWA_WINDOW = 128
ATTN_BLOCK = 128
S5_WIDTH = D_MODEL // 4
S5_GROUP_CH = 16
S5_GROUPS = S5_WIDTH // S5_GROUP_CH
S5_STATE = 64
NSA_HEAD_DIM = 128
NSA_HEADS = 24
NSA_KV_HEADS = 6
NSA_GROUP = 4
NSA_CMP_BLOCK = 32
NSA_CMP_STRIDE = 16
NSA_SLC_BLOCK = 64
NSA_TOP_N = 8
NSA_WINDOW = 512
NSA_KV_WIDTH = NSA_KV_HEADS * NSA_HEAD_DIM

LANES = 128
SUBLANES = 8
MXU_DIM = 256
VMEM_LIMIT_BYTES = 56 * 1024 * 1024

ROW_TILE = 512
FF_TILE = 256
D_FF_PAD = 5632
PROJ_TILE = 512
OUT_K_TILE = 512
NORM_ROWS = 32
CAST_ROWS = 256
NEG_BIG = -1e30


def _params(*sem):
    return pltpu.CompilerParams(dimension_semantics=sem, vmem_limit_bytes=VMEM_LIMIT_BYTES)


def _norm_rows(dst_ref, src_ref, g_ref, rows, res_ref=None, res_scale=1.0):
    g = g_ref[...]

    def body(c, carry):
        r = pl.multiple_of(c * NORM_ROWS, NORM_ROWS)
        v = src_ref[pl.ds(r, NORM_ROWS), :].astype(F32)
        ms = jnp.mean(v * v, axis=-1, keepdims=True)
        y = v * lax.rsqrt(ms + NORM_EPS) * g
        if res_ref is not None:
            y = res_ref[pl.ds(r, NORM_ROWS), :] + res_scale * y
        dst_ref[pl.ds(r, NORM_ROWS), :] = y.astype(dst_ref.dtype)
        return carry

    lax.fori_loop(0, rows // NORM_ROWS, body, 0, unroll=2)


def _cast_kernel(x_ref, o_ref, *, tr, rows_in, cols_in):
    x = x_ref[...]
    if rows_in % tr:
        r = pl.program_id(1) * tr + lax.broadcasted_iota(jnp.int32, (tr, 1), 0)
        x = jnp.where(r < rows_in, x, 0.0)
    if cols_in == o_ref.shape[1]:
        o_ref[...] = x.astype(o_ref.dtype)
    else:
        o_ref[:, :cols_in] = x.astype(o_ref.dtype)
        o_ref[:, cols_in:] = jnp.zeros((tr, o_ref.shape[1] - cols_in), o_ref.dtype)


def _cast_bf16(w, rows_out=None, cols_out=None):
    nl, rows_in, cols_in = w.shape
    rows_out = rows_out or rows_in
    cols_out = cols_out or cols_in
    tr = min(CAST_ROWS, rows_out)
    kern = functools.partial(_cast_kernel, tr=tr, rows_in=rows_in, cols_in=cols_in)
    return pl.pallas_call(
        kern,
        grid=(nl, rows_out // tr),
        in_specs=[pl.BlockSpec((None, tr, cols_in), lambda l, i: (l, i, 0))],
        out_specs=pl.BlockSpec((None, tr, cols_out), lambda l, i: (l, i, 0)),
        out_shape=jax.ShapeDtypeStruct((nl, rows_out, cols_out), BF16),
        compiler_params=_params("parallel", "parallel"),
        name="cast",
    )(w)


def _ffn_kernel(x_ref, gpre_ref, wg_ref, wu_ref, wd_ref, gpost_ref, o_ref, h_ref, *, tm, nj):
    j = pl.program_id(1)

    @pl.when(j == 0)
    def _():
        _norm_rows(h_ref, x_ref, gpre_ref, tm)
        o_ref[...] = jnp.zeros_like(o_ref)

    h = h_ref[...]
    gate = jnp.dot(h, wg_ref[...], preferred_element_type=F32)
    up = jnp.dot(h, wu_ref[...], preferred_element_type=F32)
    act = (gate * jax.nn.sigmoid(gate) * up).astype(BF16)
    nc = 512
    for n in range(D_MODEL // nc):
        sl = slice(n * nc, (n + 1) * nc)
        o_ref[:, sl] += jnp.dot(act, wd_ref[:, sl], preferred_element_type=F32)

    @pl.when(j == nj - 1)
    def _():
        _norm_rows(o_ref, o_ref, gpost_ref, tm, res_ref=x_ref, res_scale=0.5)


def _ffn(x, g_pre, wg, wu, wd, g_post, layer):
    m = x.shape[0]
    tm = min(ROW_TILE, m)
    nj = D_FF_PAD // FF_TILE
    kern = functools.partial(_ffn_kernel, tm=tm, nj=nj)
    return pl.pallas_call(
        kern,
        grid=(m // tm, nj),
        in_specs=[
            pl.BlockSpec((tm, D_MODEL), lambda i, j: (i, 0)),
            pl.BlockSpec((1, D_MODEL), lambda i, j: (0, 0)),
            pl.BlockSpec((None, D_MODEL, FF_TILE), lambda i, j: (layer, 0, j)),
            pl.BlockSpec((None, D_MODEL, FF_TILE), lambda i, j: (layer, 0, j)),
            pl.BlockSpec((None, FF_TILE, D_MODEL), lambda i, j: (layer, j, 0)),
            pl.BlockSpec((1, D_MODEL), lambda i, j: (0, 0)),
        ],
        out_specs=pl.BlockSpec((tm, D_MODEL), lambda i, j: (i, 0)),
        out_shape=jax.ShapeDtypeStruct((m, D_MODEL), F32),
        scratch_shapes=[pltpu.VMEM((tm, D_MODEL), BF16)],
        compiler_params=_params("parallel", "arbitrary"),
        name="ffn",
    )(x, g_pre, wg, wu, wd, g_post)


def _proj_kernel(x_ref, g_ref, w_ref, cos_ref, sa_ref, sb_ref, o_ref, h_ref, *, tm, rope_lo, rope_hi, half):
    j = pl.program_id(1)

    @pl.when(j == 0)
    def _():
        _norm_rows(h_ref, x_ref, g_ref, tm)

    z = jnp.dot(h_ref[...], w_ref[...], preferred_element_type=F32)
    tn = z.shape[1]
    is_rope = jnp.logical_and(j >= rope_lo, j < rope_hi)

    @pl.when(is_rope)
    def _():
        zr = (z * cos_ref[...]
              + pltpu.roll(z, tn - half, 1) * sa_ref[...]
              + pltpu.roll(z, half, 1) * sb_ref[...])
        o_ref[...] = zr.astype(o_ref.dtype)

    @pl.when(jnp.logical_not(is_rope))
    def _():
        o_ref[...] = z.astype(o_ref.dtype)


def _rope_tables(seq, head_dim, n_rope_last):
    rot = head_dim // ROPE_FRACTION
    half = rot // 2
    inv_freq = jnp.power(ROPE_THETA, -jnp.arange(half, dtype=F32) * 2.0 / rot)
    ang = jnp.arange(seq, dtype=jnp.int32).astype(F32)[:, None] * inv_freq[None, :]
    cos, sin = jnp.cos(ang), jnp.sin(ang)
    one = jnp.ones((seq, head_dim - rot), F32)
    zero_h = jnp.zeros((seq, half), F32)
    zero_r = jnp.zeros((seq, head_dim - rot), F32)
    c_head = jnp.concatenate([cos, cos, one], axis=1)
    sa_head = jnp.concatenate([-sin, zero_h, zero_r], axis=1)
    sb_head = jnp.concatenate([zero_h, sin, zero_r], axis=1)
    reps = PROJ_TILE // head_dim
    col = jnp.arange(PROJ_TILE)[None, :]

    def kinds(t, ident):
        full = jnp.tile(t, (1, reps))
        part = jnp.where(col < n_rope_last, full, ident)
        return jnp.stack([full, part], axis=0)

    return kinds(c_head, 1.0), kinds(sa_head, 0.0), kinds(sb_head, 0.0), half


def _proj(x, g_pre, w, layer, tables, seq, rope_lo, rope_hi):
    cos_t, sa_t, sb_t, half = tables
    m = x.shape[0]
    n = w.shape[2]
    tm = min(ROW_TILE, seq)
    tn = PROJ_TILE
    sblocks = seq // tm
    kern = functools.partial(_proj_kernel, tm=tm, rope_lo=rope_lo, rope_hi=rope_hi, half=half)

    def tab_map(i, j):
        return (jnp.where(j == rope_hi - 1, 1, 0), i % sblocks, 0)

    tab_spec = pl.BlockSpec((None, tm, tn), tab_map)
    return pl.pallas_call(
        kern,
        grid=(m // tm, n // tn),
        in_specs=[
            pl.BlockSpec((tm, D_MODEL), lambda i, j: (i, 0)),
            pl.BlockSpec((1, D_MODEL), lambda i, j: (0, 0)),
            pl.BlockSpec((None, D_MODEL, tn), lambda i, j: (layer, 0, j)),
            tab_spec, tab_spec, tab_spec,
        ],
        out_specs=pl.BlockSpec((tm, tn), lambda i, j: (i, j)),
        out_shape=jax.ShapeDtypeStruct((m, n), BF16),
        scratch_shapes=[pltpu.VMEM((tm, D_MODEL), BF16)],
        compiler_params=_params("parallel", "arbitrary"),
        name="proj",
    )(x, g_pre, w, cos_t, sa_t, sb_t)


def _outproj_kernel(a1_ref, a2_ref, w_ref, x_ref, g_ref, o_ref, *, tm, nk1, nk):
    k = pl.program_id(1)

    @pl.when(k == 0)
    def _():
        o_ref[...] = jnp.zeros_like(o_ref)

    def accumulate(a):
        nc = 512
        for n in range(D_MODEL // nc):
            sl = slice(n * nc, (n + 1) * nc)
            o_ref[:, sl] += jnp.dot(a, w_ref[:, sl], preferred_element_type=F32)

    @pl.when(k < nk1)
    def _():
        accumulate(a1_ref[...])

    @pl.when(k >= nk1)
    def _():
        accumulate(a2_ref[...])

    @pl.when(k == nk - 1)
    def _():
        _norm_rows(o_ref, o_ref, g_ref, tm, res_ref=x_ref, res_scale=1.0)


def _outproj(a1, a2, w, layer, x, g_post):
    m = x.shape[0]
    tm = min(ROW_TILE, m)
    tk = OUT_K_TILE
    nk1 = a1.shape[1] // tk
    nk = w.shape[1] // tk
    kern = functools.partial(_outproj_kernel, tm=tm, nk1=nk1, nk=nk)
    return pl.pallas_call(
        kern,
        grid=(m // tm, nk),
        in_specs=[
            pl.BlockSpec((tm, tk), lambda i, k: (i, jnp.minimum(k, nk1 - 1))),
            pl.BlockSpec((tm, tk), lambda i, k: (i, jnp.maximum(k - nk1, 0))),
            pl.BlockSpec((None, tk, D_MODEL), lambda i, k: (layer, k, 0)),
            pl.BlockSpec((tm, D_MODEL), lambda i, k: (i, 0)),
            pl.BlockSpec((1, D_MODEL), lambda i, k: (0, 0)),
        ],
        out_specs=pl.BlockSpec((tm, D_MODEL), lambda i, k: (i, 0)),
        out_shape=jax.ShapeDtypeStruct((m, D_MODEL), F32),
        compiler_params=_params("parallel", "arbitrary"),
        name="outproj",
    )(a1, a2, w, x, g_post)


def _swa_kernel(sink_ref, q_ref, kvp_ref, kvc_ref, o_ref):
    n = pl.program_id(1)
    blk = ATTN_BLOCK
    kvw = SWA_KV_HEADS * SWA_HEAD_DIM
    kv = jnp.concatenate([kvp_ref[...], kvc_ref[...]], axis=0)
    row = lax.broadcasted_iota(jnp.int32, (blk, 2 * blk), 0)
    col = lax.broadcasted_iota(jnp.int32, (blk, 2 * blk), 1)
    diff = row - col + blk
    vis = (diff >= 0) & (diff < SWA_WINDOW) & ((col >= blk) | (n > 0))
    scale = SWA_HEAD_DIM ** -0.5
    for kh in range(SWA_KV_HEADS):
        k_h = kv[:, kh * SWA_HEAD_DIM:(kh + 1) * SWA_HEAD_DIM]
        v_h = kv[:, kvw + kh * SWA_HEAD_DIM:kvw + (kh + 1) * SWA_HEAD_DIM]
        for g in range(SWA_GROUP):
            h = kh * SWA_GROUP + g
            q_h = q_ref[:, h * SWA_HEAD_DIM:(h + 1) * SWA_HEAD_DIM]
            s = lax.dot_general(q_h, k_h, (((1,), (1,)), ((), ())), preferred_element_type=F32) * scale
            s = jnp.where(vis, s, NEG_BIG)
            sk = sink_ref[h]
            mx = jnp.maximum(jnp.max(s, axis=-1, keepdims=True), sk)
            e = jnp.exp(s - mx)
            denom = jnp.sum(e, axis=-1, keepdims=True) + jnp.exp(sk - mx)
            p = (e / denom).astype(BF16)
            o_h = jnp.dot(p, v_h, preferred_element_type=F32)
            o_ref[:, h * SWA_HEAD_DIM:(h + 1) * SWA_HEAD_DIM] = o_h.astype(o_ref.dtype)


def _swa(z, sinks, batch, seq):
    blk = ATTN_BLOCK
    qw = SWA_HEADS * SWA_HEAD_DIM
    kvw2 = 2 * SWA_KV_HEADS * SWA_HEAD_DIM
    q_blk = POOL_WIDTH // qw
    kv_blk = (POOL_WIDTH + qw) // kvw2
    return pl.pallas_call(
        _swa_kernel,
        grid=(batch, seq // blk),
        in_specs=[
            pl.BlockSpec(memory_space=pltpu.SMEM),
            pl.BlockSpec((None, blk, qw), lambda b, n: (b, n, q_blk)),
            pl.BlockSpec((None, blk, kvw2), lambda b, n: (b, jnp.maximum(n - 1, 0), kv_blk)),
            pl.BlockSpec((None, blk, kvw2), lambda b, n: (b, n, kv_blk)),
        ],
        out_specs=pl.BlockSpec((None, blk, qw), lambda b, n: (b, n, 0)),
        out_shape=jax.ShapeDtypeStruct((batch, seq, qw), BF16),
        compiler_params=_params("parallel", "arbitrary"),
        name="swa",
    )(sinks, z, z, z)


def _pool_kernel(up_ref, uc_ref, w_ref, scale_ref, o_ref, *, ts):
    i = pl.program_id(1)
    has_prev = (i > 0).astype(F32)
    pos = i * ts + lax.broadcasted_iota(jnp.int32, (ts, 1), 0)
    for gi, win in enumerate(POOL_WINDOWS):
        sl = slice(gi * POOL_GROUP, (gi + 1) * POOL_GROUP)
        cur = uc_ref[:, sl].astype(F32)
        prev = up_ref[:, sl].astype(F32) * has_prev
        acc = jnp.concatenate([prev, cur], axis=0)
        d = 1
        while d < win:
            acc = acc + pltpu.roll(acc, d, 0)
            d *= 2
        wsum = acc[POOL_HALO:, :]
        count = jnp.minimum(pos + 1, win).astype(F32)
        zz = (wsum / count - cur).astype(BF16)
        a = jnp.dot(zz, w_ref[gi], preferred_element_type=F32) * scale_ref[:, sl]
        o_ref[:, sl] = a.astype(o_ref.dtype)


def _pool(z, w_pool, pool_scale, batch, seq):
    ts = min(256, seq)
    halo_blocks = ts // POOL_HALO
    kern = functools.partial(_pool_kernel, ts=ts)
    return pl.pallas_call(
        kern,
        grid=(batch, seq // ts),
        in_specs=[
            pl.BlockSpec((None, POOL_HALO, POOL_WIDTH), lambda b, i: (b, jnp.maximum(i * halo_blocks - 1, 0), 0)),
            pl.BlockSpec((None, ts, POOL_WIDTH), lambda b, i: (b, i, 0)),
            pl.BlockSpec((len(POOL_WINDOWS), POOL_GROUP, POOL_GROUP), lambda b, i: (0, 0, 0)),
            pl.BlockSpec((1, POOL_WIDTH), lambda b, i: (0, 0)),
        ],
        out_specs=pl.BlockSpec((None, ts, POOL_WIDTH), lambda b, i: (b, i, 0)),
        out_shape=jax.ShapeDtypeStruct((batch, seq, POOL_WIDTH), BF16),
        compiler_params=_params("parallel", "arbitrary"),
        name="pool",
    )(z, z, w_pool, pool_scale)


def _even_mixer(x, g_pre, g_post, w_in, w_out, layer, w_pool, pool_scale, sinks, tables, batch, seq):
    m = batch * seq
    q_lo = POOL_WIDTH // PROJ_TILE
    rope_hi = (POOL_WIDTH + SWA_HEADS * SWA_HEAD_DIM) // PROJ_TILE + 1
    z = _proj(x, g_pre, w_in, layer, tables, seq, q_lo, rope_hi)
    z3 = z.reshape(batch, seq, z.shape[1])
    o = _swa(z3, sinks, batch, seq)
    a = _pool(z3, w_pool, pool_scale, batch, seq)
    return _outproj(a.reshape(m, -1), o.reshape(m, -1), w_out, layer, x, g_post)


S5_SLAB = LANES
S5_SLABS = S5_WIDTH // S5_SLAB
S5_SLAB_STATE = (S5_SLAB // S5_GROUP_CH) * S5_STATE
S5_SHIFTS = (1, 2, 4)


def _gelu(y):
    return 0.5 * y * (1.0 + jnp.tanh(math.sqrt(2.0 / math.pi) * (y + 0.044715 * (y * y * y))))


def _cmul_add(vre, vim, are, aim, sre, sim):
    return vre + are * sre - aim * sim, vim + are * sim + aim * sre


def _s5_kernel(u_ref, b_ref, tab_ref, c_ref, d_ref, gw_ref, gb_ref, o_ref, carry_ref, x_ref, y_ref, *, ts):
    i = pl.program_id(1)
    ns = S5_SLAB_STATE

    @pl.when(i == 0)
    def _():
        carry_ref[...] = jnp.zeros_like(carry_ref)

    for j in range(S5_SLABS):
        ch = slice(j * S5_SLAB, (j + 1) * S5_SLAB)
        u_j = u_ref[:, ch]
        x_ref[...] = jnp.dot(u_j, b_ref[j], preferred_element_type=F32)
        tabs = [tab_ref[j, k] for k in range(8)]
        p_re, p_im = tabs[6], tabs[7]

        def tile_body(t, carry):
            c_re, c_im = carry
            r = pl.multiple_of(t * SUBLANES, SUBLANES)
            v_re = x_ref[pl.ds(r, SUBLANES), 0:ns]
            v_im = x_ref[pl.ds(r, SUBLANES), ns:2 * ns]
            for k, d in enumerate(S5_SHIFTS):
                s_re = pltpu.roll(v_re, d, 0)
                s_im = pltpu.roll(v_im, d, 0)
                v_re, v_im = _cmul_add(v_re, v_im, tabs[2 * k], tabs[2 * k + 1], s_re, s_im)
            v_re, v_im = _cmul_add(v_re, v_im, p_re, p_im, c_re, c_im)
            x_ref[pl.ds(r, SUBLANES), 0:ns] = v_re
            x_ref[pl.ds(r, SUBLANES), ns:2 * ns] = v_im
            last = SUBLANES - 1
            return (jnp.broadcast_to(v_re[last:last + 1, :], (SUBLANES, ns)),
                    jnp.broadcast_to(v_im[last:last + 1, :], (SUBLANES, ns)))

        c_re, c_im = lax.fori_loop(0, ts // SUBLANES, tile_body,
                                   (carry_ref[j, :, 0:ns], carry_ref[j, :, ns:2 * ns]))
        carry_ref[j, :, 0:ns] = c_re
        carry_ref[j, :, ns:2 * ns] = c_im
        y_j = jnp.dot(x_ref[...].astype(BF16), c_ref[j], preferred_element_type=F32)
        y_ref[:, ch] = _gelu(y_j + d_ref[:, ch] * u_j.astype(F32))

    y = y_ref[...]
    gate = jnp.dot(y.astype(BF16), gw_ref[...], preferred_element_type=F32) + gb_ref[...]
    o_ref[...] = (y * jax.nn.sigmoid(gate)).astype(o_ref.dtype)


def _s5(z, p, batch, seq):
    ts = min(256, seq)
    kern = functools.partial(_s5_kernel, ts=ts)
    full = lambda shape: pl.BlockSpec(shape, lambda b, i: (0,) * len(shape))
    return pl.pallas_call(
        kern,
        grid=(batch, seq // ts),
        in_specs=[
            pl.BlockSpec((None, ts, S5_WIDTH), lambda b, i: (b, i, 0)),
            full((S5_SLABS, S5_SLAB, 2 * S5_SLAB_STATE)),
            full((S5_SLABS, 8, SUBLANES, S5_SLAB_STATE)),
            full((S5_SLABS, 2 * S5_SLAB_STATE, S5_SLAB)),
            full((1, S5_WIDTH)),
            full((S5_WIDTH, S5_WIDTH)),
            full((1, S5_WIDTH)),
        ],
        out_specs=pl.BlockSpec((None, ts, S5_WIDTH), lambda b, i: (b, i, 0)),
        out_shape=jax.ShapeDtypeStruct((batch, seq, S5_WIDTH), BF16),
        scratch_shapes=[
            pltpu.VMEM((S5_SLABS, SUBLANES, 2 * S5_SLAB_STATE), F32),
            pltpu.VMEM((ts, 2 * S5_SLAB_STATE), F32),
            pltpu.VMEM((ts, S5_WIDTH), F32),
        ],
        compiler_params=_params("parallel", "arbitrary"),
        name="s5",
    )(z, p["s5_b"], p["s5_tab"], p["s5_c"], p["s5_d"], p["glu_w"], p["glu_b"])


def _s5_params(a_re, a_im, log_dt, b_re, b_im, c_re, c_im, d_skip, glu_w, glu_b):
    lam_re = a_re.astype(F32)
    lam_im = a_im.astype(F32)
    dt = jnp.exp(log_dt.astype(F32))[:, None]
    decay = jnp.exp(lam_re * dt)
    abar_re = decay * jnp.cos(lam_im * dt)
    abar_im = decay * jnp.sin(lam_im * dt)
    inv_mag = 1.0 / (lam_re * lam_re + lam_im * lam_im)
    num_re = abar_re - 1.0
    f_re = (num_re * lam_re + abar_im * lam_im) * inv_mag
    f_im = (abar_im * lam_re - num_re * lam_im) * inv_mag
    br = b_re.astype(F32)
    bi = b_im.astype(F32)
    bbar_re = f_re[..., None] * br - f_im[..., None] * bi
    bbar_im = f_re[..., None] * bi + f_im[..., None] * br
    gps = S5_SLAB // S5_GROUP_CH
    eye = jnp.eye(gps, dtype=F32)

    def b_slab(t):
        t = t.reshape(S5_SLABS, gps, S5_STATE, S5_GROUP_CH)
        t = jnp.einsum('jgnh,gk->jghkn', t, eye)
        return t.reshape(S5_SLABS, S5_SLAB, S5_SLAB_STATE)

    def c_slab(t):
        t = t.astype(F32).reshape(S5_SLABS, gps, S5_GROUP_CH, S5_STATE)
        t = jnp.einsum('jghn,gk->jgnkh', t, eye)
        return t.reshape(S5_SLABS, S5_SLAB_STATE, S5_SLAB)

    b_mat = jnp.concatenate([b_slab(bbar_re), b_slab(bbar_im)], axis=2).astype(BF16)
    c_mat = jnp.concatenate([c_slab(c_re), -c_slab(c_im)], axis=1).astype(BF16)

    ar = abar_re.reshape(S5_SLABS, 1, S5_SLAB_STATE)
    ai = abar_im.reshape(S5_SLABS, 1, S5_SLAB_STATE)
    row = jnp.arange(SUBLANES)[None, :, None]
    tabs = []
    pw_re, pw_im = ar, ai
    pows = []
    for d in S5_SHIFTS:
        tabs.append(jnp.where(row >= d, pw_re, 0.0))
        tabs.append(jnp.where(row >= d, pw_im, 0.0))
        pows.append((pw_re, pw_im))
        pw_re, pw_im = pw_re * pw_re - pw_im * pw_im, 2.0 * pw_re * pw_im
    pows.append((pw_re, pw_im))
    p_re = jnp.ones((S5_SLABS, SUBLANES, S5_SLAB_STATE), F32)
    p_im = jnp.zeros((S5_SLABS, SUBLANES, S5_SLAB_STATE), F32)
    for bit, (q_re, q_im) in zip((1, 2, 4, 8), pows):
        use = ((row + 1) & bit) != 0
        n_re = p_re * q_re - p_im * q_im
        n_im = p_re * q_im + p_im * q_re
        p_re = jnp.where(use, n_re, p_re)
        p_im = jnp.where(use, n_im, p_im)
    tabs += [p_re, p_im]
    tab = jnp.stack([jnp.broadcast_to(t, p_re.shape) for t in tabs], axis=1)
    return {
        "s5_b": b_mat, "s5_c": c_mat, "s5_tab": tab,
        "s5_d": d_skip.astype(F32).reshape(1, S5_WIDTH),
        "glu_w": glu_w.astype(BF16), "glu_b": glu_b.astype(F32).reshape(1, S5_WIDTH),
    }


NSA_CMP_SLOTS = 128
NSA_Q_TILE = 256
NSA_K_TILE = 256
LOG2_E = math.log2(math.e)
ODD_Q = S5_WIDTH // LANES
ODD_KC = ODD_Q + NSA_HEADS
ODD_KS = ODD_KC + NSA_KV_HEADS
ODD_KW = ODD_KS + NSA_KV_HEADS
ODD_VC = ODD_KW + NSA_KV_HEADS
ODD_VS = ODD_VC + NSA_KV_HEADS
ODD_VW = ODD_VS + NSA_KV_HEADS
ODD_GL = ODD_VW + NSA_KV_HEADS
ODD_END = ODD_GL + NSA_KV_HEADS
ODD_COLS = -(-ODD_END * LANES // PROJ_TILE) * PROJ_TILE


def _compress_kernel(x_ref, w1a_ref, w1b_ref, pos_ref, b1_ref, w2_ref, o_ref):
    half = NSA_CMP_STRIDE * NSA_HEAD_DIM
    xc = x_ref[...].astype(F32)
    xa = (xc + pos_ref[:, 0:half]).astype(BF16)
    xb = (xc + pos_ref[:, half:2 * half]).astype(BF16)
    p1 = jnp.dot(xa, w1a_ref[...], preferred_element_type=F32)
    p2 = jnp.dot(xb, w1b_ref[...], preferred_element_type=F32)
    n = p2.shape[0]
    rows = lax.broadcasted_iota(jnp.int32, p2.shape, 0)
    p2_next = jnp.where(rows < n - 1, pltpu.roll(p2, n - 1, 0), 0.0)
    h = _gelu(p1 + p2_next + b1_ref[...])
    o_ref[...] = jnp.dot(h.astype(BF16), w2_ref[...], preferred_element_type=F32).astype(o_ref.dtype)


def _compress(z, p, batch, seq):
    nchunk = seq // NSA_CMP_STRIDE
    kc = z[:, :, ODD_KC * LANES:(ODD_KC + NSA_KV_HEADS) * LANES]
    vc = z[:, :, ODD_VC * LANES:(ODD_VC + NSA_KV_HEADS) * LANES]
    xc = jnp.stack([kc, vc], axis=1).reshape(batch, 2, nchunk, NSA_CMP_STRIDE, NSA_KV_HEADS, NSA_HEAD_DIM)
    xc = xc.transpose(0, 1, 4, 2, 3, 5).reshape(batch, 2, NSA_KV_HEADS, nchunk, NSA_CMP_STRIDE * NSA_HEAD_DIM)
    half = NSA_CMP_STRIDE * NSA_HEAD_DIM
    kind = lambda shape: pl.BlockSpec((None,) + shape, lambda b, c, h: (c,) + (0,) * len(shape))
    return pl.pallas_call(
        _compress_kernel,
        grid=(batch, 2, NSA_KV_HEADS),
        in_specs=[
            pl.BlockSpec((None, None, None, nchunk, half), lambda b, c, h: (b, c, h, 0, 0)),
            kind((half, NSA_HEAD_DIM)), kind((half, NSA_HEAD_DIM)),
            kind((1, 2 * half)), kind((1, NSA_HEAD_DIM)), kind((NSA_HEAD_DIM, NSA_HEAD_DIM)),
        ],
        out_specs=pl.BlockSpec((None, None, None, nchunk, NSA_HEAD_DIM), lambda b, c, h: (b, c, h, 0, 0)),
        out_shape=jax.ShapeDtypeStruct((batch, 2, NSA_KV_HEADS, nchunk, NSA_HEAD_DIM), BF16),
        compiler_params=_params("parallel", "arbitrary", "arbitrary"),
        name="compress",
    )(xc, p["cmp_w1a"], p["cmp_w1b"], p["cmp_pos"], p["cmp_b1"], p["cmp_w2"])


def _nt_dot(a, b):
    return lax.dot_general(a, b, (((1,), (1,)), ((), ())), preferred_element_type=F32)


def _tn_dot(a, b):
    return lax.dot_general(a, b, (((0,), (0,)), ((), ())), preferred_element_type=F32)


def _nsa_kernel(q_ref, kc_ref, vc_ref, ks_ref, vs_ref, kw_ref, vw_ref, gl_ref, o_ref, acc_ref, *, ncmp, nslc):
    i = pl.program_id(2)
    tq, tk, grp, d = NSA_Q_TILE, NSA_K_TILE, NSA_GROUP, NSA_HEAD_DIM
    rows = grp * tq
    scale = d ** -0.5
    t0 = i * tq
    q = jnp.concatenate([q_ref[:, g * d:(g + 1) * d] for g in range(grp)], axis=0)
    t_q = t0 + lax.broadcasted_iota(jnp.int32, (1, tq), 1)
    t_all = jnp.concatenate([t_q] * grp, axis=1)

    s = _nt_dot(kc_ref[...], q) * scale
    cidx = lax.broadcasted_iota(jnp.int32, (ncmp, 1), 0)
    vis = (cidx * NSA_CMP_STRIDE + (NSA_CMP_BLOCK - 1)) <= t_all
    s = jnp.where(vis, s, NEG_BIG)
    mx = jnp.max(s, axis=0, keepdims=True)
    e = jnp.where(vis, jnp.exp(s - mx), 0.0)
    p_cmp = e / jnp.maximum(jnp.sum(e, axis=0, keepdims=True), 1.0)
    o_cmp = _tn_dot(vc_ref[...], p_cmp.astype(BF16))

    p_sum = p_cmp[:, 0:tq]
    for g in range(1, grp):
        p_sum = p_sum + p_cmp[:, g * tq:(g + 1) * tq]
    sj = lax.broadcasted_iota(jnp.int32, (nslc, ncmp), 0) * NSA_SLC_BLOCK
    ci = lax.broadcasted_iota(jnp.int32, (nslc, ncmp), 1) * NSA_CMP_STRIDE
    overlap = jnp.where((ci < sj + NSA_SLC_BLOCK) & (ci + NSA_CMP_BLOCK > sj), 1.0, 0.0).astype(BF16)
    p_hi = p_sum.astype(BF16)
    p_lo = (p_sum - p_hi.astype(F32)).astype(BF16)
    imp = (jnp.dot(overlap, p_hi, preferred_element_type=F32)
           + jnp.dot(overlap, p_lo, preferred_element_type=F32))

    blk = lax.broadcasted_iota(jnp.int32, (nslc, tq), 0)
    cur = t_q // NSA_SLC_BLOCK
    forced = (blk == 0) | (blk == cur) | (blk == cur - 1)
    score = jnp.where(blk > cur, -jnp.inf, jnp.where(forced, jnp.inf, imp))
    sel = jnp.zeros((nslc, tq), F32)
    for _ in range(NSA_TOP_N):
        best = jnp.max(score, axis=0, keepdims=True)
        pick = jnp.min(jnp.where(score == best, blk, nslc), axis=0, keepdims=True)
        hit = blk == pick
        sel = jnp.where(hit, 1.0, sel)
        score = jnp.where(hit, -jnp.inf, score)
    sel = sel.astype(BF16)

    def flash(k_ref, v_ref, lo, hi, mask_fn):
        acc_ref[...] = jnp.zeros_like(acc_ref)

        def body(jt, carry):
            m_old, l_old = carry
            k0 = pl.multiple_of(jt * tk, tk)
            bias_q = jnp.where(mask_fn(k0), 0.0, NEG_BIG)
            bias = jnp.concatenate([bias_q] * grp, axis=1)
            sc = _nt_dot(k_ref[pl.ds(k0, tk), :], q) * (scale * LOG2_E) + bias
            m_new = jnp.maximum(m_old, jnp.max(sc, axis=0, keepdims=True))
            alpha = jnp.exp2(m_old - m_new)
            pr = jnp.exp2(sc - m_new)
            l_new = alpha * l_old + jnp.sum(pr, axis=0, keepdims=True)
            acc_ref[...] = alpha * acc_ref[...] + _tn_dot(v_ref[pl.ds(k0, tk), :], pr.astype(BF16))
            return m_new, l_new

        init = (jnp.full((1, rows), NEG_BIG, F32), jnp.zeros((1, rows), F32))
        _, l_fin = lax.fori_loop(lo, hi, body, init)
        return acc_ref[...] / l_fin

    krow = lax.broadcasted_iota(jnp.int32, (tk, 1), 0)
    bcol = lax.broadcasted_iota(jnp.int32, (tk, nslc), 1)

    def slc_mask(k0):
        kpos = k0 + krow
        expand = jnp.where((kpos // NSA_SLC_BLOCK) == bcol, 1.0, 0.0).astype(BF16)
        chosen = jnp.dot(expand, sel, preferred_element_type=F32) > 0.5
        return chosen & (kpos <= t_q)

    def win_mask(k0):
        kpos = k0 + krow
        return (kpos <= t_q) & (kpos > t_q - NSA_WINDOW)

    hi = (t0 + tq) // tk
    o_slc = flash(ks_ref, vs_ref, 0, hi, slc_mask)
    o_win = flash(kw_ref, vw_ref, jnp.maximum(t0 - NSA_WINDOW, 0) // tk, hi, win_mask)

    gates = jax.nn.sigmoid(gl_ref[...].astype(F32)).T
    for g in range(grp):
        cs = slice(g * tq, (g + 1) * tq)
        o_g = (gates[3 * g:3 * g + 1, :] * o_cmp[:, cs]
               + gates[3 * g + 1:3 * g + 2, :] * o_slc[:, cs]
               + gates[3 * g + 2:3 * g + 3, :] * o_win[:, cs])
        o_ref[:, g * d:(g + 1) * d] = o_g.T.astype(o_ref.dtype)


def _nsa(z, kvc, batch, seq):
    tq, d, grp = NSA_Q_TILE, NSA_HEAD_DIM, NSA_GROUP
    ncmp = kvc.shape[3]
    kern = functools.partial(_nsa_kernel, ncmp=ncmp, nslc=seq // NSA_SLC_BLOCK)
    qw = grp * d

    def seq_spec(col0):
        return pl.BlockSpec((None, seq, d), lambda b, h, i: (b, 0, col0 + h))

    def cmp_spec(c):
        return pl.BlockSpec((None, None, None, ncmp, d), lambda b, h, i: (b, c, h, 0, 0))

    return pl.pallas_call(
        kern,
        grid=(batch, NSA_KV_HEADS, seq // tq),
        in_specs=[
            pl.BlockSpec((None, tq, qw), lambda b, h, i: (b, i, ODD_Q * LANES // qw + h)),
            cmp_spec(0), cmp_spec(1),
            seq_spec(ODD_KS), seq_spec(ODD_VS), seq_spec(ODD_KW), seq_spec(ODD_VW),
            pl.BlockSpec((None, tq, LANES), lambda b, h, i: (b, i, ODD_GL + h)),
        ],
        out_specs=pl.BlockSpec((None, tq, qw), lambda b, h, i: (b, i, h)),
        out_shape=jax.ShapeDtypeStruct((batch, seq, NSA_HEADS * d), BF16),
        scratch_shapes=[pltpu.VMEM((d, grp * tq), F32)],
        compiler_params=_params("parallel", "parallel", "arbitrary"),
        name="nsa",
    )(z, kvc, kvc, z, z, z, z, z)


def _odd_params(w_in, a_re, a_im, log_dt, b_re, b_im, c_re, c_im, d_skip, glu_w, glu_b,
                cmp_pos, cmp_w1, cmp_b1, cmp_w2, seq):
    kvw = NSA_KV_WIDTH
    qw = NSA_HEADS * NSA_HEAD_DIM
    o_q = S5_WIDTH
    o_kv = o_q + qw
    parts = {name: w_in[:, o_kv + n * kvw:o_kv + (n + 1) * kvw]
             for n, name in enumerate(("kc", "vc", "ks", "vs", "kw", "vw"))}
    w_gl = w_in[:, o_kv + 6 * kvw:].reshape(D_MODEL, NSA_KV_HEADS, 3 * NSA_GROUP)
    w_gl = jnp.pad(w_gl, ((0, 0), (0, 0), (0, LANES - 3 * NSA_GROUP))).reshape(D_MODEL, NSA_KV_HEADS * LANES)
    w = jnp.concatenate([w_in[:, :o_kv], parts["kc"], parts["ks"], parts["kw"],
                         parts["vc"], parts["vs"], parts["vw"], w_gl], axis=1).astype(BF16)
    w = jnp.pad(w, ((0, 0), (0, ODD_COLS - w.shape[1])))
    half = NSA_CMP_STRIDE * NSA_HEAD_DIM
    w1 = cmp_w1.astype(BF16).reshape(2, NSA_CMP_BLOCK * NSA_HEAD_DIM, NSA_HEAD_DIM)
    p = _s5_params(a_re, a_im, log_dt, b_re, b_im, c_re, c_im, d_skip, glu_w, glu_b)
    p.update({
        "w_in": w[None],
        "cmp_w1a": w1[:, :half], "cmp_w1b": w1[:, half:],
        "cmp_pos": cmp_pos.astype(F32).reshape(2, 1, NSA_CMP_BLOCK * NSA_HEAD_DIM),
        "cmp_b1": cmp_b1.astype(F32).reshape(2, 1, NSA_HEAD_DIM),
        "cmp_w2": cmp_w2.astype(BF16),
        "tables": _rope_tables(seq, NSA_HEAD_DIM, (ODD_VC * LANES) % PROJ_TILE),
    })
    return p


def _odd_mixer(x, g_pre, g_post, p, w_out, layer, batch, seq):
    m = batch * seq
    rope_lo = ODD_Q * LANES // PROJ_TILE
    rope_hi = -(-ODD_VC * LANES // PROJ_TILE)
    z = _proj(x, g_pre, p["w_in"], 0, p["tables"], seq, rope_lo, rope_hi)
    z3 = z.reshape(batch, seq, z.shape[1])
    s5_out = _s5(z3, p, batch, seq)
    kvc = _compress(z3, p, batch, seq)
    o = _nsa(z3, kvc, batch, seq)
    return _outproj(s5_out.reshape(m, -1), o.reshape(m, -1), w_out, layer, x, g_post)


def kernel(x, norm_gains, ffn1_w_gate, ffn1_w_up, ffn1_w_down, ffn2_w_gate, ffn2_w_up, ffn2_w_down, ev_w_in, ev_w_out, pool_w, pool_scale, swa_sinks, od_w_in, od_w_out, s5_a_re, s5_a_im, s5_log_dt, s5_b_re, s5_b_im, s5_c_re, s5_c_im, s5_d, s5_glu_w, s5_glu_b, nsa_cmp_pos, nsa_cmp_w1, nsa_cmp_b1, nsa_cmp_w2):
    batch, seq, _ = x.shape
    m = batch * seq
    depth = norm_gains.shape[0]
    xs = x.reshape(m, D_MODEL)
    even_tables = _rope_tables(seq, SWA_HEAD_DIM, SWA_KV_HEADS * SWA_HEAD_DIM)
    ffn_w = [(_cast_bf16(wg, cols_out=D_FF_PAD), _cast_bf16(wu, cols_out=D_FF_PAD), _cast_bf16(wd, rows_out=D_FF_PAD))
             for wg, wu, wd in ((ffn1_w_gate, ffn1_w_up, ffn1_w_down), (ffn2_w_gate, ffn2_w_up, ffn2_w_down))]
    ev_in, ev_out, od_out = _cast_bf16(ev_w_in), _cast_bf16(ev_w_out), _cast_bf16(od_w_out)
    for layer in range(depth):
        g = norm_gains[layer].astype(F32).reshape(6, 1, D_MODEL)
        i = layer // 2
        xs = _ffn(xs, g[0], *ffn_w[0], g[1], layer)
        if layer % 2 == 0:
            xs = _even_mixer(xs, g[2], g[3], ev_in, ev_out, i,
                             pool_w[i].astype(BF16), pool_scale[i].astype(F32).reshape(1, POOL_WIDTH),
                             swa_sinks[i].astype(F32), even_tables, batch, seq)
        else:
            p = _odd_params(od_w_in[i], s5_a_re[i], s5_a_im[i], s5_log_dt[i], s5_b_re[i], s5_b_im[i],
                            s5_c_re[i], s5_c_im[i], s5_d[i], s5_glu_w[i], s5_glu_b[i], nsa_cmp_pos[i],
                            nsa_cmp_w1[i], nsa_cmp_b1[i], nsa_cmp_w2[i], seq)
            xs = _odd_mixer(xs, g[2], g[3], p, od_out, i, batch, seq)
        xs = _ffn(xs, g[4], *ffn_w[1], g[5], layer)
    return xs.reshape(batch, seq, D_MODEL)
```

```python
import functools
import math

import numpy as np
import jax
import jax.numpy as jnp
from jax import lax
from jax.experimental import pallas as pl
from jax.experimental.pallas import tpu as pltpu

F32 = jnp.float32
BF16 = jnp.bfloat16

D_MODEL = 4096
D_FF = 5504
NORM_EPS = 1e-6
ROPE_THETA = 500000.0
ROPE_FRACTION = 4
POOL_WINDOWS = (2, 4, 8, 16)
POOL_WIDTH = D_MODEL // 2
POOL_GROUP = POOL_WIDTH // len(POOL_WINDOWS)
POOL_HALO = 16
SWA_HEAD_DIM = 64
SWA_HEADS = 32
SWA_KV_HEADS = 4
SWA_GROUP = SWA_HEADS // SWA_KV_HEADS
SWA_WINDOW = 128
ATTN_BLOCK = 128
S5_WIDTH = D_MODEL // 4
S5_GROUP_CH = 16
S5_GROUPS = S5_WIDTH // S5_GROUP_CH
S5_STATE = 64
NSA_HEAD_DIM = 128
NSA_HEADS = 24
NSA_KV_HEADS = 6
NSA_GROUP = 4
NSA_CMP_BLOCK = 32
NSA_CMP_STRIDE = 16
NSA_SLC_BLOCK = 64
NSA_TOP_N = 8
NSA_WINDOW = 512
NSA_KV_WIDTH = NSA_KV_HEADS * NSA_HEAD_DIM

LANES = 128
SUBLANES = 8
MXU_DIM = 256
VMEM_LIMIT_BYTES = 56 * 1024 * 1024

ROW_TILE = 512
FF_TILE = 256
D_FF_PAD = 5632
PROJ_TILE = 512
OUT_K_TILE = 512
NORM_ROWS = 32
CAST_ROWS = 256
NEG_BIG = -1e30


def _params(*sem):
    return pltpu.CompilerParams(dimension_semantics=sem, vmem_limit_bytes=VMEM_LIMIT_BYTES)


def _norm_rows(dst_ref, src_ref, g_ref, rows, res_refs=(), res_scale=1.0):
    g = g_ref[...]

    def body(c, carry):
        r = pl.multiple_of(c * NORM_ROWS, NORM_ROWS)
        rs = pl.ds(r, NORM_ROWS)
        v = src_ref[rs, :].astype(F32)
        ms = jnp.mean(v * v, axis=-1, keepdims=True)
        y = v * lax.rsqrt(ms + NORM_EPS) * g
        if not res_refs:
            dst_ref[rs, :] = y.astype(dst_ref.dtype)
        width = y.shape[1] // max(len(res_refs), 1)
        for n, res_ref in enumerate(res_refs):
            cs = slice(n * width, (n + 1) * width)
            dst_ref[rs, cs] = (res_ref[rs, :] + res_scale * y[:, cs]).astype(dst_ref.dtype)
        return carry

    lax.fori_loop(0, rows // NORM_ROWS, body, 0, unroll=2)


def _cast_kernel(x_ref, o_ref, *, tr, rows_in, cols_in, cols_out, col_tile):
    x = x_ref[...]
    if rows_in % tr:
        r = pl.program_id(1) * tr + lax.broadcasted_iota(jnp.int32, (tr, 1), 0)
        x = jnp.where(r < rows_in, x, 0.0)
    y = x.astype(o_ref.dtype)
    if cols_out > cols_in:
        y = jnp.concatenate([y, jnp.zeros((tr, cols_out - cols_in), o_ref.dtype)], axis=1)
    if col_tile is None:
        o_ref[...] = y
    else:
        for j in range(cols_out // col_tile):
            o_ref[j] = y[:, j * col_tile:(j + 1) * col_tile]


def _cast_bf16(w, rows_out=None, cols_out=None, col_tile=None):
    nl, rows_in, cols_in = w.shape
    rows_out = rows_out or rows_in
    cols_out = cols_out or cols_in
    tr = min(CAST_ROWS, rows_out)
    kern = functools.partial(_cast_kernel, tr=tr, rows_in=rows_in, cols_in=cols_in, cols_out=cols_out,
                             col_tile=col_tile)
    if col_tile is None:
        out_spec = pl.BlockSpec((None, tr, cols_out), lambda l, i: (l, i, 0))
        out_shape = (nl, rows_out, cols_out)
    else:
        nt = cols_out // col_tile
        out_spec = pl.BlockSpec((None, nt, tr, col_tile), lambda l, i: (l, 0, i, 0))
        out_shape = (nl, nt, rows_out, col_tile)
    return pl.pallas_call(
        kern,
        grid=(nl, rows_out // tr),
        in_specs=[pl.BlockSpec((None, tr, cols_in), lambda l, i: (l, i, 0))],
        out_specs=out_spec,
        out_shape=jax.ShapeDtypeStruct(out_shape, BF16),
        compiler_params=_params("parallel", "parallel"),
        name="cast",
    )(w)


def _ffn_kernel(x_ref, gpre_ref, wg_ref, wu_ref, wd_ref, gpost_ref, o_ref, h_ref, *, tm, nj):
    j = pl.program_id(1)

    @pl.when(j == 0)
    def _():
        _norm_rows(h_ref, x_ref, gpre_ref, tm)
        o_ref[...] = jnp.zeros_like(o_ref)

    h = h_ref[...]
    gate = jnp.dot(h, wg_ref[...], preferred_element_type=F32)
    up = jnp.dot(h, wu_ref[...], preferred_element_type=F32)
    act = (gate * jax.nn.sigmoid(gate) * up).astype(BF16)
    nc = 512
    for n in range(D_MODEL // nc):
        sl = slice(n * nc, (n + 1) * nc)
        o_ref[:, sl] += jnp.dot(act, wd_ref[:, sl], preferred_element_type=F32)

    @pl.when(j == nj - 1)
    def _():
        _norm_rows(o_ref, o_ref, gpost_ref, tm, res_refs=(x_ref,), res_scale=0.5)


def _ffn(x, g_pre, wg, wu, wd, g_post, layer):
    m = x.shape[0]
    tm = min(ROW_TILE, m)
    nj = D_FF_PAD // FF_TILE
    kern = functools.partial(_ffn_kernel, tm=tm, nj=nj)
    return pl.pallas_call(
        kern,
        grid=(m // tm, nj),
        in_specs=[
            pl.BlockSpec((tm, D_MODEL), lambda i, j: (i, 0)),
            pl.BlockSpec((1, D_MODEL), lambda i, j: (0, 0)),
            pl.BlockSpec((None, None, D_MODEL, FF_TILE), lambda i, j: (layer, j, 0, 0)),
            pl.BlockSpec((None, None, D_MODEL, FF_TILE), lambda i, j: (layer, j, 0, 0)),
            pl.BlockSpec((None, FF_TILE, D_MODEL), lambda i, j: (layer, j, 0)),
            pl.BlockSpec((1, D_MODEL), lambda i, j: (0, 0)),
        ],
        out_specs=pl.BlockSpec((tm, D_MODEL), lambda i, j: (i, 0)),
        out_shape=jax.ShapeDtypeStruct((m, D_MODEL), F32),
        scratch_shapes=[pltpu.VMEM((tm, D_MODEL), BF16)],
        compiler_params=_params("parallel", "arbitrary"),
        name="ffn",
    )(x, g_pre, wg, wu, wd, g_post)


def _proj_kernel(x_ref, g_ref, w_ref, cos_ref, sa_ref, sb_ref, o_ref, h_ref, *, tm, rope_lo, rope_hi, half):
    j = pl.program_id(1)

    @pl.when(j == 0)
    def _():
        _norm_rows(h_ref, x_ref, g_ref, tm)

    is_rope = jnp.logical_and(j >= rope_lo, j < rope_hi)
    tn = o_ref.shape[1]
    tc = MXU_DIM

    def tile(rope):
        h = h_ref[...]
        for c in range(tn // tc):
            cs = slice(c * tc, (c + 1) * tc)
            z = jnp.dot(h, w_ref[:, cs], preferred_element_type=F32)
            if rope:
                z = (z * cos_ref[:, cs]
                     + pltpu.roll(z, tc - half, 1) * sa_ref[:, cs]
                     + pltpu.roll(z, half, 1) * sb_ref[:, cs])
            o_ref[:, cs] = z.astype(o_ref.dtype)

    @pl.when(is_rope)
    def _():
        tile(True)

    @pl.when(jnp.logical_not(is_rope))
    def _():
        tile(False)


def _rope_tables(seq, head_dim, n_rope_last):
    rot = head_dim // ROPE_FRACTION
    half = rot // 2
    inv_freq = jnp.power(ROPE_THETA, -jnp.arange(half, dtype=F32) * 2.0 / rot)
    ang = jnp.arange(seq, dtype=jnp.int32).astype(F32)[:, None] * inv_freq[None, :]
    cos, sin = jnp.cos(ang), jnp.sin(ang)
    one = jnp.ones((seq, head_dim - rot), F32)
    zero_h = jnp.zeros((seq, half), F32)
    zero_r = jnp.zeros((seq, head_dim - rot), F32)
    c_head = jnp.concatenate([cos, cos, one], axis=1)
    sa_head = jnp.concatenate([-sin, zero_h, zero_r], axis=1)
    sb_head = jnp.concatenate([zero_h, sin, zero_r], axis=1)
    reps = PROJ_TILE // head_dim
    col = jnp.arange(PROJ_TILE)[None, :]

    def kinds(t, ident):
        full = jnp.tile(t, (1, reps))
        part = jnp.where(col < n_rope_last, full, ident)
        return jnp.stack([full, part], axis=0)

    return kinds(c_head, 1.0), kinds(sa_head, 0.0), kinds(sb_head, 0.0), half


def _proj(x, g_pre, w, layer, tables, seq, rope_lo, rope_hi):
    cos_t, sa_t, sb_t, half = tables
    m = x.shape[0]
    tn = PROJ_TILE
    n = w.shape[1] * tn
    tm = min(ROW_TILE, seq)
    sblocks = seq // tm
    kern = functools.partial(_proj_kernel, tm=tm, rope_lo=rope_lo, rope_hi=rope_hi, half=half)

    def tab_map(i, j):
        return (jnp.where(j == rope_hi - 1, 1, 0), i % sblocks, 0)

    tab_spec = pl.BlockSpec((None, tm, tn), tab_map)
    return pl.pallas_call(
        kern,
        grid=(m // tm, n // tn),
        in_specs=[
            pl.BlockSpec((tm, D_MODEL), lambda i, j: (i, 0)),
            pl.BlockSpec((1, D_MODEL), lambda i, j: (0, 0)),
            pl.BlockSpec((None, None, D_MODEL, tn), lambda i, j: (layer, j, 0, 0)),
            tab_spec, tab_spec, tab_spec,
        ],
        out_specs=pl.BlockSpec((tm, tn), lambda i, j: (i, j)),
        out_shape=jax.ShapeDtypeStruct((m, n), BF16),
        scratch_shapes=[pltpu.VMEM((tm, D_MODEL), BF16)],
        compiler_params=_params("parallel", "arbitrary"),
        name="proj",
    )(x, g_pre, w, cos_t, sa_t, sb_t)


def _outproj_kernel(a1_ref, a2_ref, w_ref, g_ref, *rest, tm, nk1, nk):
    x_refs, o_ref = rest[:-1], rest[-1]
    k = pl.program_id(1)

    @pl.when(k == 0)
    def _():
        o_ref[...] = jnp.zeros_like(o_ref)

    def accumulate(a):
        nc = 512
        for n in range(D_MODEL // nc):
            sl = slice(n * nc, (n + 1) * nc)
            o_ref[:, sl] += jnp.dot(a, w_ref[:, sl], preferred_element_type=F32)

    @pl.when(k < nk1)
    def _():
        accumulate(a1_ref[...])

    @pl.when(k >= nk1)
    def _():
        accumulate(a2_ref[...])

    @pl.when(k == nk - 1)
    def _():
        _norm_rows(o_ref, o_ref, g_ref, tm, res_refs=x_refs, res_scale=1.0)


def _outproj(a1, a2, w, layer, x, g_post):
    m = x.shape[0]
    tm = min(ROW_TILE, m)
    tk = OUT_K_TILE
    nk1 = a1.shape[1] // tk
    nk = w.shape[1] // tk
    kern = functools.partial(_outproj_kernel, tm=tm, nk1=nk1, nk=nk)
    xw = D_MODEL // nk
    x_specs = [pl.BlockSpec((tm, xw), lambda i, k, c=c: (jnp.where(k >= c, i, jnp.maximum(i - 1, 0)), c))
               for c in range(nk)]
    return pl.pallas_call(
        kern,
        grid=(m // tm, nk),
        in_specs=[
            pl.BlockSpec((tm, tk), lambda i, k: (i, jnp.minimum(k, nk1 - 1))),
            pl.BlockSpec((tm, tk), lambda i, k: (i, jnp.maximum(k - nk1, 0))),
            pl.BlockSpec((None, tk, D_MODEL), lambda i, k: (layer, k, 0)),
            pl.BlockSpec((1, D_MODEL), lambda i, k: (0, 0)),
        ] + x_specs,
        out_specs=pl.BlockSpec((tm, D_MODEL), lambda i, k: (i, 0)),
        out_shape=jax.ShapeDtypeStruct((m, D_MODEL), F32),
        compiler_params=_params("parallel", "arbitrary"),
        name="outproj",
    )(a1, a2, w, g_post, *([x] * nk))


def _swa_kernel(sink_ref, q_ref, kvp_ref, kvc_ref, o_ref):
    n = pl.program_id(1)
    blk = ATTN_BLOCK
    kvw = SWA_KV_HEADS * SWA_HEAD_DIM
    kv = jnp.concatenate([kvp_ref[...], kvc_ref[...]], axis=0)
    krow = lax.broadcasted_iota(jnp.int32, (2 * blk, 1), 0)
    qcol = lax.broadcasted_iota(jnp.int32, (1, blk), 1)
    diff = qcol - krow + blk
    vis = (diff >= 0) & (diff < SWA_WINDOW) & ((krow >= blk) | (n > 0))
    bias = jnp.where(vis, 0.0, NEG_BIG)
    scale = SWA_HEAD_DIM ** -0.5 * LOG2_E
    pair = LANES // SWA_HEAD_DIM
    for kh in range(SWA_KV_HEADS):
        k_h = kv[:, kh * SWA_HEAD_DIM:(kh + 1) * SWA_HEAD_DIM]
        v_h = kv[:, kvw + kh * SWA_HEAD_DIM:kvw + (kh + 1) * SWA_HEAD_DIM]
        heads = range(kh * SWA_GROUP, (kh + 1) * SWA_GROUP)
        scores = [_nt_dot(k_h, q_ref[:, h * SWA_HEAD_DIM:(h + 1) * SWA_HEAD_DIM]) for h in heads]
        weights, denoms = [], []
        for h, s in zip(heads, scores):
            s = s * scale + bias
            sk = sink_ref[h] * LOG2_E
            mx = jnp.maximum(jnp.max(s, axis=0, keepdims=True), sk)
            e = jnp.exp2(s - mx)
            denoms.append(jnp.sum(e, axis=0, keepdims=True) + jnp.exp2(sk - mx))
            weights.append(e.astype(BF16))
        outs = [_tn_dot(v_h, e) / d for e, d in zip(weights, denoms)]
        for g0 in range(0, SWA_GROUP, pair):
            h0 = kh * SWA_GROUP + g0
            o_ref[:, h0 * SWA_HEAD_DIM:(h0 + pair) * SWA_HEAD_DIM] = (
                jnp.concatenate(outs[g0:g0 + pair], axis=0).T.astype(o_ref.dtype))


def _swa(z, sinks, batch, seq):
    blk = ATTN_BLOCK
    qw = SWA_HEADS * SWA_HEAD_DIM
    kvw2 = 2 * SWA_KV_HEADS * SWA_HEAD_DIM
    q_blk = POOL_WIDTH // qw
    kv_blk = (POOL_WIDTH + qw) // kvw2
    return pl.pallas_call(
        _swa_kernel,
        grid=(batch, seq // blk),
        in_specs=[
            pl.BlockSpec(memory_space=pltpu.SMEM),
            pl.BlockSpec((None, blk, qw), lambda b, n: (b, n, q_blk)),
            pl.BlockSpec((None, blk, kvw2), lambda b, n: (b, jnp.maximum(n - 1, 0), kv_blk)),
            pl.BlockSpec((None, blk, kvw2), lambda b, n: (b, n, kv_blk)),
        ],
        out_specs=pl.BlockSpec((None, blk, qw), lambda b, n: (b, n, 0)),
        out_shape=jax.ShapeDtypeStruct((batch, seq, qw), BF16),
        compiler_params=_params("parallel", "arbitrary"),
        name="swa",
    )(sinks, z, z, z)


def _pool_kernel(up_ref, uc_ref, w_ref, scale_ref, o_ref, *, ts):
    i = pl.program_id(1)
    has_prev = (i > 0).astype(F32)
    pos = i * ts + lax.broadcasted_iota(jnp.int32, (ts, 1), 0)
    for gi, win in enumerate(POOL_WINDOWS):
        sl = slice(gi * POOL_GROUP, (gi + 1) * POOL_GROUP)
        cur = uc_ref[:, sl].astype(F32)
        prev = up_ref[:, sl].astype(F32) * has_prev
        acc = jnp.concatenate([prev, cur], axis=0)
        d = 1
        while d < win:
            acc = acc + pltpu.roll(acc, d, 0)
            d *= 2
        wsum = acc[POOL_HALO:, :]
        count = jnp.minimum(pos + 1, win).astype(F32)
        zz = (wsum / count - cur).astype(BF16)
        a = jnp.dot(zz, w_ref[gi], preferred_element_type=F32) * scale_ref[:, sl]
        o_ref[:, sl] = a.astype(o_ref.dtype)


def _pool(z, w_pool, pool_scale, batch, seq):
    ts = min(256, seq)
    halo_blocks = ts // POOL_HALO
    kern = functools.partial(_pool_kernel, ts=ts)
    return pl.pallas_call(
        kern,
        grid=(batch, seq // ts),
        in_specs=[
            pl.BlockSpec((None, POOL_HALO, POOL_WIDTH), lambda b, i: (b, jnp.maximum(i * halo_blocks - 1, 0), 0)),
            pl.BlockSpec((None, ts, POOL_WIDTH), lambda b, i: (b, i, 0)),
            pl.BlockSpec((len(POOL_WINDOWS), POOL_GROUP, POOL_GROUP), lambda b, i: (0, 0, 0)),
            pl.BlockSpec((1, POOL_WIDTH), lambda b, i: (0, 0)),
        ],
        out_specs=pl.BlockSpec((None, ts, POOL_WIDTH), lambda b, i: (b, i, 0)),
        out_shape=jax.ShapeDtypeStruct((batch, seq, POOL_WIDTH), BF16),
        compiler_params=_params("parallel", "arbitrary"),
        name="pool",
    )(z, z, w_pool, pool_scale)


def _even_mixer(x, g_pre, g_post, w_in, w_out, layer, w_pool, pool_scale, sinks, tables, batch, seq):
    m = batch * seq
    q_lo = POOL_WIDTH // PROJ_TILE
    rope_hi = (POOL_WIDTH + SWA_HEADS * SWA_HEAD_DIM) // PROJ_TILE + 1
    z = _proj(x, g_pre, w_in, layer, tables, seq, q_lo, rope_hi)
    z3 = z.reshape(batch, seq, z.shape[1])
    o = _swa(z3, sinks, batch, seq)
    a = _pool(z3, w_pool, pool_scale, batch, seq)
    return _outproj(a.reshape(m, -1), o.reshape(m, -1), w_out, layer, x, g_post)


S5_SLAB = LANES
S5_SLABS = S5_WIDTH // S5_SLAB
S5_SLAB_STATE = (S5_SLAB // S5_GROUP_CH) * S5_STATE
S5_SHIFTS = (1, 2, 4)


def _gelu(y):
    return 0.5 * y * (1.0 + jnp.tanh(math.sqrt(2.0 / math.pi) * (y + 0.044715 * (y * y * y))))


def _cmul_add(vre, vim, are, aim, sre, sim):
    return vre + are * sre - aim * sim, vim + are * sim + aim * sre


def _s5_kernel(u_ref, b_ref, tab_ref, c_ref, d_ref, gw_ref, gb_ref, o_ref, carry_ref, x_ref, y_ref, *, ts):
    i = pl.program_id(1)
    ns = S5_SLAB_STATE

    @pl.when(i == 0)
    def _():
        carry_ref[...] = jnp.zeros_like(carry_ref)

    for j in range(S5_SLABS):
        ch = slice(j * S5_SLAB, (j + 1) * S5_SLAB)
        u_j = u_ref[:, ch]
        x_ref[...] = jnp.dot(u_j, b_ref[j], preferred_element_type=F32)
        tabs = [tab_ref[j, k] for k in range(8)]
        p_re, p_im = tabs[6], tabs[7]

        def tile_body(t, carry):
            c_re, c_im = carry
            r = pl.multiple_of(t * SUBLANES, SUBLANES)
            v_re = x_ref[pl.ds(r, SUBLANES), 0:ns]
            v_im = x_ref[pl.ds(r, SUBLANES), ns:2 * ns]
            for k, d in enumerate(S5_SHIFTS):
                s_re = pltpu.roll(v_re, d, 0)
                s_im = pltpu.roll(v_im, d, 0)
                v_re, v_im = _cmul_add(v_re, v_im, tabs[2 * k], tabs[2 * k + 1], s_re, s_im)
            v_re, v_im = _cmul_add(v_re, v_im, p_re, p_im, c_re, c_im)
            x_ref[pl.ds(r, SUBLANES), 0:ns] = v_re
            x_ref[pl.ds(r, SUBLANES), ns:2 * ns] = v_im
            last = SUBLANES - 1
            return (jnp.broadcast_to(v_re[last:last + 1, :], (SUBLANES, ns)),
                    jnp.broadcast_to(v_im[last:last + 1, :], (SUBLANES, ns)))

        c_re, c_im = lax.fori_loop(0, ts // SUBLANES, tile_body,
                                   (carry_ref[j, :, 0:ns], carry_ref[j, :, ns:2 * ns]))
        carry_ref[j, :, 0:ns] = c_re
        carry_ref[j, :, ns:2 * ns] = c_im
        y_j = jnp.dot(x_ref[...].astype(BF16), c_ref[j], preferred_element_type=F32)
        y_ref[:, ch] = _gelu(y_j + d_ref[:, ch] * u_j.astype(F32))

    y = y_ref[...]
    gate = jnp.dot(y.astype(BF16), gw_ref[...], preferred_element_type=F32) + gb_ref[...]
    o_ref[...] = (y * jax.nn.sigmoid(gate)).astype(o_ref.dtype)


def _s5(z, p, batch, seq):
    ts = min(256, seq)
    kern = functools.partial(_s5_kernel, ts=ts)
    full = lambda shape: pl.BlockSpec(shape, lambda b, i: (0,) * len(shape))
    return pl.pallas_call(
        kern,
        grid=(batch, seq // ts),
        in_specs=[
            pl.BlockSpec((None, ts, S5_WIDTH), lambda b, i: (b, i, 0)),
            full((S5_SLABS, S5_SLAB, 2 * S5_SLAB_STATE)),
            full((S5_SLABS, 8, SUBLANES, S5_SLAB_STATE)),
            full((S5_SLABS, 2 * S5_SLAB_STATE, S5_SLAB)),
            full((1, S5_WIDTH)),
            full((S5_WIDTH, S5_WIDTH)),
            full((1, S5_WIDTH)),
        ],
        out_specs=pl.BlockSpec((None, ts, S5_WIDTH), lambda b, i: (b, i, 0)),
        out_shape=jax.ShapeDtypeStruct((batch, seq, S5_WIDTH), BF16),
        scratch_shapes=[
            pltpu.VMEM((S5_SLABS, SUBLANES, 2 * S5_SLAB_STATE), F32),
            pltpu.VMEM((ts, 2 * S5_SLAB_STATE), F32),
            pltpu.VMEM((ts, S5_WIDTH), F32),
        ],
        compiler_params=_params("parallel", "arbitrary"),
        name="s5",
    )(z, p["s5_b"], p["s5_tab"], p["s5_c"], p["s5_d"], p["glu_w"], p["glu_b"])


def _s5_params(a_re, a_im, log_dt, b_re, b_im, c_re, c_im, d_skip, glu_w, glu_b):
    lam_re = a_re.astype(F32)
    lam_im = a_im.astype(F32)
    dt = jnp.exp(log_dt.astype(F32))[:, None]
    decay = jnp.exp(lam_re * dt)
    abar_re = decay * jnp.cos(lam_im * dt)
    abar_im = decay * jnp.sin(lam_im * dt)
    inv_mag = 1.0 / (lam_re * lam_re + lam_im * lam_im)
    num_re = abar_re - 1.0
    f_re = (num_re * lam_re + abar_im * lam_im) * inv_mag
    f_im = (abar_im * lam_re - num_re * lam_im) * inv_mag
    br = b_re.astype(F32)
    bi = b_im.astype(F32)
    bbar_re = f_re[..., None] * br - f_im[..., None] * bi
    bbar_im = f_re[..., None] * bi + f_im[..., None] * br
    gps = S5_SLAB // S5_GROUP_CH
    eye = jnp.eye(gps, dtype=F32)

    def b_slab(t):
        t = t.reshape(S5_SLABS, gps, S5_STATE, S5_GROUP_CH)
        t = jnp.einsum('jgnh,gk->jghkn', t, eye)
        return t.reshape(S5_SLABS, S5_SLAB, S5_SLAB_STATE)

    def c_slab(t):
        t = t.astype(F32).reshape(S5_SLABS, gps, S5_GROUP_CH, S5_STATE)
        t = jnp.einsum('jghn,gk->jgnkh', t, eye)
        return t.reshape(S5_SLABS, S5_SLAB_STATE, S5_SLAB)

    b_mat = jnp.concatenate([b_slab(bbar_re), b_slab(bbar_im)], axis=2).astype(BF16)
    c_mat = jnp.concatenate([c_slab(c_re), -c_slab(c_im)], axis=1).astype(BF16)

    ar = abar_re.reshape(S5_SLABS, 1, S5_SLAB_STATE)
    ai = abar_im.reshape(S5_SLABS, 1, S5_SLAB_STATE)
    row = jnp.arange(SUBLANES)[None, :, None]
    tabs = []
    pw_re, pw_im = ar, ai
    pows = []
    for d in S5_SHIFTS:
        tabs.append(jnp.where(row >= d, pw_re, 0.0))
        tabs.append(jnp.where(row >= d, pw_im, 0.0))
        pows.append((pw_re, pw_im))
        pw_re, pw_im = pw_re * pw_re - pw_im * pw_im, 2.0 * pw_re * pw_im
    pows.append((pw_re, pw_im))
    p_re = jnp.ones((S5_SLABS, SUBLANES, S5_SLAB_STATE), F32)
    p_im = jnp.zeros((S5_SLABS, SUBLANES, S5_SLAB_STATE), F32)
    for bit, (q_re, q_im) in zip((1, 2, 4, 8), pows):
        use = ((row + 1) & bit) != 0
        n_re = p_re * q_re - p_im * q_im
        n_im = p_re * q_im + p_im * q_re
        p_re = jnp.where(use, n_re, p_re)
        p_im = jnp.where(use, n_im, p_im)
    tabs += [p_re, p_im]
    tab = jnp.stack([jnp.broadcast_to(t, p_re.shape) for t in tabs], axis=1)
    return {
        "s5_b": b_mat, "s5_c": c_mat, "s5_tab": tab,
        "s5_d": d_skip.astype(F32).reshape(1, S5_WIDTH),
        "glu_w": glu_w.astype(BF16), "glu_b": glu_b.astype(F32).reshape(1, S5_WIDTH),
    }


NSA_CMP_SLOTS = 128
NSA_Q_TILE = 256
NSA_K_TILE = 256
LOG2_E = math.log2(math.e)
ODD_Q = S5_WIDTH // LANES
ODD_KC = ODD_Q + NSA_HEADS
ODD_KS = ODD_KC + NSA_KV_HEADS
ODD_KW = ODD_KS + NSA_KV_HEADS
ODD_VC = ODD_KW + NSA_KV_HEADS
ODD_VS = ODD_VC + NSA_KV_HEADS
ODD_VW = ODD_VS + NSA_KV_HEADS
ODD_GL = ODD_VW + NSA_KV_HEADS
ODD_END = ODD_GL + NSA_KV_HEADS
ODD_COLS = -(-ODD_END * LANES // PROJ_TILE) * PROJ_TILE


def _compress_kernel(x_ref, w1a_ref, w1b_ref, pos_ref, b1_ref, w2_ref, o_ref):
    half = NSA_CMP_STRIDE * NSA_HEAD_DIM
    xc = x_ref[...].astype(F32)
    xa = (xc + pos_ref[:, 0:half]).astype(BF16)
    xb = (xc + pos_ref[:, half:2 * half]).astype(BF16)
    p1 = jnp.dot(xa, w1a_ref[...], preferred_element_type=F32)
    p2 = jnp.dot(xb, w1b_ref[...], preferred_element_type=F32)
    n = p2.shape[0]
    rows = lax.broadcasted_iota(jnp.int32, p2.shape, 0)
    p2_next = jnp.where(rows < n - 1, pltpu.roll(p2, n - 1, 0), 0.0)
    h = _gelu(p1 + p2_next + b1_ref[...])
    o_ref[...] = jnp.dot(h.astype(BF16), w2_ref[...], preferred_element_type=F32).astype(o_ref.dtype)


def _compress(z, p, batch, seq):
    nchunk = seq // NSA_CMP_STRIDE
    kc = z[:, :, ODD_KC * LANES:(ODD_KC + NSA_KV_HEADS) * LANES]
    vc = z[:, :, ODD_VC * LANES:(ODD_VC + NSA_KV_HEADS) * LANES]
    xc = jnp.stack([kc, vc], axis=1).reshape(batch, 2, nchunk, NSA_CMP_STRIDE, NSA_KV_HEADS, NSA_HEAD_DIM)
    xc = xc.transpose(0, 1, 4, 2, 3, 5).reshape(batch, 2, NSA_KV_HEADS, nchunk, NSA_CMP_STRIDE * NSA_HEAD_DIM)
    half = NSA_CMP_STRIDE * NSA_HEAD_DIM
    kind = lambda shape: pl.BlockSpec((None,) + shape, lambda b, c, h: (c,) + (0,) * len(shape))
    return pl.pallas_call(
        _compress_kernel,
        grid=(batch, 2, NSA_KV_HEADS),
        in_specs=[
            pl.BlockSpec((None, None, None, nchunk, half), lambda b, c, h: (b, c, h, 0, 0)),
            kind((half, NSA_HEAD_DIM)), kind((half, NSA_HEAD_DIM)),
            kind((1, 2 * half)), kind((1, NSA_HEAD_DIM)), kind((NSA_HEAD_DIM, NSA_HEAD_DIM)),
        ],
        out_specs=pl.BlockSpec((None, None, None, nchunk, NSA_HEAD_DIM), lambda b, c, h: (b, c, h, 0, 0)),
        out_shape=jax.ShapeDtypeStruct((batch, 2, NSA_KV_HEADS, nchunk, NSA_HEAD_DIM), BF16),
        compiler_params=_params("parallel", "arbitrary", "arbitrary"),
        name="compress",
    )(xc, p["cmp_w1a"], p["cmp_w1b"], p["cmp_pos"], p["cmp_b1"], p["cmp_w2"])


def _nt_dot(a, b):
    return lax.dot_general(a, b, (((1,), (1,)), ((), ())), preferred_element_type=F32)


def _tn_dot(a, b):
    return lax.dot_general(a, b, (((0,), (0,)), ((), ())), preferred_element_type=F32)


def _nsa_kernel(q_ref, kc_ref, vc_ref, ks_ref, vs_ref, kw_ref, vw_ref, gl_ref, o_ref, acc_ref, accw_ref, *, ncmp, nslc):
    i = pl.program_id(2)
    tq, tk, grp, d = NSA_Q_TILE, NSA_K_TILE, NSA_GROUP, NSA_HEAD_DIM
    rows = grp * tq
    scale = d ** -0.5
    t0 = i * tq
    q = jnp.concatenate([q_ref[:, g * d:(g + 1) * d] for g in range(grp)], axis=0)
    t_q = t0 + lax.broadcasted_iota(jnp.int32, (1, tq), 1)
    t_all = jnp.concatenate([t_q] * grp, axis=1)

    s = _nt_dot(kc_ref[...], q) * scale
    cidx = lax.broadcasted_iota(jnp.int32, (ncmp, 1), 0)
    vis = (cidx * NSA_CMP_STRIDE + (NSA_CMP_BLOCK - 1)) <= t_all
    s = jnp.where(vis, s, NEG_BIG)
    mx = jnp.max(s, axis=0, keepdims=True)
    e = jnp.where(vis, jnp.exp(s - mx), 0.0)
    p_cmp = e / jnp.maximum(jnp.sum(e, axis=0, keepdims=True), 1.0)
    o_cmp = _tn_dot(vc_ref[...], p_cmp.astype(BF16))

    p_sum = p_cmp[:, 0:tq]
    for g in range(1, grp):
        p_sum = p_sum + p_cmp[:, g * tq:(g + 1) * tq]
    sj = lax.broadcasted_iota(jnp.int32, (nslc, ncmp), 0) * NSA_SLC_BLOCK
    ci = lax.broadcasted_iota(jnp.int32, (nslc, ncmp), 1) * NSA_CMP_STRIDE
    overlap = jnp.where((ci < sj + NSA_SLC_BLOCK) & (ci + NSA_CMP_BLOCK > sj), 1.0, 0.0).astype(BF16)
    p_hi = p_sum.astype(BF16)
    p_lo = (p_sum - p_hi.astype(F32)).astype(BF16)
    imp = (jnp.dot(overlap, p_hi, preferred_element_type=F32)
           + jnp.dot(overlap, p_lo, preferred_element_type=F32))

    blk = lax.broadcasted_iota(jnp.int32, (nslc, tq), 0)
    cur = t_q // NSA_SLC_BLOCK
    forced = (blk == 0) | (blk == cur) | (blk == cur - 1)
    score = jnp.where(blk > cur, -jnp.inf, jnp.where(forced, jnp.inf, imp))
    sel = jnp.zeros((nslc, tq), F32)
    for _ in range(NSA_TOP_N):
        best = jnp.max(score, axis=0, keepdims=True)
        pick = jnp.min(jnp.where(score == best, blk, nslc), axis=0, keepdims=True)
        hit = blk == pick
        sel = jnp.where(hit, 1.0, sel)
        score = jnp.where(hit, -jnp.inf, score)
    sel = sel.astype(BF16)

    krow = lax.broadcasted_iota(jnp.int32, (tk, 1), 0)
    bcol = lax.broadcasted_iota(jnp.int32, (tk, nslc), 1)

    def slc_mask(k0):
        kpos = k0 + krow
        expand = jnp.where((kpos // NSA_SLC_BLOCK) == bcol, 1.0, 0.0).astype(BF16)
        chosen = jnp.dot(expand, sel, preferred_element_type=F32) > 0.5
        return chosen & (kpos <= t_q)

    def win_mask(k0):
        kpos = k0 + krow
        return (kpos <= t_q) & (kpos > t_q - NSA_WINDOW)

    def tile_scores(k_ref, mask_fn, k0):
        bias_q = jnp.where(mask_fn(k0), 0.0, NEG_BIG)
        bias = jnp.concatenate([bias_q] * grp, axis=1)
        return _nt_dot(k_ref[pl.ds(k0, tk), :], q) * (scale * LOG2_E) + bias

    def tile_update(sc, carry, v_ref, acc, k0):
        m_old, l_old = carry
        m_new = jnp.maximum(m_old, jnp.max(sc, axis=0, keepdims=True))
        alpha = jnp.exp2(m_old - m_new)
        pr = jnp.exp2(sc - m_new)
        l_new = alpha * l_old + jnp.sum(pr, axis=0, keepdims=True)
        acc[...] = alpha * acc[...] + _tn_dot(v_ref[pl.ds(k0, tk), :], pr.astype(BF16))
        return m_new, l_new

    def slc_body(jt, carry):
        k0 = pl.multiple_of(jt * tk, tk)
        return tile_update(tile_scores(ks_ref, slc_mask, k0), carry, vs_ref, acc_ref, k0)

    def both_body(jt, carry):
        k0 = pl.multiple_of(jt * tk, tk)
        sc_s = tile_scores(ks_ref, slc_mask, k0)
        sc_w = tile_scores(kw_ref, win_mask, k0)
        return (tile_update(sc_s, carry[0], vs_ref, acc_ref, k0),
                tile_update(sc_w, carry[1], vw_ref, accw_ref, k0))

    acc_ref[...] = jnp.zeros_like(acc_ref)
    accw_ref[...] = jnp.zeros_like(accw_ref)
    init = (jnp.full((1, rows), NEG_BIG, F32), jnp.zeros((1, rows), F32))
    hi = (t0 + tq) // tk
    win_lo = jnp.maximum(t0 - NSA_WINDOW, 0) // tk
    state_s = lax.fori_loop(0, win_lo, slc_body, init)
    (_, l_s), (_, l_w) = lax.fori_loop(win_lo, hi, both_body, (state_s, init))
    o_slc = acc_ref[...] / l_s
    o_win = accw_ref[...] / l_w

    gates = jax.nn.sigmoid(gl_ref[...].astype(F32)).T
    for g in range(grp):
        cs = slice(g * tq, (g + 1) * tq)
        o_g = (gates[3 * g:3 * g + 1, :] * o_cmp[:, cs]
               + gates[3 * g + 1:3 * g + 2, :] * o_slc[:, cs]
               + gates[3 * g + 2:3 * g + 3, :] * o_win[:, cs])
        o_ref[:, g * d:(g + 1) * d] = o_g.T.astype(o_ref.dtype)


def _nsa(z, kvc, batch, seq):
    tq, d, grp = NSA_Q_TILE, NSA_HEAD_DIM, NSA_GROUP
    ncmp = kvc.shape[3]
    kern = functools.partial(_nsa_kernel, ncmp=ncmp, nslc=seq // NSA_SLC_BLOCK)
    qw = grp * d

    def seq_spec(col0):
        return pl.BlockSpec((None, seq, d), lambda b, h, i: (b, 0, col0 + h))

    def cmp_spec(c):
        return pl.BlockSpec((None, None, None, ncmp, d), lambda b, h, i: (b, c, h, 0, 0))

    return pl.pallas_call(
        kern,
        grid=(batch, NSA_KV_HEADS, seq // tq),
        in_specs=[
            pl.BlockSpec((None, tq, qw), lambda b, h, i: (b, i, ODD_Q * LANES // qw + h)),
            cmp_spec(0), cmp_spec(1),
            seq_spec(ODD_KS), seq_spec(ODD_VS), seq_spec(ODD_KW), seq_spec(ODD_VW),
            pl.BlockSpec((None, tq, LANES), lambda b, h, i: (b, i, ODD_GL + h)),
        ],
        out_specs=pl.BlockSpec((None, tq, qw), lambda b, h, i: (b, i, h)),
        out_shape=jax.ShapeDtypeStruct((batch, seq, NSA_HEADS * d), BF16),
        scratch_shapes=[pltpu.VMEM((d, grp * tq), F32), pltpu.VMEM((d, grp * tq), F32)],
        compiler_params=_params("parallel", "parallel", "arbitrary"),
        name="nsa",
    )(z, kvc, kvc, z, z, z, z, z)


def _odd_params(w_in, a_re, a_im, log_dt, b_re, b_im, c_re, c_im, d_skip, glu_w, glu_b,
                cmp_pos, cmp_w1, cmp_b1, cmp_w2, seq):
    kvw = NSA_KV_WIDTH
    qw = NSA_HEADS * NSA_HEAD_DIM
    o_q = S5_WIDTH
    o_kv = o_q + qw
    parts = {name: w_in[:, o_kv + n * kvw:o_kv + (n + 1) * kvw]
             for n, name in enumerate(("kc", "vc", "ks", "vs", "kw", "vw"))}
    w_gl = w_in[:, o_kv + 6 * kvw:].reshape(D_MODEL, NSA_KV_HEADS, 3 * NSA_GROUP)
    w_gl = jnp.pad(w_gl, ((0, 0), (0, 0), (0, LANES - 3 * NSA_GROUP))).reshape(D_MODEL, NSA_KV_HEADS * LANES)
    w = jnp.concatenate([w_in[:, :o_kv], parts["kc"], parts["ks"], parts["kw"],
                         parts["vc"], parts["vs"], parts["vw"], w_gl], axis=1).astype(BF16)
    w = jnp.pad(w, ((0, 0), (0, ODD_COLS - w.shape[1])))
    w = w.reshape(D_MODEL, ODD_COLS // PROJ_TILE, PROJ_TILE).transpose(1, 0, 2)
    half = NSA_CMP_STRIDE * NSA_HEAD_DIM
    w1 = cmp_w1.astype(BF16).reshape(2, NSA_CMP_BLOCK * NSA_HEAD_DIM, NSA_HEAD_DIM)
    p = _s5_params(a_re, a_im, log_dt, b_re, b_im, c_re, c_im, d_skip, glu_w, glu_b)
    p.update({
        "w_in": w[None],
        "cmp_w1a": w1[:, :half], "cmp_w1b": w1[:, half:],
        "cmp_pos": cmp_pos.astype(F32).reshape(2, 1, NSA_CMP_BLOCK * NSA_HEAD_DIM),
        "cmp_b1": cmp_b1.astype(F32).reshape(2, 1, NSA_HEAD_DIM),
        "cmp_w2": cmp_w2.astype(BF16),
        "tables": _rope_tables(seq, NSA_HEAD_DIM, (ODD_VC * LANES) % PROJ_TILE),
    })
    return p


def _odd_mixer(x, g_pre, g_post, p, w_out, layer, batch, seq):
    m = batch * seq
    rope_lo = ODD_Q * LANES // PROJ_TILE
    rope_hi = -(-ODD_VC * LANES // PROJ_TILE)
    z = _proj(x, g_pre, p["w_in"], 0, p["tables"], seq, rope_lo, rope_hi)
    z3 = z.reshape(batch, seq, z.shape[1])
    s5_out = _s5(z3, p, batch, seq)
    kvc = _compress(z3, p, batch, seq)
    o = _nsa(z3, kvc, batch, seq)
    return _outproj(s5_out.reshape(m, -1), o.reshape(m, -1), w_out, layer, x, g_post)


def kernel(x, norm_gains, ffn1_w_gate, ffn1_w_up, ffn1_w_down, ffn2_w_gate, ffn2_w_up, ffn2_w_down, ev_w_in, ev_w_out, pool_w, pool_scale, swa_sinks, od_w_in, od_w_out, s5_a_re, s5_a_im, s5_log_dt, s5_b_re, s5_b_im, s5_c_re, s5_c_im, s5_d, s5_glu_w, s5_glu_b, nsa_cmp_pos, nsa_cmp_w1, nsa_cmp_b1, nsa_cmp_w2):
    batch, seq, _ = x.shape
    m = batch * seq
    depth = norm_gains.shape[0]
    xs = x.reshape(m, D_MODEL)
    even_tables = _rope_tables(seq, SWA_HEAD_DIM, SWA_KV_HEADS * SWA_HEAD_DIM)
    ffn_w = [(_cast_bf16(wg, cols_out=D_FF_PAD, col_tile=FF_TILE), _cast_bf16(wu, cols_out=D_FF_PAD, col_tile=FF_TILE),
              _cast_bf16(wd, rows_out=D_FF_PAD))
             for wg, wu, wd in ((ffn1_w_gate, ffn1_w_up, ffn1_w_down), (ffn2_w_gate, ffn2_w_up, ffn2_w_down))]
    ev_in = _cast_bf16(ev_w_in, col_tile=PROJ_TILE)
    ev_out, od_out = _cast_bf16(ev_w_out), _cast_bf16(od_w_out)
    for layer in range(depth):
        g = norm_gains[layer].astype(F32).reshape(6, 1, D_MODEL)
        i = layer // 2
        xs = _ffn(xs, g[0], *ffn_w[0], g[1], layer)
        if layer % 2 == 0:
            xs = _even_mixer(xs, g[2], g[3], ev_in, ev_out, i,
                             pool_w[i].astype(BF16), pool_scale[i].astype(F32).reshape(1, POOL_WIDTH),
                             swa_sinks[i].astype(F32), even_tables, batch, seq)
        else:
            p = _odd_params(od_w_in[i], s5_a_re[i], s5_a_im[i], s5_log_dt[i], s5_b_re[i], s5_b_im[i],
                            s5_c_re[i], s5_c_im[i], s5_d[i], s5_glu_w[i], s5_glu_b[i], nsa_cmp_pos[i],
                            nsa_cmp_w1[i], nsa_cmp_b1[i], nsa_cmp_w2[i], seq)
            xs = _odd_mixer(xs, g[2], g[3], p, od_out, i, batch, seq)
        xs = _ffn(xs, g[4], *ffn_w[1], g[5], layer)
    return xs.reshape(batch, seq, D_MODEL)
```

```python
import functools
import math

import numpy as np
import jax
import jax.numpy as jnp
from jax import lax
from jax.experimental import pallas as pl
from jax.experimental.pallas import tpu as pltpu

F32 = jnp.float32
BF16 = jnp.bfloat16

D_MODEL = 4096
D_FF = 5504
NORM_EPS = 1e-6
ROPE_THETA = 500000.0
ROPE_FRACTION = 4
POOL_WINDOWS = (2, 4, 8, 16)
POOL_WIDTH = D_MODEL // 2
POOL_GROUP = POOL_WIDTH // len(POOL_WINDOWS)
POOL_HALO = 16
SWA_HEAD_DIM = 64
SWA_HEADS = 32
SWA_KV_HEADS = 4
SWA_GROUP = SWA_HEADS // SWA_KV_HEADS
SWA_WINDOW = 128
ATTN_BLOCK = 128
S5_WIDTH = D_MODEL // 4
S5_GROUP_CH = 16
S5_GROUPS = S5_WIDTH // S5_GROUP_CH
S5_STATE = 64
NSA_HEAD_DIM = 128
NSA_HEADS = 24
NSA_KV_HEADS = 6
NSA_GROUP = 4
NSA_CMP_BLOCK = 32
NSA_CMP_STRIDE = 16
NSA_SLC_BLOCK = 64
NSA_TOP_N = 8
NSA_WINDOW = 512
NSA_KV_WIDTH = NSA_KV_HEADS * NSA_HEAD_DIM

LANES = 128
SUBLANES = 8
MXU_DIM = 256
VMEM_LIMIT_BYTES = 56 * 1024 * 1024

ROW_TILE = 512
FF_TILE = 256
D_FF_PAD = 5632
PROJ_TILE = 512
OUT_ROW_TILE = 256
OUT_COL_CHUNK = 512
OUT_VMEM_LIMIT_BYTES = 60 * 1024 * 1024
NORM_ROWS = 32
NORM_UNROLL = 2
CAST_ROWS = 256
NEG_BIG = -1e30


def _params(*sem, vmem=VMEM_LIMIT_BYTES):
    return pltpu.CompilerParams(dimension_semantics=sem, vmem_limit_bytes=vmem)


def _norm_rows(dst_ref, src_ref, g_ref, rows, res_refs=(), res_scale=1.0):
    g = g_ref[...]

    def body(c, carry):
        r = pl.multiple_of(c * NORM_ROWS, NORM_ROWS)
        rs = pl.ds(r, NORM_ROWS)
        v = src_ref[rs, :].astype(F32)
        ms = jnp.mean(v * v, axis=-1, keepdims=True)
        y = v * lax.rsqrt(ms + NORM_EPS) * g
        if not res_refs:
            dst_ref[rs, :] = y.astype(dst_ref.dtype)
        width = y.shape[1] // max(len(res_refs), 1)
        for n, res_ref in enumerate(res_refs):
            cs = slice(n * width, (n + 1) * width)
            dst_ref[rs, cs] = (res_ref[rs, :] + res_scale * y[:, cs]).astype(dst_ref.dtype)
        return carry

    lax.fori_loop(0, rows // NORM_ROWS, body, 0, unroll=NORM_UNROLL)


def _cast_kernel(x_ref, o_ref, *, tr, rows_in, cols_in, cols_out, col_tile):
    x = x_ref[...]
    if rows_in % tr:
        r = pl.program_id(1) * tr + lax.broadcasted_iota(jnp.int32, (tr, 1), 0)
        x = jnp.where(r < rows_in, x, 0.0)
    y = x.astype(o_ref.dtype)
    if cols_out > cols_in:
        y = jnp.concatenate([y, jnp.zeros((tr, cols_out - cols_in), o_ref.dtype)], axis=1)
    if col_tile is None:
        o_ref[...] = y
    else:
        for j in range(cols_out // col_tile):
            o_ref[j] = y[:, j * col_tile:(j + 1) * col_tile]


def _cast_bf16(w, rows_out=None, cols_out=None, col_tile=None):
    nl, rows_in, cols_in = w.shape
    rows_out = rows_out or rows_in
    cols_out = cols_out or cols_in
    tr = min(CAST_ROWS, rows_out)
    kern = functools.partial(_cast_kernel, tr=tr, rows_in=rows_in, cols_in=cols_in, cols_out=cols_out,
                             col_tile=col_tile)
    if col_tile is None:
        out_spec = pl.BlockSpec((None, tr, cols_out), lambda l, i: (l, i, 0))
        out_shape = (nl, rows_out, cols_out)
    else:
        nt = cols_out // col_tile
        out_spec = pl.BlockSpec((None, nt, tr, col_tile), lambda l, i: (l, 0, i, 0))
        out_shape = (nl, nt, rows_out, col_tile)
    return pl.pallas_call(
        kern,
        grid=(nl, rows_out // tr),
        in_specs=[pl.BlockSpec((None, tr, cols_in), lambda l, i: (l, i, 0))],
        out_specs=out_spec,
        out_shape=jax.ShapeDtypeStruct(out_shape, BF16),
        compiler_params=_params("parallel", "parallel"),
        name="cast",
    )(w)


def _ffn_kernel(x_ref, gpre_ref, wg_ref, wu_ref, wd_ref, gpost_ref, o_ref, h_ref, *, tm, nj):
    j = pl.program_id(1)

    @pl.when(j == 0)
    def _():
        _norm_rows(h_ref, x_ref, gpre_ref, tm)
        o_ref[...] = jnp.zeros_like(o_ref)

    h = h_ref[...]
    gate = jnp.dot(h, wg_ref[...], preferred_element_type=F32)
    up = jnp.dot(h, wu_ref[...], preferred_element_type=F32)
    act = (gate * jax.nn.sigmoid(gate) * up).astype(BF16)
    nc = 512
    for n in range(D_MODEL // nc):
        sl = slice(n * nc, (n + 1) * nc)
        o_ref[:, sl] += jnp.dot(act, wd_ref[:, sl], preferred_element_type=F32)

    @pl.when(j == nj - 1)
    def _():
        _norm_rows(o_ref, o_ref, gpost_ref, tm, res_refs=(x_ref,), res_scale=0.5)


def _ffn(x, g_pre, wg, wu, wd, g_post, layer):
    m = x.shape[0]
    tm = min(ROW_TILE, m)
    nj = D_FF_PAD // FF_TILE
    kern = functools.partial(_ffn_kernel, tm=tm, nj=nj)
    return pl.pallas_call(
        kern,
        grid=(m // tm, nj),
        in_specs=[
            pl.BlockSpec((tm, D_MODEL), lambda i, j: (i, 0)),
            pl.BlockSpec((1, D_MODEL), lambda i, j: (0, 0)),
            pl.BlockSpec((None, None, D_MODEL, FF_TILE), lambda i, j: (layer, j, 0, 0)),
            pl.BlockSpec((None, None, D_MODEL, FF_TILE), lambda i, j: (layer, j, 0, 0)),
            pl.BlockSpec((None, FF_TILE, D_MODEL), lambda i, j: (layer, j, 0)),
            pl.BlockSpec((1, D_MODEL), lambda i, j: (0, 0)),
        ],
        out_specs=pl.BlockSpec((tm, D_MODEL), lambda i, j: (i, 0)),
        out_shape=jax.ShapeDtypeStruct((m, D_MODEL), F32),
        scratch_shapes=[pltpu.VMEM((tm, D_MODEL), BF16)],
        compiler_params=_params("parallel", "arbitrary"),
        name="ffn",
    )(x, g_pre, wg, wu, wd, g_post)


def _proj_kernel(x_ref, g_ref, w_ref, cos_ref, sa_ref, sb_ref, o_ref, h_ref, *, tm, rope_lo, rope_hi, half):
    j = pl.program_id(1)

    @pl.when(j == 0)
    def _():
        _norm_rows(h_ref, x_ref, g_ref, tm)

    is_rope = jnp.logical_and(j >= rope_lo, j < rope_hi)
    tn = o_ref.shape[1]
    tc = MXU_DIM

    def tile(rope):
        h = h_ref[...]
        for c in range(tn // tc):
            cs = slice(c * tc, (c + 1) * tc)
            z = jnp.dot(h, w_ref[:, cs], preferred_element_type=F32)
            if rope:
                z = (z * cos_ref[:, cs]
                     + pltpu.roll(z, tc - half, 1) * sa_ref[:, cs]
                     + pltpu.roll(z, half, 1) * sb_ref[:, cs])
            o_ref[:, cs] = z.astype(o_ref.dtype)

    @pl.when(is_rope)
    def _():
        tile(True)

    @pl.when(jnp.logical_not(is_rope))
    def _():
        tile(False)


def _rope_tables(seq, head_dim, n_rope_last):
    rot = head_dim // ROPE_FRACTION
    half = rot // 2
    inv_freq = jnp.power(ROPE_THETA, -jnp.arange(half, dtype=F32) * 2.0 / rot)
    ang = jnp.arange(seq, dtype=jnp.int32).astype(F32)[:, None] * inv_freq[None, :]
    cos, sin = jnp.cos(ang), jnp.sin(ang)
    one = jnp.ones((seq, head_dim - rot), F32)
    zero_h = jnp.zeros((seq, half), F32)
    zero_r = jnp.zeros((seq, head_dim - rot), F32)
    c_head = jnp.concatenate([cos, cos, one], axis=1)
    sa_head = jnp.concatenate([-sin, zero_h, zero_r], axis=1)
    sb_head = jnp.concatenate([zero_h, sin, zero_r], axis=1)
    reps = PROJ_TILE // head_dim
    col = jnp.arange(PROJ_TILE)[None, :]

    def kinds(t, ident):
        full = jnp.tile(t, (1, reps))
        part = jnp.where(col < n_rope_last, full, ident)
        return jnp.stack([full, part], axis=0)

    return kinds(c_head, 1.0), kinds(sa_head, 0.0), kinds(sb_head, 0.0), half


def _proj(x, g_pre, w, layer, tables, seq, rope_lo, rope_hi):
    cos_t, sa_t, sb_t, half = tables
    m = x.shape[0]
    tn = PROJ_TILE
    n = w.shape[1] * tn
    tm = min(ROW_TILE, seq)
    sblocks = seq // tm
    kern = functools.partial(_proj_kernel, tm=tm, rope_lo=rope_lo, rope_hi=rope_hi, half=half)

    def tab_map(i, j):
        return (jnp.where(j == rope_hi - 1, 1, 0), i % sblocks, 0)

    tab_spec = pl.BlockSpec((None, tm, tn), tab_map)
    return pl.pallas_call(
        kern,
        grid=(m // tm, n // tn),
        in_specs=[
            pl.BlockSpec((tm, D_MODEL), lambda i, j: (i, 0)),
            pl.BlockSpec((1, D_MODEL), lambda i, j: (0, 0)),
            pl.BlockSpec((None, None, D_MODEL, tn), lambda i, j: (layer, j, 0, 0)),
            tab_spec, tab_spec, tab_spec,
        ],
        out_specs=pl.BlockSpec((tm, tn), lambda i, j: (i, j)),
        out_shape=jax.ShapeDtypeStruct((m, n), BF16),
        scratch_shapes=[pltpu.VMEM((tm, D_MODEL), BF16)],
        compiler_params=_params("parallel", "arbitrary"),
        name="proj",
    )(x, g_pre, w, cos_t, sa_t, sb_t)


def _outproj_kernel(a1_ref, a2_ref, w_ref, g_ref, x_ref, o_ref, *, tm, k1):
    a1 = a1_ref[...]
    a2 = a2_ref[...]
    nc = OUT_COL_CHUNK
    for n in range(D_MODEL // nc):
        sl = slice(n * nc, (n + 1) * nc)
        o_ref[:, sl] = (jnp.dot(a1, w_ref[0:k1, sl], preferred_element_type=F32)
                        + jnp.dot(a2, w_ref[k1:, sl], preferred_element_type=F32))
    _norm_rows(o_ref, o_ref, g_ref, tm, res_refs=(x_ref,), res_scale=1.0)


def _outproj(a1, a2, w, layer, x, g_post):
    m = x.shape[0]
    tm = min(OUT_ROW_TILE, m)
    k1, k2 = a1.shape[1], a2.shape[1]
    kern = functools.partial(_outproj_kernel, tm=tm, k1=k1)
    return pl.pallas_call(
        kern,
        grid=(m // tm,),
        in_specs=[
            pl.BlockSpec((tm, k1), lambda i: (i, 0)),
            pl.BlockSpec((tm, k2), lambda i: (i, 0)),
            pl.BlockSpec((None, k1 + k2, D_MODEL), lambda i: (layer, 0, 0), pipeline_mode=pl.Buffered(1)),
            pl.BlockSpec((1, D_MODEL), lambda i: (0, 0)),
            pl.BlockSpec((tm, D_MODEL), lambda i: (i, 0)),
        ],
        out_specs=pl.BlockSpec((tm, D_MODEL), lambda i: (i, 0)),
        out_shape=jax.ShapeDtypeStruct((m, D_MODEL), F32),
        compiler_params=_params("parallel", vmem=OUT_VMEM_LIMIT_BYTES),
        name="outproj",
    )(a1, a2, w, g_post, x)


def _swa_kernel(sink_ref, q_ref, kvp_ref, kvc_ref, o_ref):
    n = pl.program_id(1)
    blk = ATTN_BLOCK
    kvw = SWA_KV_HEADS * SWA_HEAD_DIM
    kv = jnp.concatenate([kvp_ref[...], kvc_ref[...]], axis=0)
    krow = lax.broadcasted_iota(jnp.int32, (2 * blk, 1), 0)
    qcol = lax.broadcasted_iota(jnp.int32, (1, blk), 1)
    diff = qcol - krow + blk
    vis = (diff >= 0) & (diff < SWA_WINDOW) & ((krow >= blk) | (n > 0))
    bias = jnp.where(vis, 0.0, NEG_BIG)
    scale = SWA_HEAD_DIM ** -0.5 * LOG2_E
    pair = LANES // SWA_HEAD_DIM
    for kh in range(SWA_KV_HEADS):
        k_h = kv[:, kh * SWA_HEAD_DIM:(kh + 1) * SWA_HEAD_DIM]
        v_h = kv[:, kvw + kh * SWA_HEAD_DIM:kvw + (kh + 1) * SWA_HEAD_DIM]
        heads = range(kh * SWA_GROUP, (kh + 1) * SWA_GROUP)
        scores = [_nt_dot(k_h, q_ref[:, h * SWA_HEAD_DIM:(h + 1) * SWA_HEAD_DIM]) for h in heads]
        weights, denoms = [], []
        for h, s in zip(heads, scores):
            s = s * scale + bias
            sk = sink_ref[h] * LOG2_E
            mx = jnp.maximum(jnp.max(s, axis=0, keepdims=True), sk)
            e = jnp.exp2(s - mx)
            denoms.append(jnp.sum(e, axis=0, keepdims=True) + jnp.exp2(sk - mx))
            weights.append(e.astype(BF16))
        outs = [_tn_dot(v_h, e) / d for e, d in zip(weights, denoms)]
        for g0 in range(0, SWA_GROUP, pair):
            h0 = kh * SWA_GROUP + g0
            o_ref[:, h0 * SWA_HEAD_DIM:(h0 + pair) * SWA_HEAD_DIM] = (
                jnp.concatenate(outs[g0:g0 + pair], axis=0).T.astype(o_ref.dtype))


def _swa(z, sinks, batch, seq):
    blk = ATTN_BLOCK
    qw = SWA_HEADS * SWA_HEAD_DIM
    kvw2 = 2 * SWA_KV_HEADS * SWA_HEAD_DIM
    q_blk = POOL_WIDTH // qw
    kv_blk = (POOL_WIDTH + qw) // kvw2
    return pl.pallas_call(
        _swa_kernel,
        grid=(batch, seq // blk),
        in_specs=[
            pl.BlockSpec(memory_space=pltpu.SMEM),
            pl.BlockSpec((None, blk, qw), lambda b, n: (b, n, q_blk)),
            pl.BlockSpec((None, blk, kvw2), lambda b, n: (b, jnp.maximum(n - 1, 0), kv_blk)),
            pl.BlockSpec((None, blk, kvw2), lambda b, n: (b, n, kv_blk)),
        ],
        out_specs=pl.BlockSpec((None, blk, qw), lambda b, n: (b, n, 0)),
        out_shape=jax.ShapeDtypeStruct((batch, seq, qw), BF16),
        compiler_params=_params("parallel", "arbitrary"),
        name="swa",
    )(sinks, z, z, z)


def _pool_kernel(up_ref, uc_ref, w_ref, scale_ref, o_ref, *, ts):
    i = pl.program_id(1)
    has_prev = (i > 0).astype(F32)
    pos = i * ts + lax.broadcasted_iota(jnp.int32, (ts, 1), 0)
    for gi, win in enumerate(POOL_WINDOWS):
        sl = slice(gi * POOL_GROUP, (gi + 1) * POOL_GROUP)
        cur = uc_ref[:, sl].astype(F32)
        prev = up_ref[:, sl].astype(F32) * has_prev
        acc = jnp.concatenate([prev, cur], axis=0)
        d = 1
        while d < win:
            acc = acc + pltpu.roll(acc, d, 0)
            d *= 2
        wsum = acc[POOL_HALO:, :]
        count = jnp.minimum(pos + 1, win).astype(F32)
        zz = (wsum / count - cur).astype(BF16)
        a = jnp.dot(zz, w_ref[gi], preferred_element_type=F32) * scale_ref[:, sl]
        o_ref[:, sl] = a.astype(o_ref.dtype)


def _pool(z, w_pool, pool_scale, batch, seq):
    ts = min(256, seq)
    halo_blocks = ts // POOL_HALO
    kern = functools.partial(_pool_kernel, ts=ts)
    return pl.pallas_call(
        kern,
        grid=(batch, seq // ts),
        in_specs=[
            pl.BlockSpec((None, POOL_HALO, POOL_WIDTH), lambda b, i: (b, jnp.maximum(i * halo_blocks - 1, 0), 0)),
            pl.BlockSpec((None, ts, POOL_WIDTH), lambda b, i: (b, i, 0)),
            pl.BlockSpec((len(POOL_WINDOWS), POOL_GROUP, POOL_GROUP), lambda b, i: (0, 0, 0)),
            pl.BlockSpec((1, POOL_WIDTH), lambda b, i: (0, 0)),
        ],
        out_specs=pl.BlockSpec((None, ts, POOL_WIDTH), lambda b, i: (b, i, 0)),
        out_shape=jax.ShapeDtypeStruct((batch, seq, POOL_WIDTH), BF16),
        compiler_params=_params("parallel", "arbitrary"),
        name="pool",
    )(z, z, w_pool, pool_scale)


def _even_mixer(x, g_pre, g_post, w_in, w_out, layer, w_pool, pool_scale, sinks, tables, batch, seq):
    m = batch * seq
    q_lo = POOL_WIDTH // PROJ_TILE
    rope_hi = (POOL_WIDTH + SWA_HEADS * SWA_HEAD_DIM) // PROJ_TILE + 1
    z = _proj(x, g_pre, w_in, layer, tables, seq, q_lo, rope_hi)
    z3 = z.reshape(batch, seq, z.shape[1])
    o = _swa(z3, sinks, batch, seq)
    a = _pool(z3, w_pool, pool_scale, batch, seq)
    return _outproj(a.reshape(m, -1), o.reshape(m, -1), w_out, layer, x, g_post)


S5_SLAB = LANES
S5_SLABS = S5_WIDTH // S5_SLAB
S5_SLAB_STATE = (S5_SLAB // S5_GROUP_CH) * S5_STATE
S5_TIME_TILE = 128
S5_SCAN_UNROLL = 8


def _gelu(y):
    return 0.5 * y * (1.0 + jnp.tanh(math.sqrt(2.0 / math.pi) * (y + 0.044715 * (y * y * y))))


def _s5_kernel(u_ref, b_ref, tab_ref, c_ref, d_ref, gw_ref, gb_ref, o_ref, carry_ref, x_ref, y_ref, *, ts):
    i = pl.program_id(0)
    ns = S5_SLAB_STATE
    nb = SUBLANES
    nl = ns // LANES

    @pl.when(i == 0)
    def _():
        carry_ref[...] = jnp.zeros_like(carry_ref)

    u_all = u_ref[...].reshape(nb * ts, S5_WIDTH)
    for j in range(S5_SLABS):
        ch = slice(j * S5_SLAB, (j + 1) * S5_SLAB)
        u_j = u_all[:, ch]
        bu = jnp.dot(u_j, b_ref[j], preferred_element_type=F32)
        for b in range(nb):
            for c in range(2 * nl):
                x_ref[c, pl.ds(b, ts, stride=nb), :] = bu[b * ts:(b + 1) * ts, c * LANES:(c + 1) * LANES]
        a_re, a_im = tab_ref[j, 0], tab_ref[j, 1]

        def step(t, carry):
            c_re, c_im = carry
            r = pl.multiple_of(t * nb, nb)
            v_re = jnp.concatenate([x_ref[c, pl.ds(r, nb), :] for c in range(nl)], axis=1)
            v_im = jnp.concatenate([x_ref[nl + c, pl.ds(r, nb), :] for c in range(nl)], axis=1)
            n_re = v_re + a_re * c_re - a_im * c_im
            n_im = v_im + a_re * c_im + a_im * c_re
            for c in range(nl):
                x_ref[c, pl.ds(r, nb), :] = n_re[:, c * LANES:(c + 1) * LANES]
                x_ref[nl + c, pl.ds(r, nb), :] = n_im[:, c * LANES:(c + 1) * LANES]
            return n_re, n_im

        c_re, c_im = lax.fori_loop(0, ts, step, (carry_ref[j, :, 0:ns], carry_ref[j, :, ns:2 * ns]),
                                   unroll=S5_SCAN_UNROLL)
        carry_ref[j, :, 0:ns] = c_re
        carry_ref[j, :, ns:2 * ns] = c_im
        states = jnp.concatenate(
            [jnp.concatenate([x_ref[c, pl.ds(b, ts, stride=nb), :] for c in range(2 * nl)], axis=1)
             for b in range(nb)], axis=0)
        y_j = jnp.dot(states.astype(BF16), c_ref[j], preferred_element_type=F32)
        y_ref[:, ch] = _gelu(y_j + d_ref[:, ch] * u_j.astype(F32))

    y = y_ref[...]
    gate = jnp.dot(y.astype(BF16), gw_ref[...], preferred_element_type=F32) + gb_ref[...]
    o_ref[...] = (y * jax.nn.sigmoid(gate)).astype(o_ref.dtype).reshape(nb, ts, S5_WIDTH)


def _s5(z, p, batch, seq):
    assert batch == SUBLANES, "the S5 scan lays the batch out on the 8 sublanes"
    ts = min(S5_TIME_TILE, seq)
    kern = functools.partial(_s5_kernel, ts=ts)
    full = lambda shape: pl.BlockSpec(shape, lambda i: (0,) * len(shape))
    return pl.pallas_call(
        kern,
        grid=(seq // ts,),
        in_specs=[
            pl.BlockSpec((batch, ts, S5_WIDTH), lambda i: (0, i, 0)),
            full((S5_SLABS, S5_SLAB, 2 * S5_SLAB_STATE)),
            full((S5_SLABS, 2, SUBLANES, S5_SLAB_STATE)),
            full((S5_SLABS, 2 * S5_SLAB_STATE, S5_SLAB)),
            full((1, S5_WIDTH)),
            full((S5_WIDTH, S5_WIDTH)),
            full((1, S5_WIDTH)),
        ],
        out_specs=pl.BlockSpec((batch, ts, S5_WIDTH), lambda i: (0, i, 0)),
        out_shape=jax.ShapeDtypeStruct((batch, seq, S5_WIDTH), BF16),
        scratch_shapes=[
            pltpu.VMEM((S5_SLABS, SUBLANES, 2 * S5_SLAB_STATE), F32),
            pltpu.VMEM((2 * S5_SLAB_STATE // LANES, batch * ts, LANES), F32),
            pltpu.VMEM((batch * ts, S5_WIDTH), F32),
        ],
        compiler_params=_params("arbitrary"),
        name="s5",
    )(z, p["s5_b"], p["s5_tab"], p["s5_c"], p["s5_d"], p["glu_w"], p["glu_b"])


def _s5_params(a_re, a_im, log_dt, b_re, b_im, c_re, c_im, d_skip, glu_w, glu_b):
    lam_re = a_re.astype(F32)
    lam_im = a_im.astype(F32)
    dt = jnp.exp(log_dt.astype(F32))[:, None]
    decay = jnp.exp(lam_re * dt)
    abar_re = decay * jnp.cos(lam_im * dt)
    abar_im = decay * jnp.sin(lam_im * dt)
    inv_mag = 1.0 / (lam_re * lam_re + lam_im * lam_im)
    num_re = abar_re - 1.0
    f_re = (num_re * lam_re + abar_im * lam_im) * inv_mag
    f_im = (abar_im * lam_re - num_re * lam_im) * inv_mag
    br = b_re.astype(F32)
    bi = b_im.astype(F32)
    bbar_re = f_re[..., None] * br - f_im[..., None] * bi
    bbar_im = f_re[..., None] * bi + f_im[..., None] * br
    gps = S5_SLAB // S5_GROUP_CH
    eye = jnp.eye(gps, dtype=F32)

    def b_slab(t):
        t = t.reshape(S5_SLABS, gps, S5_STATE, S5_GROUP_CH)
        t = jnp.einsum('jgnh,gk->jghkn', t, eye)
        return t.reshape(S5_SLABS, S5_SLAB, S5_SLAB_STATE)

    def c_slab(t):
        t = t.astype(F32).reshape(S5_SLABS, gps, S5_GROUP_CH, S5_STATE)
        t = jnp.einsum('jghn,gk->jgnkh', t, eye)
        return t.reshape(S5_SLABS, S5_SLAB_STATE, S5_SLAB)

    b_mat = jnp.concatenate([b_slab(bbar_re), b_slab(bbar_im)], axis=2).astype(BF16)
    c_mat = jnp.concatenate([c_slab(c_re), -c_slab(c_im)], axis=1).astype(BF16)

    tab = jnp.stack([abar_re.reshape(S5_SLABS, 1, S5_SLAB_STATE), abar_im.reshape(S5_SLABS, 1, S5_SLAB_STATE)], axis=1)
    tab = jnp.broadcast_to(tab, (S5_SLABS, 2, SUBLANES, S5_SLAB_STATE))
    return {
        "s5_b": b_mat, "s5_c": c_mat, "s5_tab": tab,
        "s5_d": d_skip.astype(F32).reshape(1, S5_WIDTH),
        "glu_w": glu_w.astype(BF16), "glu_b": glu_b.astype(F32).reshape(1, S5_WIDTH),
    }


NSA_CMP_SLOTS = 128
NSA_Q_TILE = 256
NSA_K_TILE = 256
LOG2_E = math.log2(math.e)
ODD_Q = S5_WIDTH // LANES
ODD_KC = ODD_Q + NSA_HEADS
ODD_KS = ODD_KC + NSA_KV_HEADS
ODD_KW = ODD_KS + NSA_KV_HEADS
ODD_VC = ODD_KW + NSA_KV_HEADS
ODD_VS = ODD_VC + NSA_KV_HEADS
ODD_VW = ODD_VS + NSA_KV_HEADS
ODD_GL = ODD_VW + NSA_KV_HEADS
ODD_END = ODD_GL + NSA_KV_HEADS
ODD_COLS = -(-ODD_END * LANES // PROJ_TILE) * PROJ_TILE


def _compress_kernel(x_ref, w1a_ref, w1b_ref, pos_ref, b1_ref, w2_ref, o_ref):
    half = NSA_CMP_STRIDE * NSA_HEAD_DIM
    xc = x_ref[...].astype(F32)
    xa = (xc + pos_ref[:, 0:half]).astype(BF16)
    xb = (xc + pos_ref[:, half:2 * half]).astype(BF16)
    p1 = jnp.dot(xa, w1a_ref[...], preferred_element_type=F32)
    p2 = jnp.dot(xb, w1b_ref[...], preferred_element_type=F32)
    n = p2.shape[0]
    rows = lax.broadcasted_iota(jnp.int32, p2.shape, 0)
    p2_next = jnp.where(rows < n - 1, pltpu.roll(p2, n - 1, 0), 0.0)
    h = _gelu(p1 + p2_next + b1_ref[...])
    o_ref[...] = jnp.dot(h.astype(BF16), w2_ref[...], preferred_element_type=F32).astype(o_ref.dtype)


def _compress(z, p, batch, seq):
    nchunk = seq // NSA_CMP_STRIDE
    kc = z[:, :, ODD_KC * LANES:(ODD_KC + NSA_KV_HEADS) * LANES]
    vc = z[:, :, ODD_VC * LANES:(ODD_VC + NSA_KV_HEADS) * LANES]
    xc = jnp.stack([kc, vc], axis=1).reshape(batch, 2, nchunk, NSA_CMP_STRIDE, NSA_KV_HEADS, NSA_HEAD_DIM)
    xc = xc.transpose(0, 1, 4, 2, 3, 5).reshape(batch, 2, NSA_KV_HEADS, nchunk, NSA_CMP_STRIDE * NSA_HEAD_DIM)
    half = NSA_CMP_STRIDE * NSA_HEAD_DIM
    kind = lambda shape: pl.BlockSpec((None,) + shape, lambda b, c, h: (c,) + (0,) * len(shape))
    return pl.pallas_call(
        _compress_kernel,
        grid=(batch, 2, NSA_KV_HEADS),
        in_specs=[
            pl.BlockSpec((None, None, None, nchunk, half), lambda b, c, h: (b, c, h, 0, 0)),
            kind((half, NSA_HEAD_DIM)), kind((half, NSA_HEAD_DIM)),
            kind((1, 2 * half)), kind((1, NSA_HEAD_DIM)), kind((NSA_HEAD_DIM, NSA_HEAD_DIM)),
        ],
        out_specs=pl.BlockSpec((None, None, None, nchunk, NSA_HEAD_DIM), lambda b, c, h: (b, c, h, 0, 0)),
        out_shape=jax.ShapeDtypeStruct((batch, 2, NSA_KV_HEADS, nchunk, NSA_HEAD_DIM), BF16),
        compiler_params=_params("parallel", "arbitrary", "arbitrary"),
        name="compress",
    )(xc, p["cmp_w1a"], p["cmp_w1b"], p["cmp_pos"], p["cmp_b1"], p["cmp_w2"])


def _nt_dot(a, b):
    return lax.dot_general(a, b, (((1,), (1,)), ((), ())), preferred_element_type=F32)


def _tn_dot(a, b):
    return lax.dot_general(a, b, (((0,), (0,)), ((), ())), preferred_element_type=F32)


def _nsa_kernel(q_ref, kc_ref, vc_ref, ks_ref, vs_ref, kw_ref, vw_ref, gl_ref, o_ref, acc_ref, accw_ref, *, ncmp, nslc):
    i = pl.program_id(2)
    tq, tk, grp, d = NSA_Q_TILE, NSA_K_TILE, NSA_GROUP, NSA_HEAD_DIM
    rows = grp * tq
    scale = d ** -0.5
    t0 = i * tq
    q = jnp.concatenate([q_ref[:, g * d:(g + 1) * d] for g in range(grp)], axis=0)
    t_q = t0 + lax.broadcasted_iota(jnp.int32, (1, tq), 1)
    t_all = jnp.concatenate([t_q] * grp, axis=1)

    s = _nt_dot(kc_ref[...], q) * scale
    cidx = lax.broadcasted_iota(jnp.int32, (ncmp, 1), 0)
    vis = (cidx * NSA_CMP_STRIDE + (NSA_CMP_BLOCK - 1)) <= t_all
    s = jnp.where(vis, s, NEG_BIG)
    mx = jnp.max(s, axis=0, keepdims=True)
    e = jnp.where(vis, jnp.exp(s - mx), 0.0)
    p_cmp = e / jnp.maximum(jnp.sum(e, axis=0, keepdims=True), 1.0)
    o_cmp = _tn_dot(vc_ref[...], p_cmp.astype(BF16))

    p_sum = p_cmp[:, 0:tq]
    for g in range(1, grp):
        p_sum = p_sum + p_cmp[:, g * tq:(g + 1) * tq]
    sj = lax.broadcasted_iota(jnp.int32, (nslc, ncmp), 0) * NSA_SLC_BLOCK
    ci = lax.broadcasted_iota(jnp.int32, (nslc, ncmp), 1) * NSA_CMP_STRIDE
    overlap = jnp.where((ci < sj + NSA_SLC_BLOCK) & (ci + NSA_CMP_BLOCK > sj), 1.0, 0.0).astype(BF16)
    p_hi = p_sum.astype(BF16)
    p_lo = (p_sum - p_hi.astype(F32)).astype(BF16)
    imp = (jnp.dot(overlap, p_hi, preferred_element_type=F32)
           + jnp.dot(overlap, p_lo, preferred_element_type=F32))

    blk = lax.broadcasted_iota(jnp.int32, (nslc, tq), 0)
    cur = t_q // NSA_SLC_BLOCK
    forced = (blk == 0) | (blk == cur) | (blk == cur - 1)
    score = jnp.where(blk > cur, -jnp.inf, jnp.where(forced, jnp.inf, imp))
    sel = jnp.zeros((nslc, tq), F32)
    for _ in range(NSA_TOP_N):
        best = jnp.max(score, axis=0, keepdims=True)
        pick = jnp.min(jnp.where(score == best, blk, nslc), axis=0, keepdims=True)
        hit = blk == pick
        sel = jnp.where(hit, 1.0, sel)
        score = jnp.where(hit, -jnp.inf, score)
    sel = sel.astype(BF16)

    krow = lax.broadcasted_iota(jnp.int32, (tk, 1), 0)
    bcol = lax.broadcasted_iota(jnp.int32, (tk, nslc), 1)

    def slc_mask(k0):
        kpos = k0 + krow
        expand = jnp.where((kpos // NSA_SLC_BLOCK) == bcol, 1.0, 0.0).astype(BF16)
        chosen = jnp.dot(expand, sel, preferred_element_type=F32) > 0.5
        return chosen & (kpos <= t_q)

    def win_mask(k0):
        kpos = k0 + krow
        return (kpos <= t_q) & (kpos > t_q - NSA_WINDOW)

    def tile_scores(k_ref, mask_fn, k0):
        bias_q = jnp.where(mask_fn(k0), 0.0, NEG_BIG)
        bias = jnp.concatenate([bias_q] * grp, axis=1)
        return _nt_dot(k_ref[pl.ds(k0, tk), :], q) * (scale * LOG2_E) + bias

    def tile_update(sc, carry, v_ref, acc, k0):
        m_old, l_old = carry
        m_new = jnp.maximum(m_old, jnp.max(sc, axis=0, keepdims=True))
        alpha = jnp.exp2(m_old - m_new)
        pr = jnp.exp2(sc - m_new)
        l_new = alpha * l_old + jnp.sum(pr, axis=0, keepdims=True)
        acc[...] = alpha * acc[...] + _tn_dot(v_ref[pl.ds(k0, tk), :], pr.astype(BF16))
        return m_new, l_new

    def slc_body(jt, carry):
        k0 = pl.multiple_of(jt * tk, tk)
        return tile_update(tile_scores(ks_ref, slc_mask, k0), carry, vs_ref, acc_ref, k0)

    def both_body(jt, carry):
        k0 = pl.multiple_of(jt * tk, tk)
        sc_s = tile_scores(ks_ref, slc_mask, k0)
        sc_w = tile_scores(kw_ref, win_mask, k0)
        return (tile_update(sc_s, carry[0], vs_ref, acc_ref, k0),
                tile_update(sc_w, carry[1], vw_ref, accw_ref, k0))

    acc_ref[...] = jnp.zeros_like(acc_ref)
    accw_ref[...] = jnp.zeros_like(accw_ref)
    init = (jnp.full((1, rows), NEG_BIG, F32), jnp.zeros((1, rows), F32))
    hi = (t0 + tq) // tk
    win_lo = jnp.maximum(t0 - NSA_WINDOW, 0) // tk
    state_s = lax.fori_loop(0, win_lo, slc_body, init)
    (_, l_s), (_, l_w) = lax.fori_loop(win_lo, hi, both_body, (state_s, init))
    o_slc = acc_ref[...] / l_s
    o_win = accw_ref[...] / l_w

    gates = jax.nn.sigmoid(gl_ref[...].astype(F32)).T
    for g in range(grp):
        cs = slice(g * tq, (g + 1) * tq)
        o_g = (gates[3 * g:3 * g + 1, :] * o_cmp[:, cs]
               + gates[3 * g + 1:3 * g + 2, :] * o_slc[:, cs]
               + gates[3 * g + 2:3 * g + 3, :] * o_win[:, cs])
        o_ref[:, g * d:(g + 1) * d] = o_g.T.astype(o_ref.dtype)


def _nsa(z, kvc, batch, seq):
    tq, d, grp = NSA_Q_TILE, NSA_HEAD_DIM, NSA_GROUP
    ncmp = kvc.shape[3]
    kern = functools.partial(_nsa_kernel, ncmp=ncmp, nslc=seq // NSA_SLC_BLOCK)
    qw = grp * d

    def seq_spec(col0):
        return pl.BlockSpec((None, seq, d), lambda b, h, i: (b, 0, col0 + h))

    def cmp_spec(c):
        return pl.BlockSpec((None, None, None, ncmp, d), lambda b, h, i: (b, c, h, 0, 0))

    return pl.pallas_call(
        kern,
        grid=(batch, NSA_KV_HEADS, seq // tq),
        in_specs=[
            pl.BlockSpec((None, tq, qw), lambda b, h, i: (b, i, ODD_Q * LANES // qw + h)),
            cmp_spec(0), cmp_spec(1),
            seq_spec(ODD_KS), seq_spec(ODD_VS), seq_spec(ODD_KW), seq_spec(ODD_VW),
            pl.BlockSpec((None, tq, LANES), lambda b, h, i: (b, i, ODD_GL + h)),
        ],
        out_specs=pl.BlockSpec((None, tq, qw), lambda b, h, i: (b, i, h)),
        out_shape=jax.ShapeDtypeStruct((batch, seq, NSA_HEADS * d), BF16),
        scratch_shapes=[pltpu.VMEM((d, grp * tq), F32), pltpu.VMEM((d, grp * tq), F32)],
        compiler_params=_params("parallel", "parallel", "arbitrary"),
        name="nsa",
    )(z, kvc, kvc, z, z, z, z, z)


def _odd_params(w_in, a_re, a_im, log_dt, b_re, b_im, c_re, c_im, d_skip, glu_w, glu_b,
                cmp_pos, cmp_w1, cmp_b1, cmp_w2, seq):
    kvw = NSA_KV_WIDTH
    qw = NSA_HEADS * NSA_HEAD_DIM
    o_q = S5_WIDTH
    o_kv = o_q + qw
    parts = {name: w_in[:, o_kv + n * kvw:o_kv + (n + 1) * kvw]
             for n, name in enumerate(("kc", "vc", "ks", "vs", "kw", "vw"))}
    w_gl = w_in[:, o_kv + 6 * kvw:].reshape(D_MODEL, NSA_KV_HEADS, 3 * NSA_GROUP)
    w_gl = jnp.pad(w_gl, ((0, 0), (0, 0), (0, LANES - 3 * NSA_GROUP))).reshape(D_MODEL, NSA_KV_HEADS * LANES)
    w = jnp.concatenate([w_in[:, :o_kv], parts["kc"], parts["ks"], parts["kw"],
                         parts["vc"], parts["vs"], parts["vw"], w_gl], axis=1).astype(BF16)
    w = jnp.pad(w, ((0, 0), (0, ODD_COLS - w.shape[1])))
    w = w.reshape(D_MODEL, ODD_COLS // PROJ_TILE, PROJ_TILE).transpose(1, 0, 2)
    half = NSA_CMP_STRIDE * NSA_HEAD_DIM
    w1 = cmp_w1.astype(BF16).reshape(2, NSA_CMP_BLOCK * NSA_HEAD_DIM, NSA_HEAD_DIM)
    p = _s5_params(a_re, a_im, log_dt, b_re, b_im, c_re, c_im, d_skip, glu_w, glu_b)
    p.update({
        "w_in": w[None],
        "cmp_w1a": w1[:, :half], "cmp_w1b": w1[:, half:],
        "cmp_pos": cmp_pos.astype(F32).reshape(2, 1, NSA_CMP_BLOCK * NSA_HEAD_DIM),
        "cmp_b1": cmp_b1.astype(F32).reshape(2, 1, NSA_HEAD_DIM),
        "cmp_w2": cmp_w2.astype(BF16),
        "tables": _rope_tables(seq, NSA_HEAD_DIM, (ODD_VC * LANES) % PROJ_TILE),
    })
    return p


def _odd_mixer(x, g_pre, g_post, p, w_out, layer, batch, seq):
    m = batch * seq
    rope_lo = ODD_Q * LANES // PROJ_TILE
    rope_hi = -(-ODD_VC * LANES // PROJ_TILE)
    z = _proj(x, g_pre, p["w_in"], 0, p["tables"], seq, rope_lo, rope_hi)
    z3 = z.reshape(batch, seq, z.shape[1])
    s5_out = _s5(z3, p, batch, seq)
    kvc = _compress(z3, p, batch, seq)
    o = _nsa(z3, kvc, batch, seq)
    return _outproj(s5_out.reshape(m, -1), o.reshape(m, -1), w_out, layer, x, g_post)


def kernel(x, norm_gains, ffn1_w_gate, ffn1_w_up, ffn1_w_down, ffn2_w_gate, ffn2_w_up, ffn2_w_down, ev_w_in, ev_w_out, pool_w, pool_scale, swa_sinks, od_w_in, od_w_out, s5_a_re, s5_a_im, s5_log_dt, s5_b_re, s5_b_im, s5_c_re, s5_c_im, s5_d, s5_glu_w, s5_glu_b, nsa_cmp_pos, nsa_cmp_w1, nsa_cmp_b1, nsa_cmp_w2):
    batch, seq, _ = x.shape
    m = batch * seq
    depth = norm_gains.shape[0]
    xs = x.reshape(m, D_MODEL)
    even_tables = _rope_tables(seq, SWA_HEAD_DIM, SWA_KV_HEADS * SWA_HEAD_DIM)
    ffn_w = [(_cast_bf16(wg, cols_out=D_FF_PAD, col_tile=FF_TILE), _cast_bf16(wu, cols_out=D_FF_PAD, col_tile=FF_TILE),
              _cast_bf16(wd, rows_out=D_FF_PAD))
             for wg, wu, wd in ((ffn1_w_gate, ffn1_w_up, ffn1_w_down), (ffn2_w_gate, ffn2_w_up, ffn2_w_down))]
    ev_in = _cast_bf16(ev_w_in, col_tile=PROJ_TILE)
    ev_out, od_out = _cast_bf16(ev_w_out), _cast_bf16(od_w_out)
    for layer in range(depth):
        g = norm_gains[layer].astype(F32).reshape(6, 1, D_MODEL)
        i = layer // 2
        xs = _ffn(xs, g[0], *ffn_w[0], g[1], layer)
        if layer % 2 == 0:
            xs = _even_mixer(xs, g[2], g[3], ev_in, ev_out, i,
                             pool_w[i].astype(BF16), pool_scale[i].astype(F32).reshape(1, POOL_WIDTH),
                             swa_sinks[i].astype(F32), even_tables, batch, seq)
        else:
            p = _odd_params(od_w_in[i], s5_a_re[i], s5_a_im[i], s5_log_dt[i], s5_b_re[i], s5_b_im[i],
                            s5_c_re[i], s5_c_im[i], s5_d[i], s5_glu_w[i], s5_glu_b[i], nsa_cmp_pos[i],
                            nsa_cmp_w1[i], nsa_cmp_b1[i], nsa_cmp_w2[i], seq)
            xs = _odd_mixer(xs, g[2], g[3], p, od_out, i, batch, seq)
        xs = _ffn(xs, g[4], *ffn_w[1], g[5], layer)
    return xs.reshape(batch, seq, D_MODEL)
```

```python
import functools
import math

import numpy as np
import jax
import jax.numpy as jnp
from jax import lax
from jax.experimental import pallas as pl
from jax.experimental.pallas import tpu as pltpu

F32 = jnp.float32
BF16 = jnp.bfloat16

D_MODEL = 4096
D_FF = 5504
NORM_EPS = 1e-6
ROPE_THETA = 500000.0
ROPE_FRACTION = 4
POOL_WINDOWS = (2, 4, 8, 16)
POOL_WIDTH = D_MODEL // 2
POOL_GROUP = POOL_WIDTH // len(POOL_WINDOWS)
POOL_HALO = 16
SWA_HEAD_DIM = 64
SWA_HEADS = 32
SWA_KV_HEADS = 4
SWA_GROUP = SWA_HEADS // SWA_KV_HEADS
SWA_WINDOW = 128
ATTN_BLOCK = 128
S5_WIDTH = D_MODEL // 4
S5_GROUP_CH = 16
S5_GROUPS = S5_WIDTH // S5_GROUP_CH
S5_STATE = 64
NSA_HEAD_DIM = 128
NSA_HEADS = 24
NSA_KV_HEADS = 6
NSA_GROUP = 4
NSA_CMP_BLOCK = 32
NSA_CMP_STRIDE = 16
NSA_SLC_BLOCK = 64
NSA_TOP_N = 8
NSA_WINDOW = 512
NSA_KV_WIDTH = NSA_KV_HEADS * NSA_HEAD_DIM

LANES = 128
SUBLANES = 8
MXU_DIM = 256
VMEM_LIMIT_BYTES = 56 * 1024 * 1024

ROW_TILE = 512
FF_TILE = 256
D_FF_PAD = 5632
PROJ_ROW_TILE = 256
OUT_ROW_TILE = 256
OUT_COL_CHUNK = 512
OUT_VMEM_LIMIT_BYTES = 60 * 1024 * 1024
FFN_NEXT_VMEM_LIMIT_BYTES = 60 * 1024 * 1024
NORM_ROWS = 32
NORM_UNROLL = 2
CAST_ROWS = 256
NEG_BIG = -1e30


def _params(*sem, vmem=VMEM_LIMIT_BYTES):
    return pltpu.CompilerParams(dimension_semantics=sem, vmem_limit_bytes=vmem)


def _norm_rows(dst_ref, src_ref, g_ref, rows, res_ref=None, res_scale=1.0, next_ref=None, gnext_ref=None):
    g = g_ref[...]
    g_next = None if next_ref is None else gnext_ref[...]

    def body(c, carry):
        r = pl.multiple_of(c * NORM_ROWS, NORM_ROWS)
        rs = pl.ds(r, NORM_ROWS)
        v = src_ref[rs, :].astype(F32)
        ms = jnp.mean(v * v, axis=-1, keepdims=True)
        y = v * lax.rsqrt(ms + NORM_EPS) * g
        if res_ref is not None:
            y = res_ref[rs, :] + res_scale * y
        dst_ref[rs, :] = y.astype(dst_ref.dtype)
        if next_ref is not None:
            ms_next = jnp.mean(y * y, axis=-1, keepdims=True)
            next_ref[rs, :] = (y * lax.rsqrt(ms_next + NORM_EPS) * g_next).astype(next_ref.dtype)
        return carry

    lax.fori_loop(0, rows // NORM_ROWS, body, 0, unroll=NORM_UNROLL)


def _cast_kernel(x_ref, o_ref, *, tr, rows_in, cols_in, cols_out, col_tile):
    x = x_ref[...]
    if rows_in % tr:
        r = pl.program_id(1) * tr + lax.broadcasted_iota(jnp.int32, (tr, 1), 0)
        x = jnp.where(r < rows_in, x, 0.0)
    y = x.astype(o_ref.dtype)
    if cols_out > cols_in:
        y = jnp.concatenate([y, jnp.zeros((tr, cols_out - cols_in), o_ref.dtype)], axis=1)
    if col_tile is None:
        o_ref[...] = y
    else:
        for j in range(cols_out // col_tile):
            o_ref[j] = y[:, j * col_tile:(j + 1) * col_tile]


def _cast_bf16(w, rows_out=None, cols_out=None, col_tile=None):
    nl, rows_in, cols_in = w.shape
    rows_out = rows_out or rows_in
    cols_out = cols_out or cols_in
    tr = min(CAST_ROWS, rows_out)
    kern = functools.partial(_cast_kernel, tr=tr, rows_in=rows_in, cols_in=cols_in, cols_out=cols_out,
                             col_tile=col_tile)
    if col_tile is None:
        out_spec = pl.BlockSpec((None, tr, cols_out), lambda l, i: (l, i, 0))
        out_shape = (nl, rows_out, cols_out)
    else:
        nt = cols_out // col_tile
        out_spec = pl.BlockSpec((None, nt, tr, col_tile), lambda l, i: (l, 0, i, 0))
        out_shape = (nl, nt, rows_out, col_tile)
    return pl.pallas_call(
        kern,
        grid=(nl, rows_out // tr),
        in_specs=[pl.BlockSpec((None, tr, cols_in), lambda l, i: (l, i, 0))],
        out_specs=out_spec,
        out_shape=jax.ShapeDtypeStruct(out_shape, BF16),
        compiler_params=_params("parallel", "parallel"),
        name="cast",
    )(w)


def _ffn_kernel(x_ref, gpre_ref, wg_ref, wu_ref, wd_ref, gpost_ref, *rest, tm, nj):
    gnext_ref = rest[0] if len(rest) == 3 else None
    o_ref, h_ref = rest[-2:]
    j = pl.program_id(1)

    @pl.when(j == 0)
    def _():
        _norm_rows(h_ref, x_ref, gpre_ref, tm)
        o_ref[...] = jnp.zeros_like(o_ref)

    h = h_ref[...]
    gate = jnp.dot(h, wg_ref[...], preferred_element_type=F32)
    up = jnp.dot(h, wu_ref[...], preferred_element_type=F32)
    act = (gate * jax.nn.sigmoid(gate) * up).astype(BF16)
    nc = 512
    for n in range(D_MODEL // nc):
        sl = slice(n * nc, (n + 1) * nc)
        o_ref[:, sl] += jnp.dot(act, wd_ref[:, sl], preferred_element_type=F32)

    @pl.when(j == nj - 1)
    def _():
        _norm_rows(o_ref, o_ref, gpost_ref, tm, res_ref=x_ref, res_scale=0.5,
                   next_ref=None if gnext_ref is None else h_ref, gnext_ref=gnext_ref)


def _ffn(x, g_pre, wg, wu, wd, g_post, layer, g_next=None):
    m = x.shape[0]
    tm = min(ROW_TILE, m)
    nj = D_FF_PAD // FF_TILE
    kern = functools.partial(_ffn_kernel, tm=tm, nj=nj)
    row_spec = pl.BlockSpec((tm, D_MODEL), lambda i, j: (i, 0))
    gain_spec = pl.BlockSpec((1, D_MODEL), lambda i, j: (0, 0))
    in_specs = [
        row_spec, gain_spec,
        pl.BlockSpec((None, None, D_MODEL, FF_TILE), lambda i, j: (layer, j, 0, 0)),
        pl.BlockSpec((None, None, D_MODEL, FF_TILE), lambda i, j: (layer, j, 0, 0)),
        pl.BlockSpec((None, FF_TILE, D_MODEL), lambda i, j: (layer, j, 0)),
        gain_spec,
    ]
    x_shape = jax.ShapeDtypeStruct((m, D_MODEL), F32)
    if g_next is None:
        return pl.pallas_call(
            kern, grid=(m // tm, nj), in_specs=in_specs, out_specs=row_spec, out_shape=x_shape,
            scratch_shapes=[pltpu.VMEM((tm, D_MODEL), BF16)],
            compiler_params=_params("parallel", "arbitrary"), name="ffn",
        )(x, g_pre, wg, wu, wd, g_post)
    return pl.pallas_call(
        kern, grid=(m // tm, nj), in_specs=in_specs + [gain_spec], out_specs=(row_spec, row_spec),
        out_shape=(x_shape, jax.ShapeDtypeStruct((m, D_MODEL), BF16)),
        compiler_params=_params("parallel", "arbitrary", vmem=FFN_NEXT_VMEM_LIMIT_BYTES), name="ffn_next",
    )(x, g_pre, wg, wu, wd, g_post, g_next)


def _proj_kernel(h_ref, w_ref, cos_ref, sa_ref, sb_ref, o_ref, *, rope_chunks, half):
    h = h_ref[...]
    tc = MXU_DIM
    for c in range(o_ref.shape[1] // tc):
        cs = slice(c * tc, (c + 1) * tc)
        z = jnp.dot(h, w_ref[:, cs], preferred_element_type=F32)
        if c in rope_chunks:
            parts = []
            for p in range(tc // LANES):
                zp = z[:, p * LANES:(p + 1) * LANES]
                parts.append(zp * cos_ref[...]
                             + pltpu.roll(zp, LANES - half, 1) * sa_ref[...]
                             + pltpu.roll(zp, half, 1) * sb_ref[...])
            z = jnp.concatenate(parts, axis=1)
        o_ref[:, cs] = z.astype(o_ref.dtype)


def _rope_tables(seq, head_dim):
    rot = head_dim // ROPE_FRACTION
    half = rot // 2
    inv_freq = jnp.power(ROPE_THETA, -jnp.arange(half, dtype=F32) * 2.0 / rot)
    ang = jnp.arange(seq, dtype=jnp.int32).astype(F32)[:, None] * inv_freq[None, :]
    cos, sin = jnp.cos(ang), jnp.sin(ang)
    one = jnp.ones((seq, head_dim - rot), F32)
    zero_h = jnp.zeros((seq, half), F32)
    zero_r = jnp.zeros((seq, head_dim - rot), F32)
    c_head = jnp.concatenate([cos, cos, one], axis=1)
    sa_head = jnp.concatenate([-sin, zero_h, zero_r], axis=1)
    sb_head = jnp.concatenate([zero_h, sin, zero_r], axis=1)
    reps = LANES // head_dim
    return jnp.tile(c_head, (1, reps)), jnp.tile(sa_head, (1, reps)), jnp.tile(sb_head, (1, reps)), half


def _proj(h, w, tables, seq, rope_cols):
    cos_t, sa_t, sb_t, half = tables
    m = h.shape[0]
    gw = w.shape[1]
    tm = min(PROJ_ROW_TILE, seq)
    sblocks = seq // tm
    rope_chunks = frozenset(range(rope_cols[0] // MXU_DIM, rope_cols[1] // MXU_DIM))
    kern = functools.partial(_proj_kernel, rope_chunks=rope_chunks, half=half)
    tab_spec = pl.BlockSpec((tm, LANES), lambda i: (i % sblocks, 0))
    return pl.pallas_call(
        kern,
        grid=(m // tm,),
        in_specs=[
            pl.BlockSpec((tm, D_MODEL), lambda i: (i, 0)),
            pl.BlockSpec((D_MODEL, gw), lambda i: (0, 0), pipeline_mode=pl.Buffered(1)),
            tab_spec, tab_spec, tab_spec,
        ],
        out_specs=pl.BlockSpec((tm, gw), lambda i: (i, 0)),
        out_shape=jax.ShapeDtypeStruct((m, gw), BF16),
        compiler_params=_params("parallel"),
        name="proj",
    )(h, w, cos_t, sa_t, sb_t)


def _outproj_kernel(a1_ref, a2_ref, w_ref, g_ref, x_ref, o_ref, *, tm, k1):
    a1 = a1_ref[...]
    a2 = a2_ref[...]
    nc = OUT_COL_CHUNK
    for n in range(D_MODEL // nc):
        sl = slice(n * nc, (n + 1) * nc)
        o_ref[:, sl] = (jnp.dot(a1, w_ref[0:k1, sl], preferred_element_type=F32)
                        + jnp.dot(a2, w_ref[k1:, sl], preferred_element_type=F32))
    _norm_rows(o_ref, o_ref, g_ref, tm, res_ref=x_ref, res_scale=1.0)


def _outproj(a1, a2, w, layer, x, g_post):
    m = x.shape[0]
    tm = min(OUT_ROW_TILE, m)
    k1, k2 = a1.shape[1], a2.shape[1]
    kern = functools.partial(_outproj_kernel, tm=tm, k1=k1)
    return pl.pallas_call(
        kern,
        grid=(m // tm,),
        in_specs=[
            pl.BlockSpec((tm, k1), lambda i: (i, 0)),
            pl.BlockSpec((tm, k2), lambda i: (i, 0)),
            pl.BlockSpec((None, k1 + k2, D_MODEL), lambda i: (layer, 0, 0), pipeline_mode=pl.Buffered(1)),
            pl.BlockSpec((1, D_MODEL), lambda i: (0, 0)),
            pl.BlockSpec((tm, D_MODEL), lambda i: (i, 0)),
        ],
        out_specs=pl.BlockSpec((tm, D_MODEL), lambda i: (i, 0)),
        out_shape=jax.ShapeDtypeStruct((m, D_MODEL), F32),
        compiler_params=_params("parallel", vmem=OUT_VMEM_LIMIT_BYTES),
        name="outproj",
    )(a1, a2, w, g_post, x)


def _swa_kernel(sink_ref, q_ref, kvp_ref, kvc_ref, o_ref):
    n = pl.program_id(1)
    blk = ATTN_BLOCK
    kvw = SWA_KV_HEADS * SWA_HEAD_DIM
    kv = jnp.concatenate([kvp_ref[...], kvc_ref[...]], axis=0)
    krow = lax.broadcasted_iota(jnp.int32, (2 * blk, 1), 0)
    qcol = lax.broadcasted_iota(jnp.int32, (1, blk), 1)
    diff = qcol - krow + blk
    vis = (diff >= 0) & (diff < SWA_WINDOW) & ((krow >= blk) | (n > 0))
    bias = jnp.where(vis, 0.0, NEG_BIG)
    scale = SWA_HEAD_DIM ** -0.5 * LOG2_E
    pair = LANES // SWA_HEAD_DIM
    for kh in range(SWA_KV_HEADS):
        k_h = kv[:, kh * SWA_HEAD_DIM:(kh + 1) * SWA_HEAD_DIM]
        v_h = kv[:, kvw + kh * SWA_HEAD_DIM:kvw + (kh + 1) * SWA_HEAD_DIM]
        heads = range(kh * SWA_GROUP, (kh + 1) * SWA_GROUP)
        scores = [_nt_dot(k_h, q_ref[:, h * SWA_HEAD_DIM:(h + 1) * SWA_HEAD_DIM]) for h in heads]
        weights, denoms = [], []
        for h, s in zip(heads, scores):
            s = s * scale + bias
            sk = sink_ref[h] * LOG2_E
            mx = jnp.maximum(jnp.max(s, axis=0, keepdims=True), sk)
            e = jnp.exp2(s - mx)
            denoms.append(jnp.sum(e, axis=0, keepdims=True) + jnp.exp2(sk - mx))
            weights.append(e.astype(BF16))
        outs = [_tn_dot(v_h, e) / d for e, d in zip(weights, denoms)]
        for g0 in range(0, SWA_GROUP, pair):
            h0 = kh * SWA_GROUP + g0
            o_ref[:, h0 * SWA_HEAD_DIM:(h0 + pair) * SWA_HEAD_DIM] = (
                jnp.concatenate(outs[g0:g0 + pair], axis=0).T.astype(o_ref.dtype))


def _swa(zq, za, sinks, batch, seq):
    blk = ATTN_BLOCK
    nblk = seq // blk
    qw = SWA_HEADS * SWA_HEAD_DIM
    kvw2 = 2 * SWA_KV_HEADS * SWA_HEAD_DIM
    kv_blk = POOL_WIDTH // kvw2
    return pl.pallas_call(
        _swa_kernel,
        grid=(batch, nblk),
        in_specs=[
            pl.BlockSpec(memory_space=pltpu.SMEM),
            pl.BlockSpec((None, blk, qw), lambda b, n: (b, n, 0)),
            pl.BlockSpec((None, blk, kvw2), lambda b, n: (b, jnp.maximum(n - 1, 0), kv_blk)),
            pl.BlockSpec((None, blk, kvw2), lambda b, n: (b, n, kv_blk)),
        ],
        out_specs=pl.BlockSpec((blk, qw), lambda b, n: (b * nblk + n, 0)),
        out_shape=jax.ShapeDtypeStruct((batch * seq, qw), BF16),
        compiler_params=_params("parallel", "arbitrary"),
        name="swa",
    )(sinks, zq, za, za)


def _pool_kernel(up_ref, uc_ref, w_ref, scale_ref, o_ref, *, ts):
    i = pl.program_id(1)
    has_prev = (i > 0).astype(F32)
    pos = i * ts + lax.broadcasted_iota(jnp.int32, (ts, 1), 0)
    for gi, win in enumerate(POOL_WINDOWS):
        sl = slice(gi * POOL_GROUP, (gi + 1) * POOL_GROUP)
        cur = uc_ref[:, sl].astype(F32)
        prev = up_ref[:, sl].astype(F32) * has_prev
        acc = jnp.concatenate([prev, cur], axis=0)
        d = 1
        while d < win:
            acc = acc + pltpu.roll(acc, d, 0)
            d *= 2
        wsum = acc[POOL_HALO:, :]
        count = jnp.minimum(pos + 1, win).astype(F32)
        zz = (wsum / count - cur).astype(BF16)
        a = jnp.dot(zz, w_ref[gi], preferred_element_type=F32) * scale_ref[:, sl]
        o_ref[:, sl] = a.astype(o_ref.dtype)


def _pool(za, w_pool, pool_scale, batch, seq):
    ts = min(256, seq)
    nts = seq // ts
    halo_blocks = ts // POOL_HALO
    kern = functools.partial(_pool_kernel, ts=ts)
    return pl.pallas_call(
        kern,
        grid=(batch, nts),
        in_specs=[
            pl.BlockSpec((None, POOL_HALO, POOL_WIDTH), lambda b, i: (b, jnp.maximum(i * halo_blocks - 1, 0), 0)),
            pl.BlockSpec((None, ts, POOL_WIDTH), lambda b, i: (b, i, 0)),
            pl.BlockSpec((len(POOL_WINDOWS), POOL_GROUP, POOL_GROUP), lambda b, i: (0, 0, 0)),
            pl.BlockSpec((1, POOL_WIDTH), lambda b, i: (0, 0)),
        ],
        out_specs=pl.BlockSpec((ts, POOL_WIDTH), lambda b, i: (b * nts + i, 0)),
        out_shape=jax.ShapeDtypeStruct((batch * seq, POOL_WIDTH), BF16),
        compiler_params=_params("parallel", "arbitrary"),
        name="pool",
    )(za, za, w_pool, pool_scale)


def _even_weights(w_in):
    qw = SWA_HEADS * SWA_HEAD_DIM
    w_a = jnp.concatenate([w_in[:, :POOL_WIDTH], w_in[:, POOL_WIDTH + qw:]], axis=1).astype(BF16)
    return w_a, w_in[:, POOL_WIDTH:POOL_WIDTH + qw].astype(BF16)


def _even_mixer(x, h, g_post, w_a, w_q, w_out, layer, w_pool, pool_scale, sinks, tables, batch, seq):
    qw = SWA_HEADS * SWA_HEAD_DIM
    k_cols = (POOL_WIDTH, POOL_WIDTH + SWA_KV_HEADS * SWA_HEAD_DIM)
    za = _proj(h, w_a, tables, seq, k_cols).reshape(batch, seq, -1)
    zq = _proj(h, w_q, tables, seq, (0, qw)).reshape(batch, seq, -1)
    o = _swa(zq, za, sinks, batch, seq)
    a = _pool(za, w_pool, pool_scale, batch, seq)
    return _outproj(a, o, w_out, layer, x, g_post)


S5_SLAB = LANES
S5_SLABS = S5_WIDTH // S5_SLAB
S5_SLAB_STATE = (S5_SLAB // S5_GROUP_CH) * S5_STATE
S5_TIME_TILE = 128
S5_SCAN_UNROLL = 8


def _gelu(y):
    return 0.5 * y * (1.0 + jnp.tanh(math.sqrt(2.0 / math.pi) * (y + 0.044715 * (y * y * y))))


def _s5_kernel(u_ref, b_ref, tab_ref, c_ref, d_ref, gw_ref, gb_ref, o_ref, carry_ref, x_ref, y_ref, *, ts):
    i = pl.program_id(0)
    ns = S5_SLAB_STATE
    nb = SUBLANES
    nl = ns // LANES

    @pl.when(i == 0)
    def _():
        carry_ref[...] = jnp.zeros_like(carry_ref)

    u_all = u_ref[...].reshape(nb * ts, S5_WIDTH)
    for j in range(S5_SLABS):
        ch = slice(j * S5_SLAB, (j + 1) * S5_SLAB)
        u_j = u_all[:, ch]
        bu = jnp.dot(u_j, b_ref[j], preferred_element_type=F32)
        for b in range(nb):
            for c in range(2 * nl):
                x_ref[c, pl.ds(b, ts, stride=nb), :] = bu[b * ts:(b + 1) * ts, c * LANES:(c + 1) * LANES]
        a_re, a_im = tab_ref[j, 0], tab_ref[j, 1]

        def step(t, carry):
            c_re, c_im = carry
            r = pl.multiple_of(t * nb, nb)
            v_re = jnp.concatenate([x_ref[c, pl.ds(r, nb), :] for c in range(nl)], axis=1)
            v_im = jnp.concatenate([x_ref[nl + c, pl.ds(r, nb), :] for c in range(nl)], axis=1)
            n_re = v_re + a_re * c_re - a_im * c_im
            n_im = v_im + a_re * c_im + a_im * c_re
            for c in range(nl):
                x_ref[c, pl.ds(r, nb), :] = n_re[:, c * LANES:(c + 1) * LANES]
                x_ref[nl + c, pl.ds(r, nb), :] = n_im[:, c * LANES:(c + 1) * LANES]
            return n_re, n_im

        c_re, c_im = lax.fori_loop(0, ts, step, (carry_ref[j, :, 0:ns], carry_ref[j, :, ns:2 * ns]),
                                   unroll=S5_SCAN_UNROLL)
        carry_ref[j, :, 0:ns] = c_re
        carry_ref[j, :, ns:2 * ns] = c_im
        states = jnp.concatenate(
            [jnp.concatenate([x_ref[c, pl.ds(b, ts, stride=nb), :] for c in range(2 * nl)], axis=1)
             for b in range(nb)], axis=0)
        y_j = jnp.dot(states.astype(BF16), c_ref[j], preferred_element_type=F32)
        y_ref[:, ch] = _gelu(y_j + d_ref[:, ch] * u_j.astype(F32))

    y = y_ref[...]
    gate = jnp.dot(y.astype(BF16), gw_ref[...], preferred_element_type=F32) + gb_ref[...]
    o_ref[...] = (y * jax.nn.sigmoid(gate)).astype(o_ref.dtype).reshape(nb, ts, S5_WIDTH)


def _s5(z, p, batch, seq):
    assert batch == SUBLANES, "the S5 scan lays the batch out on the 8 sublanes"
    ts = min(S5_TIME_TILE, seq)
    kern = functools.partial(_s5_kernel, ts=ts)
    full = lambda shape: pl.BlockSpec(shape, lambda i: (0,) * len(shape))
    return pl.pallas_call(
        kern,
        grid=(seq // ts,),
        in_specs=[
            pl.BlockSpec((batch, ts, S5_WIDTH), lambda i: (0, i, 0)),
            full((S5_SLABS, S5_SLAB, 2 * S5_SLAB_STATE)),
            full((S5_SLABS, 2, SUBLANES, S5_SLAB_STATE)),
            full((S5_SLABS, 2 * S5_SLAB_STATE, S5_SLAB)),
            full((1, S5_WIDTH)),
            full((S5_WIDTH, S5_WIDTH)),
            full((1, S5_WIDTH)),
        ],
        out_specs=pl.BlockSpec((batch, ts, S5_WIDTH), lambda i: (0, i, 0)),
        out_shape=jax.ShapeDtypeStruct((batch, seq, S5_WIDTH), BF16),
        scratch_shapes=[
            pltpu.VMEM((S5_SLABS, SUBLANES, 2 * S5_SLAB_STATE), F32),
            pltpu.VMEM((2 * S5_SLAB_STATE // LANES, batch * ts, LANES), F32),
            pltpu.VMEM((batch * ts, S5_WIDTH), F32),
        ],
        compiler_params=_params("arbitrary"),
        name="s5",
    )(z, p["s5_b"], p["s5_tab"], p["s5_c"], p["s5_d"], p["glu_w"], p["glu_b"])


def _s5_params(a_re, a_im, log_dt, b_re, b_im, c_re, c_im, d_skip, glu_w, glu_b):
    lam_re = a_re.astype(F32)
    lam_im = a_im.astype(F32)
    dt = jnp.exp(log_dt.astype(F32))[:, None]
    decay = jnp.exp(lam_re * dt)
    abar_re = decay * jnp.cos(lam_im * dt)
    abar_im = decay * jnp.sin(lam_im * dt)
    inv_mag = 1.0 / (lam_re * lam_re + lam_im * lam_im)
    num_re = abar_re - 1.0
    f_re = (num_re * lam_re + abar_im * lam_im) * inv_mag
    f_im = (abar_im * lam_re - num_re * lam_im) * inv_mag
    br = b_re.astype(F32)
    bi = b_im.astype(F32)
    bbar_re = f_re[..., None] * br - f_im[..., None] * bi
    bbar_im = f_re[..., None] * bi + f_im[..., None] * br
    gps = S5_SLAB // S5_GROUP_CH
    eye = jnp.eye(gps, dtype=F32)

    def b_slab(t):
        t = t.reshape(S5_SLABS, gps, S5_STATE, S5_GROUP_CH)
        t = jnp.einsum('jgnh,gk->jghkn', t, eye)
        return t.reshape(S5_SLABS, S5_SLAB, S5_SLAB_STATE)

    def c_slab(t):
        t = t.astype(F32).reshape(S5_SLABS, gps, S5_GROUP_CH, S5_STATE)
        t = jnp.einsum('jghn,gk->jgnkh', t, eye)
        return t.reshape(S5_SLABS, S5_SLAB_STATE, S5_SLAB)

    b_mat = jnp.concatenate([b_slab(bbar_re), b_slab(bbar_im)], axis=2).astype(BF16)
    c_mat = jnp.concatenate([c_slab(c_re), -c_slab(c_im)], axis=1).astype(BF16)

    tab = jnp.stack([abar_re.reshape(S5_SLABS, 1, S5_SLAB_STATE), abar_im.reshape(S5_SLABS, 1, S5_SLAB_STATE)], axis=1)
    tab = jnp.broadcast_to(tab, (S5_SLABS, 2, SUBLANES, S5_SLAB_STATE))
    return {
        "s5_b": b_mat, "s5_c": c_mat, "s5_tab": tab,
        "s5_d": d_skip.astype(F32).reshape(1, S5_WIDTH),
        "glu_w": glu_w.astype(BF16), "glu_b": glu_b.astype(F32).reshape(1, S5_WIDTH),
    }


NSA_CMP_SLOTS = 128
NSA_Q_TILE = 256
NSA_K_TILE = 256
LOG2_E = math.log2(math.e)
ODD_A_KC = S5_WIDTH // LANES
ODD_A_KS = ODD_A_KC + NSA_KV_HEADS
ODD_A_KW = ODD_A_KS + NSA_KV_HEADS
ODD_A_END = ODD_A_KW + NSA_KV_HEADS
ODD_C_VS = NSA_KV_HEADS
ODD_C_VW = 2 * NSA_KV_HEADS
ODD_C_GL = 3 * NSA_KV_HEADS


def _compress_kernel(x_ref, w1a_ref, w1b_ref, pos_ref, b1_ref, w2_ref, o_ref):
    half = NSA_CMP_STRIDE * NSA_HEAD_DIM
    xc = x_ref[...].astype(F32)
    xa = (xc + pos_ref[:, 0:half]).astype(BF16)
    xb = (xc + pos_ref[:, half:2 * half]).astype(BF16)
    p1 = jnp.dot(xa, w1a_ref[...], preferred_element_type=F32)
    p2 = jnp.dot(xb, w1b_ref[...], preferred_element_type=F32)
    n = p2.shape[0]
    rows = lax.broadcasted_iota(jnp.int32, p2.shape, 0)
    p2_next = jnp.where(rows < n - 1, pltpu.roll(p2, n - 1, 0), 0.0)
    h = _gelu(p1 + p2_next + b1_ref[...])
    o_ref[...] = jnp.dot(h.astype(BF16), w2_ref[...], preferred_element_type=F32).astype(o_ref.dtype)


def _compress(za, zc, p, batch, seq):
    nchunk = seq // NSA_CMP_STRIDE
    kc = za[:, :, ODD_A_KC * LANES:ODD_A_KS * LANES]
    vc = zc[:, :, 0:ODD_C_VS * LANES]
    xc = jnp.stack([kc, vc], axis=1).reshape(batch, 2, nchunk, NSA_CMP_STRIDE, NSA_KV_HEADS, NSA_HEAD_DIM)
    xc = xc.transpose(0, 1, 4, 2, 3, 5).reshape(batch, 2, NSA_KV_HEADS, nchunk, NSA_CMP_STRIDE * NSA_HEAD_DIM)
    half = NSA_CMP_STRIDE * NSA_HEAD_DIM
    kind = lambda shape: pl.BlockSpec((None,) + shape, lambda b, c, h: (c,) + (0,) * len(shape))
    return pl.pallas_call(
        _compress_kernel,
        grid=(batch, 2, NSA_KV_HEADS),
        in_specs=[
            pl.BlockSpec((None, None, None, nchunk, half), lambda b, c, h: (b, c, h, 0, 0)),
            kind((half, NSA_HEAD_DIM)), kind((half, NSA_HEAD_DIM)),
            kind((1, 2 * half)), kind((1, NSA_HEAD_DIM)), kind((NSA_HEAD_DIM, NSA_HEAD_DIM)),
        ],
        out_specs=pl.BlockSpec((None, None, None, nchunk, NSA_HEAD_DIM), lambda b, c, h: (b, c, h, 0, 0)),
        out_shape=jax.ShapeDtypeStruct((batch, 2, NSA_KV_HEADS, nchunk, NSA_HEAD_DIM), BF16),
        compiler_params=_params("parallel", "arbitrary", "arbitrary"),
        name="compress",
    )(xc, p["cmp_w1a"], p["cmp_w1b"], p["cmp_pos"], p["cmp_b1"], p["cmp_w2"])


def _nt_dot(a, b):
    return lax.dot_general(a, b, (((1,), (1,)), ((), ())), preferred_element_type=F32)


def _tn_dot(a, b):
    return lax.dot_general(a, b, (((0,), (0,)), ((), ())), preferred_element_type=F32)


def _nsa_kernel(q_ref, kc_ref, vc_ref, ks_ref, vs_ref, kw_ref, vw_ref, gl_ref, o_ref, acc_ref, accw_ref, *, ncmp, nslc):
    i = pl.program_id(2)
    tq, tk, grp, d = NSA_Q_TILE, NSA_K_TILE, NSA_GROUP, NSA_HEAD_DIM
    rows = grp * tq
    scale = d ** -0.5
    t0 = i * tq
    q = jnp.concatenate([q_ref[:, g * d:(g + 1) * d] for g in range(grp)], axis=0)
    t_q = t0 + lax.broadcasted_iota(jnp.int32, (1, tq), 1)
    t_all = jnp.concatenate([t_q] * grp, axis=1)

    s = _nt_dot(kc_ref[...], q) * scale
    cidx = lax.broadcasted_iota(jnp.int32, (ncmp, 1), 0)
    vis = (cidx * NSA_CMP_STRIDE + (NSA_CMP_BLOCK - 1)) <= t_all
    s = jnp.where(vis, s, NEG_BIG)
    mx = jnp.max(s, axis=0, keepdims=True)
    e = jnp.where(vis, jnp.exp(s - mx), 0.0)
    p_cmp = e / jnp.maximum(jnp.sum(e, axis=0, keepdims=True), 1.0)
    o_cmp = _tn_dot(vc_ref[...], p_cmp.astype(BF16))

    p_sum = p_cmp[:, 0:tq]
    for g in range(1, grp):
        p_sum = p_sum + p_cmp[:, g * tq:(g + 1) * tq]
    sj = lax.broadcasted_iota(jnp.int32, (nslc, ncmp), 0) * NSA_SLC_BLOCK
    ci = lax.broadcasted_iota(jnp.int32, (nslc, ncmp), 1) * NSA_CMP_STRIDE
    overlap = jnp.where((ci < sj + NSA_SLC_BLOCK) & (ci + NSA_CMP_BLOCK > sj), 1.0, 0.0).astype(BF16)
    p_hi = p_sum.astype(BF16)
    p_lo = (p_sum - p_hi.astype(F32)).astype(BF16)
    imp = (jnp.dot(overlap, p_hi, preferred_element_type=F32)
           + jnp.dot(overlap, p_lo, preferred_element_type=F32))

    blk = lax.broadcasted_iota(jnp.int32, (nslc, tq), 0)
    cur = t_q // NSA_SLC_BLOCK
    forced = (blk == 0) | (blk == cur) | (blk == cur - 1)
    score = jnp.where(blk > cur, -jnp.inf, jnp.where(forced, jnp.inf, imp))
    sel = jnp.zeros((nslc, tq), F32)
    for _ in range(NSA_TOP_N):
        best = jnp.max(score, axis=0, keepdims=True)
        pick = jnp.min(jnp.where(score == best, blk, nslc), axis=0, keepdims=True)
        hit = blk == pick
        sel = jnp.where(hit, 1.0, sel)
        score = jnp.where(hit, -jnp.inf, score)
    sel = sel.astype(BF16)

    krow = lax.broadcasted_iota(jnp.int32, (tk, 1), 0)
    bcol = lax.broadcasted_iota(jnp.int32, (tk, nslc), 1)

    def slc_mask(k0):
        kpos = k0 + krow
        expand = jnp.where((kpos // NSA_SLC_BLOCK) == bcol, 1.0, 0.0).astype(BF16)
        chosen = jnp.dot(expand, sel, preferred_element_type=F32) > 0.5
        return chosen & (kpos <= t_q)

    def win_mask(k0):
        kpos = k0 + krow
        return (kpos <= t_q) & (kpos > t_q - NSA_WINDOW)

    def tile_scores(k_ref, mask_fn, k0):
        bias_q = jnp.where(mask_fn(k0), 0.0, NEG_BIG)
        bias = jnp.concatenate([bias_q] * grp, axis=1)
        return _nt_dot(k_ref[pl.ds(k0, tk), :], q) * (scale * LOG2_E) + bias

    def tile_update(sc, carry, v_ref, acc, k0):
        m_old, l_old = carry
        m_new = jnp.maximum(m_old, jnp.max(sc, axis=0, keepdims=True))
        alpha = jnp.exp2(m_old - m_new)
        pr = jnp.exp2(sc - m_new)
        l_new = alpha * l_old + jnp.sum(pr, axis=0, keepdims=True)
        acc[...] = alpha * acc[...] + _tn_dot(v_ref[pl.ds(k0, tk), :], pr.astype(BF16))
        return m_new, l_new

    def slc_body(jt, carry):
        k0 = pl.multiple_of(jt * tk, tk)
        return tile_update(tile_scores(ks_ref, slc_mask, k0), carry, vs_ref, acc_ref, k0)

    def both_body(jt, carry):
        k0 = pl.multiple_of(jt * tk, tk)
        sc_s = tile_scores(ks_ref, slc_mask, k0)
        sc_w = tile_scores(kw_ref, win_mask, k0)
        return (tile_update(sc_s, carry[0], vs_ref, acc_ref, k0),
                tile_update(sc_w, carry[1], vw_ref, accw_ref, k0))

    acc_ref[...] = jnp.zeros_like(acc_ref)
    accw_ref[...] = jnp.zeros_like(accw_ref)
    init = (jnp.full((1, rows), NEG_BIG, F32), jnp.zeros((1, rows), F32))
    hi = (t0 + tq) // tk
    win_lo = jnp.maximum(t0 - NSA_WINDOW, 0) // tk
    state_s = lax.fori_loop(0, win_lo, slc_body, init)
    (_, l_s), (_, l_w) = lax.fori_loop(win_lo, hi, both_body, (state_s, init))
    o_slc = acc_ref[...] / l_s
    o_win = accw_ref[...] / l_w

    gates = jax.nn.sigmoid(gl_ref[...].astype(F32)).T
    for g in range(grp):
        cs = slice(g * tq, (g + 1) * tq)
        o_g = (gates[3 * g:3 * g + 1, :] * o_cmp[:, cs]
               + gates[3 * g + 1:3 * g + 2, :] * o_slc[:, cs]
               + gates[3 * g + 2:3 * g + 3, :] * o_win[:, cs])
        o_ref[:, g * d:(g + 1) * d] = o_g.T.astype(o_ref.dtype)


def _nsa(za, zq, zc, kvc, batch, seq):
    tq, d, grp = NSA_Q_TILE, NSA_HEAD_DIM, NSA_GROUP
    ntq = seq // tq
    ncmp = kvc.shape[3]
    kern = functools.partial(_nsa_kernel, ncmp=ncmp, nslc=seq // NSA_SLC_BLOCK)
    qw = grp * d

    def seq_spec(col0):
        return pl.BlockSpec((None, seq, d), lambda b, h, i: (b, 0, col0 + h))

    def cmp_spec(c):
        return pl.BlockSpec((None, None, None, ncmp, d), lambda b, h, i: (b, c, h, 0, 0))

    return pl.pallas_call(
        kern,
        grid=(batch, NSA_KV_HEADS, ntq),
        in_specs=[
            pl.BlockSpec((None, tq, qw), lambda b, h, i: (b, i, h)),
            cmp_spec(0), cmp_spec(1),
            seq_spec(ODD_A_KS), seq_spec(ODD_C_VS), seq_spec(ODD_A_KW), seq_spec(ODD_C_VW),
            pl.BlockSpec((None, tq, LANES), lambda b, h, i: (b, i, ODD_C_GL + h)),
        ],
        out_specs=pl.BlockSpec((tq, qw), lambda b, h, i: (b * ntq + i, h)),
        out_shape=jax.ShapeDtypeStruct((batch * seq, NSA_HEADS * d), BF16),
        scratch_shapes=[pltpu.VMEM((d, grp * tq), F32), pltpu.VMEM((d, grp * tq), F32)],
        compiler_params=_params("parallel", "parallel", "arbitrary"),
        name="nsa",
    )(zq, kvc, kvc, za, zc, za, zc, zc)


def _odd_params(w_in, a_re, a_im, log_dt, b_re, b_im, c_re, c_im, d_skip, glu_w, glu_b,
                cmp_pos, cmp_w1, cmp_b1, cmp_w2, seq):
    kvw = NSA_KV_WIDTH
    qw = NSA_HEADS * NSA_HEAD_DIM
    o_q = S5_WIDTH
    o_kv = o_q + qw
    parts = {name: w_in[:, o_kv + n * kvw:o_kv + (n + 1) * kvw]
             for n, name in enumerate(("kc", "vc", "ks", "vs", "kw", "vw"))}
    w_gl = w_in[:, o_kv + 6 * kvw:].reshape(D_MODEL, NSA_KV_HEADS, 3 * NSA_GROUP)
    w_gl = jnp.pad(w_gl, ((0, 0), (0, 0), (0, LANES - 3 * NSA_GROUP))).reshape(D_MODEL, NSA_KV_HEADS * LANES)
    w_a = jnp.concatenate([w_in[:, :o_q], parts["kc"], parts["ks"], parts["kw"]], axis=1).astype(BF16)
    w_q = w_in[:, o_q:o_kv].astype(BF16)
    w_c = jnp.concatenate([parts["vc"], parts["vs"], parts["vw"], w_gl], axis=1).astype(BF16)
    half = NSA_CMP_STRIDE * NSA_HEAD_DIM
    w1 = cmp_w1.astype(BF16).reshape(2, NSA_CMP_BLOCK * NSA_HEAD_DIM, NSA_HEAD_DIM)
    p = _s5_params(a_re, a_im, log_dt, b_re, b_im, c_re, c_im, d_skip, glu_w, glu_b)
    p.update({
        "w_a": w_a, "w_q": w_q, "w_c": w_c,
        "cmp_w1a": w1[:, :half], "cmp_w1b": w1[:, half:],
        "cmp_pos": cmp_pos.astype(F32).reshape(2, 1, NSA_CMP_BLOCK * NSA_HEAD_DIM),
        "cmp_b1": cmp_b1.astype(F32).reshape(2, 1, NSA_HEAD_DIM),
        "cmp_w2": cmp_w2.astype(BF16),
        "tables": _rope_tables(seq, NSA_HEAD_DIM),
    })
    return p


def _odd_mixer(x, h, g_post, p, w_out, layer, batch, seq):
    m = batch * seq
    tables = p["tables"]
    za = _proj(h, p["w_a"], tables, seq, (ODD_A_KC * LANES, ODD_A_END * LANES)).reshape(batch, seq, -1)
    zq = _proj(h, p["w_q"], tables, seq, (0, NSA_HEADS * NSA_HEAD_DIM)).reshape(batch, seq, -1)
    zc = _proj(h, p["w_c"], tables, seq, (0, 0)).reshape(batch, seq, -1)
    s5_out = _s5(za, p, batch, seq)
    kvc = _compress(za, zc, p, batch, seq)
    o = _nsa(za, zq, zc, kvc, batch, seq)
    return _outproj(s5_out.reshape(m, -1), o, w_out, layer, x, g_post)


def kernel(x, norm_gains, ffn1_w_gate, ffn1_w_up, ffn1_w_down, ffn2_w_gate, ffn2_w_up, ffn2_w_down, ev_w_in, ev_w_out, pool_w, pool_scale, swa_sinks, od_w_in, od_w_out, s5_a_re, s5_a_im, s5_log_dt, s5_b_re, s5_b_im, s5_c_re, s5_c_im, s5_d, s5_glu_w, s5_glu_b, nsa_cmp_pos, nsa_cmp_w1, nsa_cmp_b1, nsa_cmp_w2):
    batch, seq, _ = x.shape
    m = batch * seq
    depth = norm_gains.shape[0]
    xs = x.reshape(m, D_MODEL)
    even_tables = _rope_tables(seq, SWA_HEAD_DIM)
    ffn_w = [(_cast_bf16(wg, cols_out=D_FF_PAD, col_tile=FF_TILE), _cast_bf16(wu, cols_out=D_FF_PAD, col_tile=FF_TILE),
              _cast_bf16(wd, rows_out=D_FF_PAD))
             for wg, wu, wd in ((ffn1_w_gate, ffn1_w_up, ffn1_w_down), (ffn2_w_gate, ffn2_w_up, ffn2_w_down))]
    ev_out, od_out = _cast_bf16(ev_w_out), _cast_bf16(od_w_out)
    for layer in range(depth):
        g = norm_gains[layer].astype(F32).reshape(6, 1, D_MODEL)
        i = layer // 2
        xs, h = _ffn(xs, g[0], *ffn_w[0], g[1], layer, g_next=g[2])
        if layer % 2 == 0:
            xs = _even_mixer(xs, h, g[3], *_even_weights(ev_w_in[i]), ev_out, i,
                             pool_w[i].astype(BF16), pool_scale[i].astype(F32).reshape(1, POOL_WIDTH),
                             swa_sinks[i].astype(F32), even_tables, batch, seq)
        else:
            p = _odd_params(od_w_in[i], s5_a_re[i], s5_a_im[i], s5_log_dt[i], s5_b_re[i], s5_b_im[i],
                            s5_c_re[i], s5_c_im[i], s5_d[i], s5_glu_w[i], s5_glu_b[i], nsa_cmp_pos[i],
                            nsa_cmp_w1[i], nsa_cmp_b1[i], nsa_cmp_w2[i], seq)
            xs = _odd_mixer(xs, h, g[3], p, od_out, i, batch, seq)
        xs = _ffn(xs, g[4], *ffn_w[1], g[5], layer)
    return xs.reshape(batch, seq, D_MODEL)
```

```python
import functools
import math

import numpy as np
import jax
import jax.numpy as jnp
from jax import lax
from jax.experimental import pallas as pl
from jax.experimental.pallas import tpu as pltpu

F32 = jnp.float32
BF16 = jnp.bfloat16

D_MODEL = 4096
D_FF = 5504
NORM_EPS = 1e-6
ROPE_THETA = 500000.0
ROPE_FRACTION = 4
POOL_WINDOWS = (2, 4, 8, 16)
POOL_WIDTH = D_MODEL // 2
POOL_GROUP = POOL_WIDTH // len(POOL_WINDOWS)
POOL_HALO = 16
SWA_HEAD_DIM = 64
SWA_HEADS = 32
SWA_KV_HEADS = 4
SWA_GROUP = SWA_HEADS // SWA_KV_HEADS
SWA_WINDOW = 128
ATTN_BLOCK = 128
S5_WIDTH = D_MODEL // 4
S5_GROUP_CH = 16
S5_GROUPS = S5_WIDTH // S5_GROUP_CH
S5_STATE = 64
NSA_HEAD_DIM = 128
NSA_HEADS = 24
NSA_KV_HEADS = 6
NSA_GROUP = 4
NSA_CMP_BLOCK = 32
NSA_CMP_STRIDE = 16
NSA_SLC_BLOCK = 64
NSA_TOP_N = 8
NSA_WINDOW = 512
NSA_KV_WIDTH = NSA_KV_HEADS * NSA_HEAD_DIM

LANES = 128
SUBLANES = 8
MXU_DIM = 256
VMEM_LIMIT_BYTES = 56 * 1024 * 1024

ROW_TILE = 512
FF_TILE = 256
D_FF_PAD = 5632
PROJ_ROW_TILE = 256
OUT_ROW_TILE = 256
OUT_COL_CHUNK = 512
OUT_VMEM_LIMIT_BYTES = 60 * 1024 * 1024
FFN_NEXT_VMEM_LIMIT_BYTES = 60 * 1024 * 1024
NORM_ROWS = 32
NORM_UNROLL = 2
CAST_ROWS = 256
NEG_BIG = -1e30


def _params(*sem, vmem=VMEM_LIMIT_BYTES):
    return pltpu.CompilerParams(dimension_semantics=sem, vmem_limit_bytes=vmem)


def _rms_scaled(v, g):
    ms = jnp.mean(v * v, axis=-1, keepdims=True)
    return v * lax.rsqrt(ms + NORM_EPS) * g


def _norm_chunk(c):
    return pl.ds(pl.multiple_of(c * NORM_ROWS, NORM_ROWS), NORM_ROWS)


def _norm_rows(dst_ref, src_ref, g_ref, rows):
    g = g_ref[...]

    def body(c, carry):
        rs = _norm_chunk(c)
        dst_ref[rs, :] = _rms_scaled(src_ref[rs, :].astype(F32), g).astype(dst_ref.dtype)
        return carry

    lax.fori_loop(0, rows // NORM_ROWS, body, 0, unroll=NORM_UNROLL)


def _norm_residual_rows(acc_ref, g_ref, res_ref, res_scale, inv_ref, rows, next_ref=None, gnext_ref=None):
    g = g_ref[...]
    g_next = None if next_ref is None else gnext_ref[...]
    reps = acc_ref.shape[1] // LANES

    def stats(c, carry):
        rs = _norm_chunk(c)
        v = acc_ref[rs, :]
        ms = jnp.mean(v * v, axis=-1, keepdims=True)
        inv_ref[rs, :] = jnp.broadcast_to(lax.rsqrt(ms + NORM_EPS), (NORM_ROWS, LANES))
        return carry

    def scale(c, carry):
        rs = _norm_chunk(c)
        y = res_ref[rs, :] + res_scale * (acc_ref[rs, :] * pltpu.repeat(inv_ref[rs, :], reps, axis=1) * g)
        acc_ref[rs, :] = y
        if next_ref is not None:
            next_ref[rs, :] = _rms_scaled(y, g_next).astype(next_ref.dtype)
        return carry

    lax.fori_loop(0, rows // NORM_ROWS, stats, 0, unroll=NORM_UNROLL)
    lax.fori_loop(0, rows // NORM_ROWS, scale, 0, unroll=NORM_UNROLL)


def _cast_kernel(x_ref, o_ref, *, tr, rows_in, cols_in, cols_out, col_tile):
    x = x_ref[...]
    if rows_in % tr:
        r = pl.program_id(1) * tr + lax.broadcasted_iota(jnp.int32, (tr, 1), 0)
        x = jnp.where(r < rows_in, x, 0.0)
    y = x.astype(o_ref.dtype)
    if cols_out > cols_in:
        y = jnp.concatenate([y, jnp.zeros((tr, cols_out - cols_in), o_ref.dtype)], axis=1)
    if col_tile is None:
        o_ref[...] = y
    else:
        for j in range(cols_out // col_tile):
            o_ref[j] = y[:, j * col_tile:(j + 1) * col_tile]


def _cast_bf16(w, rows_out=None, cols_out=None, col_tile=None):
    nl, rows_in, cols_in = w.shape
    rows_out = rows_out or rows_in
    cols_out = cols_out or cols_in
    tr = min(CAST_ROWS, rows_out)
    kern = functools.partial(_cast_kernel, tr=tr, rows_in=rows_in, cols_in=cols_in, cols_out=cols_out,
                             col_tile=col_tile)
    if col_tile is None:
        out_spec = pl.BlockSpec((None, tr, cols_out), lambda l, i: (l, i, 0))
        out_shape = (nl, rows_out, cols_out)
    else:
        nt = cols_out // col_tile
        out_spec = pl.BlockSpec((None, nt, tr, col_tile), lambda l, i: (l, 0, i, 0))
        out_shape = (nl, nt, rows_out, col_tile)
    return pl.pallas_call(
        kern,
        grid=(nl, rows_out // tr),
        in_specs=[pl.BlockSpec((None, tr, cols_in), lambda l, i: (l, i, 0))],
        out_specs=out_spec,
        out_shape=jax.ShapeDtypeStruct(out_shape, BF16),
        compiler_params=_params("parallel", "parallel"),
        name="cast",
    )(w)


def _ffn_kernel(x_ref, gpre_ref, wg_ref, wu_ref, wd_ref, gpost_ref, *rest, tm, nj):
    gnext_ref = rest[0] if len(rest) == 4 else None
    o_ref, h_ref, inv_ref = rest[-3:]
    j = pl.program_id(1)

    @pl.when(j == 0)
    def _():
        _norm_rows(h_ref, x_ref, gpre_ref, tm)
        o_ref[...] = jnp.zeros_like(o_ref)

    h = h_ref[...]
    gate = jnp.dot(h, wg_ref[...], preferred_element_type=F32)
    up = jnp.dot(h, wu_ref[...], preferred_element_type=F32)
    act = (gate * jax.nn.sigmoid(gate) * up).astype(BF16)
    nc = 512
    for n in range(D_MODEL // nc):
        sl = slice(n * nc, (n + 1) * nc)
        o_ref[:, sl] += jnp.dot(act, wd_ref[:, sl], preferred_element_type=F32)

    @pl.when(j == nj - 1)
    def _():
        _norm_residual_rows(o_ref, gpost_ref, x_ref, 0.5, inv_ref, tm,
                            next_ref=None if gnext_ref is None else h_ref, gnext_ref=gnext_ref)


def _ffn(x, g_pre, wg, wu, wd, g_post, layer, g_next=None):
    m = x.shape[0]
    tm = min(ROW_TILE, m)
    nj = D_FF_PAD // FF_TILE
    kern = functools.partial(_ffn_kernel, tm=tm, nj=nj)
    row_spec = pl.BlockSpec((tm, D_MODEL), lambda i, j: (i, 0))
    gain_spec = pl.BlockSpec((1, D_MODEL), lambda i, j: (0, 0))
    in_specs = [
        row_spec, gain_spec,
        pl.BlockSpec((None, None, D_MODEL, FF_TILE), lambda i, j: (layer, j, 0, 0)),
        pl.BlockSpec((None, None, D_MODEL, FF_TILE), lambda i, j: (layer, j, 0, 0)),
        pl.BlockSpec((None, FF_TILE, D_MODEL), lambda i, j: (layer, j, 0)),
        gain_spec,
    ]
    x_shape = jax.ShapeDtypeStruct((m, D_MODEL), F32)
    inv_scratch = pltpu.VMEM((tm, LANES), F32)
    if g_next is None:
        return pl.pallas_call(
            kern, grid=(m // tm, nj), in_specs=in_specs, out_specs=row_spec, out_shape=x_shape,
            scratch_shapes=[pltpu.VMEM((tm, D_MODEL), BF16), inv_scratch],
            compiler_params=_params("parallel", "arbitrary"), name="ffn",
        )(x, g_pre, wg, wu, wd, g_post)
    return pl.pallas_call(
        kern, grid=(m // tm, nj), in_specs=in_specs + [gain_spec], out_specs=(row_spec, row_spec),
        out_shape=(x_shape, jax.ShapeDtypeStruct((m, D_MODEL), BF16)), scratch_shapes=[inv_scratch],
        compiler_params=_params("parallel", "arbitrary", vmem=FFN_NEXT_VMEM_LIMIT_BYTES), name="ffn_next",
    )(x, g_pre, wg, wu, wd, g_post, g_next)


def _proj_kernel(h_ref, w_ref, cos_ref, sa_ref, sb_ref, o_ref, *, rope_chunks, half):
    h = h_ref[...]
    tc = MXU_DIM
    for c in range(o_ref.shape[1] // tc):
        cs = slice(c * tc, (c + 1) * tc)
        z = jnp.dot(h, w_ref[:, cs], preferred_element_type=F32)
        if c in rope_chunks:
            parts = []
            for p in range(tc // LANES):
                zp = z[:, p * LANES:(p + 1) * LANES]
                parts.append(zp * cos_ref[...]
                             + pltpu.roll(zp, LANES - half, 1) * sa_ref[...]
                             + pltpu.roll(zp, half, 1) * sb_ref[...])
            z = jnp.concatenate(parts, axis=1)
        o_ref[:, cs] = z.astype(o_ref.dtype)


def _rope_tables(seq, head_dim):
    rot = head_dim // ROPE_FRACTION
    half = rot // 2
    inv_freq = jnp.power(ROPE_THETA, -jnp.arange(half, dtype=F32) * 2.0 / rot)
    ang = jnp.arange(seq, dtype=jnp.int32).astype(F32)[:, None] * inv_freq[None, :]
    cos, sin = jnp.cos(ang), jnp.sin(ang)
    one = jnp.ones((seq, head_dim - rot), F32)
    zero_h = jnp.zeros((seq, half), F32)
    zero_r = jnp.zeros((seq, head_dim - rot), F32)
    c_head = jnp.concatenate([cos, cos, one], axis=1)
    sa_head = jnp.concatenate([-sin, zero_h, zero_r], axis=1)
    sb_head = jnp.concatenate([zero_h, sin, zero_r], axis=1)
    reps = LANES // head_dim
    return jnp.tile(c_head, (1, reps)), jnp.tile(sa_head, (1, reps)), jnp.tile(sb_head, (1, reps)), half


def _proj(h, w, tables, seq, rope_cols):
    cos_t, sa_t, sb_t, half = tables
    m = h.shape[0]
    gw = w.shape[1]
    tm = min(PROJ_ROW_TILE, seq)
    sblocks = seq // tm
    rope_chunks = frozenset(range(rope_cols[0] // MXU_DIM, rope_cols[1] // MXU_DIM))
    kern = functools.partial(_proj_kernel, rope_chunks=rope_chunks, half=half)
    tab_spec = pl.BlockSpec((tm, LANES), lambda i: (i % sblocks, 0))
    return pl.pallas_call(
        kern,
        grid=(m // tm,),
        in_specs=[
            pl.BlockSpec((tm, D_MODEL), lambda i: (i, 0)),
            pl.BlockSpec((D_MODEL, gw), lambda i: (0, 0), pipeline_mode=pl.Buffered(1)),
            tab_spec, tab_spec, tab_spec,
        ],
        out_specs=pl.BlockSpec((None, tm, gw), lambda i: (i // sblocks, i % sblocks, 0)),
        out_shape=jax.ShapeDtypeStruct((m // seq, seq, gw), BF16),
        compiler_params=_params("parallel"),
        name="proj",
    )(h, w, cos_t, sa_t, sb_t)


def _outproj_kernel(a1_ref, a2_ref, w_ref, g_ref, x_ref, o_ref, inv_ref, *, tm, k1):
    a1 = a1_ref[...]
    a2 = a2_ref[...]
    nc = OUT_COL_CHUNK
    for n in range(D_MODEL // nc):
        sl = slice(n * nc, (n + 1) * nc)
        o_ref[:, sl] = (jnp.dot(a1, w_ref[0:k1, sl], preferred_element_type=F32)
                        + jnp.dot(a2, w_ref[k1:, sl], preferred_element_type=F32))
    _norm_residual_rows(o_ref, g_ref, x_ref, 1.0, inv_ref, tm)


def _outproj(a1, a2, w, layer, x, g_post):
    m = x.shape[0]
    k1, k2 = a1.shape[-1], a2.shape[-1]
    if a1.ndim == 3:
        tm = min(OUT_ROW_TILE, a1.shape[1])
        sb = a1.shape[1] // tm
        a1_spec = pl.BlockSpec((None, tm, k1), lambda i: (i // sb, i % sb, 0))
    else:
        tm = min(OUT_ROW_TILE, m)
        a1_spec = pl.BlockSpec((tm, k1), lambda i: (i, 0))
    kern = functools.partial(_outproj_kernel, tm=tm, k1=k1)
    return pl.pallas_call(
        kern,
        grid=(m // tm,),
        in_specs=[
            a1_spec,
            pl.BlockSpec((tm, k2), lambda i: (i, 0)),
            pl.BlockSpec((None, k1 + k2, D_MODEL), lambda i: (layer, 0, 0), pipeline_mode=pl.Buffered(1)),
            pl.BlockSpec((1, D_MODEL), lambda i: (0, 0)),
            pl.BlockSpec((tm, D_MODEL), lambda i: (i, 0)),
        ],
        out_specs=pl.BlockSpec((tm, D_MODEL), lambda i: (i, 0)),
        out_shape=jax.ShapeDtypeStruct((m, D_MODEL), F32),
        scratch_shapes=[pltpu.VMEM((tm, LANES), F32)],
        compiler_params=_params("parallel", vmem=OUT_VMEM_LIMIT_BYTES),
        name="outproj",
    )(a1, a2, w, g_post, x)


def _swa_kernel(sink_ref, q_ref, kvp_ref, kvc_ref, o_ref):
    n = pl.program_id(1)
    blk = ATTN_BLOCK
    kvw = SWA_KV_HEADS * SWA_HEAD_DIM
    kv = jnp.concatenate([kvp_ref[...], kvc_ref[...]], axis=0)
    krow = lax.broadcasted_iota(jnp.int32, (2 * blk, 1), 0)
    qcol = lax.broadcasted_iota(jnp.int32, (1, blk), 1)
    diff = qcol - krow + blk
    vis = (diff >= 0) & (diff < SWA_WINDOW) & ((krow >= blk) | (n > 0))
    bias = jnp.where(vis, 0.0, NEG_BIG)
    scale = SWA_HEAD_DIM ** -0.5 * LOG2_E
    pair = LANES // SWA_HEAD_DIM
    for kh in range(SWA_KV_HEADS):
        k_h = kv[:, kh * SWA_HEAD_DIM:(kh + 1) * SWA_HEAD_DIM]
        v_h = kv[:, kvw + kh * SWA_HEAD_DIM:kvw + (kh + 1) * SWA_HEAD_DIM]
        heads = range(kh * SWA_GROUP, (kh + 1) * SWA_GROUP)
        scores = [_nt_dot(k_h, q_ref[:, h * SWA_HEAD_DIM:(h + 1) * SWA_HEAD_DIM]) for h in heads]
        weights, denoms = [], []
        for h, s in zip(heads, scores):
            s = s * scale + bias
            sk = sink_ref[h] * LOG2_E
            mx = jnp.maximum(jnp.max(s, axis=0, keepdims=True), sk)
            e = jnp.exp2(s - mx)
            denoms.append(jnp.sum(e, axis=0, keepdims=True) + jnp.exp2(sk - mx))
            weights.append(e.astype(BF16))
        outs = [_tn_dot(v_h, e) / d for e, d in zip(weights, denoms)]
        for g0 in range(0, SWA_GROUP, pair):
            h0 = kh * SWA_GROUP + g0
            o_ref[:, h0 * SWA_HEAD_DIM:(h0 + pair) * SWA_HEAD_DIM] = (
                jnp.concatenate(outs[g0:g0 + pair], axis=0).T.astype(o_ref.dtype))


def _swa(zq, za, sinks, batch, seq):
    blk = ATTN_BLOCK
    nblk = seq // blk
    qw = SWA_HEADS * SWA_HEAD_DIM
    kvw2 = 2 * SWA_KV_HEADS * SWA_HEAD_DIM
    kv_blk = POOL_WIDTH // kvw2
    return pl.pallas_call(
        _swa_kernel,
        grid=(batch, nblk),
        in_specs=[
            pl.BlockSpec(memory_space=pltpu.SMEM),
            pl.BlockSpec((None, blk, qw), lambda b, n: (b, n, 0)),
            pl.BlockSpec((None, blk, kvw2), lambda b, n: (b, jnp.maximum(n - 1, 0), kv_blk)),
            pl.BlockSpec((None, blk, kvw2), lambda b, n: (b, n, kv_blk)),
        ],
        out_specs=pl.BlockSpec((blk, qw), lambda b, n: (b * nblk + n, 0)),
        out_shape=jax.ShapeDtypeStruct((batch * seq, qw), BF16),
        compiler_params=_params("parallel", "arbitrary"),
        name="swa",
    )(sinks, zq, za, za)


def _pool_kernel(up_ref, uc_ref, w_ref, scale_ref, o_ref, *, ts):
    i = pl.program_id(1)
    has_prev = (i > 0).astype(F32)
    pos = i * ts + lax.broadcasted_iota(jnp.int32, (ts, 1), 0)
    for gi, win in enumerate(POOL_WINDOWS):
        sl = slice(gi * POOL_GROUP, (gi + 1) * POOL_GROUP)
        cur = uc_ref[:, sl].astype(F32)
        prev = up_ref[:, sl].astype(F32) * has_prev
        acc = jnp.concatenate([prev, cur], axis=0)
        d = 1
        while d < win:
            acc = acc + pltpu.roll(acc, d, 0)
            d *= 2
        wsum = acc[POOL_HALO:, :]
        count = jnp.minimum(pos + 1, win).astype(F32)
        zz = (wsum / count - cur).astype(BF16)
        a = jnp.dot(zz, w_ref[gi], preferred_element_type=F32) * scale_ref[:, sl]
        o_ref[:, sl] = a.astype(o_ref.dtype)


def _pool(za, w_pool, pool_scale, batch, seq):
    ts = min(256, seq)
    nts = seq // ts
    halo_blocks = ts // POOL_HALO
    kern = functools.partial(_pool_kernel, ts=ts)
    return pl.pallas_call(
        kern,
        grid=(batch, nts),
        in_specs=[
            pl.BlockSpec((None, POOL_HALO, POOL_WIDTH), lambda b, i: (b, jnp.maximum(i * halo_blocks - 1, 0), 0)),
            pl.BlockSpec((None, ts, POOL_WIDTH), lambda b, i: (b, i, 0)),
            pl.BlockSpec((len(POOL_WINDOWS), POOL_GROUP, POOL_GROUP), lambda b, i: (0, 0, 0)),
            pl.BlockSpec((1, POOL_WIDTH), lambda b, i: (0, 0)),
        ],
        out_specs=pl.BlockSpec((ts, POOL_WIDTH), lambda b, i: (b * nts + i, 0)),
        out_shape=jax.ShapeDtypeStruct((batch * seq, POOL_WIDTH), BF16),
        compiler_params=_params("parallel", "arbitrary"),
        name="pool",
    )(za, za, w_pool, pool_scale)


def _even_weights(w_in):
    qw = SWA_HEADS * SWA_HEAD_DIM
    w_a = jnp.concatenate([w_in[:, :POOL_WIDTH], w_in[:, POOL_WIDTH + qw:]], axis=1).astype(BF16)
    return w_a, w_in[:, POOL_WIDTH:POOL_WIDTH + qw].astype(BF16)


def _even_mixer(x, h, g_post, w_a, w_q, w_out, layer, w_pool, pool_scale, sinks, tables, batch, seq):
    qw = SWA_HEADS * SWA_HEAD_DIM
    k_cols = (POOL_WIDTH, POOL_WIDTH + SWA_KV_HEADS * SWA_HEAD_DIM)
    za = _proj(h, w_a, tables, seq, k_cols)
    zq = _proj(h, w_q, tables, seq, (0, qw))
    o = _swa(zq, za, sinks, batch, seq)
    a = _pool(za, w_pool, pool_scale, batch, seq)
    return _outproj(a, o, w_out, layer, x, g_post)


S5_SLAB = LANES
S5_SLABS = S5_WIDTH // S5_SLAB
S5_SLAB_STATE = (S5_SLAB // S5_GROUP_CH) * S5_STATE
S5_TIME_TILE = 128
S5_SCAN_UNROLL = 8


def _gelu(y):
    return 0.5 * y * (1.0 + jnp.tanh(math.sqrt(2.0 / math.pi) * (y + 0.044715 * (y * y * y))))


def _s5_kernel(u_ref, b_ref, tab_ref, c_ref, d_ref, gw_ref, gb_ref, o_ref, carry_ref, x_ref, y_ref, *, ts):
    i = pl.program_id(0)
    ns = S5_SLAB_STATE
    nb = SUBLANES
    nl = ns // LANES

    @pl.when(i == 0)
    def _():
        carry_ref[...] = jnp.zeros_like(carry_ref)

    u_all = u_ref[...].reshape(nb * ts, S5_WIDTH)
    for j in range(S5_SLABS):
        ch = slice(j * S5_SLAB, (j + 1) * S5_SLAB)
        u_j = u_all[:, ch]
        bu = jnp.dot(u_j, b_ref[j], preferred_element_type=F32)
        for b in range(nb):
            for c in range(2 * nl):
                x_ref[c, pl.ds(b, ts, stride=nb), :] = bu[b * ts:(b + 1) * ts, c * LANES:(c + 1) * LANES]
        a_re, a_im = tab_ref[j, 0], tab_ref[j, 1]

        def step(t, carry):
            c_re, c_im = carry
            r = pl.multiple_of(t * nb, nb)
            v_re = jnp.concatenate([x_ref[c, pl.ds(r, nb), :] for c in range(nl)], axis=1)
            v_im = jnp.concatenate([x_ref[nl + c, pl.ds(r, nb), :] for c in range(nl)], axis=1)
            n_re = v_re + a_re * c_re - a_im * c_im
            n_im = v_im + a_re * c_im + a_im * c_re
            for c in range(nl):
                x_ref[c, pl.ds(r, nb), :] = n_re[:, c * LANES:(c + 1) * LANES]
                x_ref[nl + c, pl.ds(r, nb), :] = n_im[:, c * LANES:(c + 1) * LANES]
            return n_re, n_im

        c_re, c_im = lax.fori_loop(0, ts, step, (carry_ref[j, :, 0:ns], carry_ref[j, :, ns:2 * ns]),
                                   unroll=S5_SCAN_UNROLL)
        carry_ref[j, :, 0:ns] = c_re
        carry_ref[j, :, ns:2 * ns] = c_im
        states = jnp.concatenate(
            [jnp.concatenate([x_ref[c, pl.ds(b, ts, stride=nb), :] for c in range(2 * nl)], axis=1)
             for b in range(nb)], axis=0)
        y_j = jnp.dot(states.astype(BF16), c_ref[j], preferred_element_type=F32)
        y_ref[:, ch] = _gelu(y_j + d_ref[:, ch] * u_j.astype(F32))

    y = y_ref[...]
    gate = jnp.dot(y.astype(BF16), gw_ref[...], preferred_element_type=F32) + gb_ref[...]
    o_ref[...] = (y * jax.nn.sigmoid(gate)).astype(o_ref.dtype).reshape(nb, ts, S5_WIDTH)


def _s5(z, p, batch, seq):
    assert batch == SUBLANES, "the S5 scan lays the batch out on the 8 sublanes"
    ts = min(S5_TIME_TILE, seq)
    kern = functools.partial(_s5_kernel, ts=ts)
    full = lambda shape: pl.BlockSpec(shape, lambda i: (0,) * len(shape))
    return pl.pallas_call(
        kern,
        grid=(seq // ts,),
        in_specs=[
            pl.BlockSpec((batch, ts, S5_WIDTH), lambda i: (0, i, 0)),
            full((S5_SLABS, S5_SLAB, 2 * S5_SLAB_STATE)),
            full((S5_SLABS, 2, SUBLANES, S5_SLAB_STATE)),
            full((S5_SLABS, 2 * S5_SLAB_STATE, S5_SLAB)),
            full((1, S5_WIDTH)),
            full((S5_WIDTH, S5_WIDTH)),
            full((1, S5_WIDTH)),
        ],
        out_specs=pl.BlockSpec((batch, ts, S5_WIDTH), lambda i: (0, i, 0)),
        out_shape=jax.ShapeDtypeStruct((batch, seq, S5_WIDTH), BF16),
        scratch_shapes=[
            pltpu.VMEM((S5_SLABS, SUBLANES, 2 * S5_SLAB_STATE), F32),
            pltpu.VMEM((2 * S5_SLAB_STATE // LANES, batch * ts, LANES), F32),
            pltpu.VMEM((batch * ts, S5_WIDTH), F32),
        ],
        compiler_params=_params("arbitrary"),
        name="s5",
    )(z, p["s5_b"], p["s5_tab"], p["s5_c"], p["s5_d"], p["glu_w"], p["glu_b"])


def _s5_params(a_re, a_im, log_dt, b_re, b_im, c_re, c_im, d_skip, glu_w, glu_b):
    lam_re = a_re.astype(F32)
    lam_im = a_im.astype(F32)
    dt = jnp.exp(log_dt.astype(F32))[:, None]
    decay = jnp.exp(lam_re * dt)
    abar_re = decay * jnp.cos(lam_im * dt)
    abar_im = decay * jnp.sin(lam_im * dt)
    inv_mag = 1.0 / (lam_re * lam_re + lam_im * lam_im)
    num_re = abar_re - 1.0
    f_re = (num_re * lam_re + abar_im * lam_im) * inv_mag
    f_im = (abar_im * lam_re - num_re * lam_im) * inv_mag
    br = b_re.astype(F32)
    bi = b_im.astype(F32)
    bbar_re = f_re[..., None] * br - f_im[..., None] * bi
    bbar_im = f_re[..., None] * bi + f_im[..., None] * br
    gps = S5_SLAB // S5_GROUP_CH
    eye = jnp.eye(gps, dtype=F32)

    def b_slab(t):
        t = t.reshape(S5_SLABS, gps, S5_STATE, S5_GROUP_CH)
        t = jnp.einsum('jgnh,gk->jghkn', t, eye)
        return t.reshape(S5_SLABS, S5_SLAB, S5_SLAB_STATE)

    def c_slab(t):
        t = t.astype(F32).reshape(S5_SLABS, gps, S5_GROUP_CH, S5_STATE)
        t = jnp.einsum('jghn,gk->jgnkh', t, eye)
        return t.reshape(S5_SLABS, S5_SLAB_STATE, S5_SLAB)

    b_mat = jnp.concatenate([b_slab(bbar_re), b_slab(bbar_im)], axis=2).astype(BF16)
    c_mat = jnp.concatenate([c_slab(c_re), -c_slab(c_im)], axis=1).astype(BF16)

    tab = jnp.stack([abar_re.reshape(S5_SLABS, 1, S5_SLAB_STATE), abar_im.reshape(S5_SLABS, 1, S5_SLAB_STATE)], axis=1)
    tab = jnp.broadcast_to(tab, (S5_SLABS, 2, SUBLANES, S5_SLAB_STATE))
    return {
        "s5_b": b_mat, "s5_c": c_mat, "s5_tab": tab,
        "s5_d": d_skip.astype(F32).reshape(1, S5_WIDTH),
        "glu_w": glu_w.astype(BF16), "glu_b": glu_b.astype(F32).reshape(1, S5_WIDTH),
    }


NSA_CMP_SLOTS = 128
NSA_Q_TILE = 256
NSA_K_TILE = 256
LOG2_E = math.log2(math.e)
ODD_A_KC = S5_WIDTH // LANES
ODD_A_KS = ODD_A_KC + NSA_KV_HEADS
ODD_A_KW = ODD_A_KS + NSA_KV_HEADS
ODD_A_END = ODD_A_KW + NSA_KV_HEADS
ODD_C_VS = NSA_KV_HEADS
ODD_C_VW = 2 * NSA_KV_HEADS
ODD_C_GL = 3 * NSA_KV_HEADS


def _compress_kernel(x_ref, w1a_ref, w1b_ref, pos_ref, b1_ref, w2_ref, o_ref):
    half = NSA_CMP_STRIDE * NSA_HEAD_DIM
    xc = x_ref[...].astype(F32)
    xa = (xc + pos_ref[:, 0:half]).astype(BF16)
    xb = (xc + pos_ref[:, half:2 * half]).astype(BF16)
    p1 = jnp.dot(xa, w1a_ref[...], preferred_element_type=F32)
    p2 = jnp.dot(xb, w1b_ref[...], preferred_element_type=F32)
    n = p2.shape[0]
    rows = lax.broadcasted_iota(jnp.int32, p2.shape, 0)
    p2_next = jnp.where(rows < n - 1, pltpu.roll(p2, n - 1, 0), 0.0)
    h = _gelu(p1 + p2_next + b1_ref[...])
    o_ref[...] = jnp.dot(h.astype(BF16), w2_ref[...], preferred_element_type=F32).astype(o_ref.dtype)


def _compress(za, zc, p, batch, seq):
    nchunk = seq // NSA_CMP_STRIDE
    kc = za[:, :, ODD_A_KC * LANES:ODD_A_KS * LANES]
    vc = zc[:, :, 0:ODD_C_VS * LANES]
    xc = jnp.stack([kc, vc], axis=1).reshape(batch, 2, nchunk, NSA_CMP_STRIDE, NSA_KV_HEADS, NSA_HEAD_DIM)
    xc = xc.transpose(0, 1, 4, 2, 3, 5).reshape(batch, 2, NSA_KV_HEADS, nchunk, NSA_CMP_STRIDE * NSA_HEAD_DIM)
    half = NSA_CMP_STRIDE * NSA_HEAD_DIM
    kind = lambda shape: pl.BlockSpec((None,) + shape, lambda b, c, h: (c,) + (0,) * len(shape))
    return pl.pallas_call(
        _compress_kernel,
        grid=(batch, 2, NSA_KV_HEADS),
        in_specs=[
            pl.BlockSpec((None, None, None, nchunk, half), lambda b, c, h: (b, c, h, 0, 0)),
            kind((half, NSA_HEAD_DIM)), kind((half, NSA_HEAD_DIM)),
            kind((1, 2 * half)), kind((1, NSA_HEAD_DIM)), kind((NSA_HEAD_DIM, NSA_HEAD_DIM)),
        ],
        out_specs=pl.BlockSpec((None, None, None, nchunk, NSA_HEAD_DIM), lambda b, c, h: (b, c, h, 0, 0)),
        out_shape=jax.ShapeDtypeStruct((batch, 2, NSA_KV_HEADS, nchunk, NSA_HEAD_DIM), BF16),
        compiler_params=_params("parallel", "arbitrary", "arbitrary"),
        name="compress",
    )(xc, p["cmp_w1a"], p["cmp_w1b"], p["cmp_pos"], p["cmp_b1"], p["cmp_w2"])


def _nt_dot(a, b):
    return lax.dot_general(a, b, (((1,), (1,)), ((), ())), preferred_element_type=F32)


def _tn_dot(a, b):
    return lax.dot_general(a, b, (((0,), (0,)), ((), ())), preferred_element_type=F32)


def _nsa_kernel(q_ref, kc_ref, vc_ref, ks_ref, vs_ref, kw_ref, vw_ref, gl_ref, o_ref, acc_ref, accw_ref, *, ncmp, nslc):
    i = pl.program_id(2)
    tq, tk, grp, d = NSA_Q_TILE, NSA_K_TILE, NSA_GROUP, NSA_HEAD_DIM
    rows = grp * tq
    scale = d ** -0.5
    t0 = i * tq
    q = jnp.concatenate([q_ref[:, g * d:(g + 1) * d] for g in range(grp)], axis=0)
    t_q = t0 + lax.broadcasted_iota(jnp.int32, (1, tq), 1)
    t_all = jnp.concatenate([t_q] * grp, axis=1)

    s = _nt_dot(kc_ref[...], q) * scale
    cidx = lax.broadcasted_iota(jnp.int32, (ncmp, 1), 0)
    vis = (cidx * NSA_CMP_STRIDE + (NSA_CMP_BLOCK - 1)) <= t_all
    s = jnp.where(vis, s, NEG_BIG)
    mx = jnp.max(s, axis=0, keepdims=True)
    e = jnp.where(vis, jnp.exp(s - mx), 0.0)
    p_cmp = e / jnp.maximum(jnp.sum(e, axis=0, keepdims=True), 1.0)
    o_cmp = _tn_dot(vc_ref[...], p_cmp.astype(BF16))

    p_sum = p_cmp[:, 0:tq]
    for g in range(1, grp):
        p_sum = p_sum + p_cmp[:, g * tq:(g + 1) * tq]
    sj = lax.broadcasted_iota(jnp.int32, (nslc, ncmp), 0) * NSA_SLC_BLOCK
    ci = lax.broadcasted_iota(jnp.int32, (nslc, ncmp), 1) * NSA_CMP_STRIDE
    overlap = jnp.where((ci < sj + NSA_SLC_BLOCK) & (ci + NSA_CMP_BLOCK > sj), 1.0, 0.0).astype(BF16)
    p_hi = p_sum.astype(BF16)
    p_lo = (p_sum - p_hi.astype(F32)).astype(BF16)
    imp = (jnp.dot(overlap, p_hi, preferred_element_type=F32)
           + jnp.dot(overlap, p_lo, preferred_element_type=F32))

    blk = lax.broadcasted_iota(jnp.int32, (nslc, tq), 0)
    cur = t_q // NSA_SLC_BLOCK
    forced = (blk == 0) | (blk == cur) | (blk == cur - 1)
    score = jnp.where(blk > cur, -jnp.inf, jnp.where(forced, jnp.inf, imp))
    sel = jnp.zeros((nslc, tq), F32)
    for _ in range(NSA_TOP_N):
        best = jnp.max(score, axis=0, keepdims=True)
        pick = jnp.min(jnp.where(score == best, blk, nslc), axis=0, keepdims=True)
        hit = blk == pick
        sel = jnp.where(hit, 1.0, sel)
        score = jnp.where(hit, -jnp.inf, score)
    sel = sel.astype(BF16)

    krow = lax.broadcasted_iota(jnp.int32, (tk, 1), 0)
    bcol = lax.broadcasted_iota(jnp.int32, (tk, nslc), 1)

    def slc_mask(k0):
        kpos = k0 + krow
        expand = jnp.where((kpos // NSA_SLC_BLOCK) == bcol, 1.0, 0.0).astype(BF16)
        chosen = jnp.dot(expand, sel, preferred_element_type=F32) > 0.5
        return chosen & (kpos <= t_q)

    def win_mask(k0):
        kpos = k0 + krow
        return (kpos <= t_q) & (kpos > t_q - NSA_WINDOW)

    q_log2 = (q.astype(F32) * (scale * LOG2_E)).astype(BF16)

    def tile_scores(k_ref, mask_fn, k0):
        bias_q = jnp.where(mask_fn(k0), 0.0, NEG_BIG)
        bias = jnp.concatenate([bias_q] * grp, axis=1)
        return _nt_dot(k_ref[pl.ds(k0, tk), :], q_log2) + bias

    def tile_update(sc, carry, v_ref, acc, k0):
        m_old, l_old = carry
        m_new = jnp.maximum(m_old, jnp.max(sc, axis=0, keepdims=True))
        alpha = jnp.exp2(m_old - m_new)
        pr = jnp.exp2(sc - m_new)
        l_new = alpha * l_old + jnp.sum(pr, axis=0, keepdims=True)
        acc[...] = alpha * acc[...] + _tn_dot(v_ref[pl.ds(k0, tk), :], pr.astype(BF16))
        return m_new, l_new

    def slc_body(jt, carry):
        k0 = pl.multiple_of(jt * tk, tk)
        return tile_update(tile_scores(ks_ref, slc_mask, k0), carry, vs_ref, acc_ref, k0)

    def both_body(jt, carry):
        k0 = pl.multiple_of(jt * tk, tk)
        sc_s = tile_scores(ks_ref, slc_mask, k0)
        sc_w = tile_scores(kw_ref, win_mask, k0)
        return (tile_update(sc_s, carry[0], vs_ref, acc_ref, k0),
                tile_update(sc_w, carry[1], vw_ref, accw_ref, k0))

    acc_ref[...] = jnp.zeros_like(acc_ref)
    accw_ref[...] = jnp.zeros_like(accw_ref)
    init = (jnp.full((1, rows), NEG_BIG, F32), jnp.zeros((1, rows), F32))
    hi = (t0 + tq) // tk
    win_lo = jnp.maximum(t0 - NSA_WINDOW, 0) // tk
    state_s = lax.fori_loop(0, win_lo, slc_body, init)
    (_, l_s), (_, l_w) = lax.fori_loop(win_lo, hi, both_body, (state_s, init))
    o_slc = acc_ref[...] / l_s
    o_win = accw_ref[...] / l_w

    gates = jax.nn.sigmoid(gl_ref[...].astype(F32)).T
    for g in range(grp):
        cs = slice(g * tq, (g + 1) * tq)
        o_g = (gates[3 * g:3 * g + 1, :] * o_cmp[:, cs]
               + gates[3 * g + 1:3 * g + 2, :] * o_slc[:, cs]
               + gates[3 * g + 2:3 * g + 3, :] * o_win[:, cs])
        o_ref[:, g * d:(g + 1) * d] = o_g.T.astype(o_ref.dtype)


def _nsa(za, zq, zc, kvc, batch, seq):
    tq, d, grp = NSA_Q_TILE, NSA_HEAD_DIM, NSA_GROUP
    ntq = seq // tq
    ncmp = kvc.shape[3]
    kern = functools.partial(_nsa_kernel, ncmp=ncmp, nslc=seq // NSA_SLC_BLOCK)
    qw = grp * d

    def seq_spec(col0):
        return pl.BlockSpec((None, seq, d), lambda b, h, i: (b, 0, col0 + h))

    def cmp_spec(c):
        return pl.BlockSpec((None, None, None, ncmp, d), lambda b, h, i: (b, c, h, 0, 0))

    return pl.pallas_call(
        kern,
        grid=(batch, NSA_KV_HEADS, ntq),
        in_specs=[
            pl.BlockSpec((None, tq, qw), lambda b, h, i: (b, i, h)),
            cmp_spec(0), cmp_spec(1),
            seq_spec(ODD_A_KS), seq_spec(ODD_C_VS), seq_spec(ODD_A_KW), seq_spec(ODD_C_VW),
            pl.BlockSpec((None, tq, LANES), lambda b, h, i: (b, i, ODD_C_GL + h)),
        ],
        out_specs=pl.BlockSpec((tq, qw), lambda b, h, i: (b * ntq + i, h)),
        out_shape=jax.ShapeDtypeStruct((batch * seq, NSA_HEADS * d), BF16),
        scratch_shapes=[pltpu.VMEM((d, grp * tq), F32), pltpu.VMEM((d, grp * tq), F32)],
        compiler_params=_params("parallel", "parallel", "arbitrary"),
        name="nsa",
    )(zq, kvc, kvc, za, zc, za, zc, zc)


def _odd_params(w_in, a_re, a_im, log_dt, b_re, b_im, c_re, c_im, d_skip, glu_w, glu_b,
                cmp_pos, cmp_w1, cmp_b1, cmp_w2, seq):
    kvw = NSA_KV_WIDTH
    qw = NSA_HEADS * NSA_HEAD_DIM
    o_q = S5_WIDTH
    o_kv = o_q + qw
    parts = {name: w_in[:, o_kv + n * kvw:o_kv + (n + 1) * kvw]
             for n, name in enumerate(("kc", "vc", "ks", "vs", "kw", "vw"))}
    w_gl = w_in[:, o_kv + 6 * kvw:].reshape(D_MODEL, NSA_KV_HEADS, 3 * NSA_GROUP)
    w_gl = jnp.pad(w_gl, ((0, 0), (0, 0), (0, LANES - 3 * NSA_GROUP))).reshape(D_MODEL, NSA_KV_HEADS * LANES)
    w_a = jnp.concatenate([w_in[:, :o_q], parts["kc"], parts["ks"], parts["kw"]], axis=1).astype(BF16)
    w_q = w_in[:, o_q:o_kv].astype(BF16)
    w_c = jnp.concatenate([parts["vc"], parts["vs"], parts["vw"], w_gl], axis=1).astype(BF16)
    half = NSA_CMP_STRIDE * NSA_HEAD_DIM
    w1 = cmp_w1.astype(BF16).reshape(2, NSA_CMP_BLOCK * NSA_HEAD_DIM, NSA_HEAD_DIM)
    p = _s5_params(a_re, a_im, log_dt, b_re, b_im, c_re, c_im, d_skip, glu_w, glu_b)
    p.update({
        "w_a": w_a, "w_q": w_q, "w_c": w_c,
        "cmp_w1a": w1[:, :half], "cmp_w1b": w1[:, half:],
        "cmp_pos": cmp_pos.astype(F32).reshape(2, 1, NSA_CMP_BLOCK * NSA_HEAD_DIM),
        "cmp_b1": cmp_b1.astype(F32).reshape(2, 1, NSA_HEAD_DIM),
        "cmp_w2": cmp_w2.astype(BF16),
        "tables": _rope_tables(seq, NSA_HEAD_DIM),
    })
    return p


def _odd_mixer(x, h, g_post, p, w_out, layer, batch, seq):
    tables = p["tables"]
    za = _proj(h, p["w_a"], tables, seq, (ODD_A_KC * LANES, ODD_A_END * LANES))
    zq = _proj(h, p["w_q"], tables, seq, (0, NSA_HEADS * NSA_HEAD_DIM))
    zc = _proj(h, p["w_c"], tables, seq, (0, 0))
    s5_out = _s5(za, p, batch, seq)
    kvc = _compress(za, zc, p, batch, seq)
    o = _nsa(za, zq, zc, kvc, batch, seq)
    return _outproj(s5_out, o, w_out, layer, x, g_post)


def kernel(x, norm_gains, ffn1_w_gate, ffn1_w_up, ffn1_w_down, ffn2_w_gate, ffn2_w_up, ffn2_w_down, ev_w_in, ev_w_out, pool_w, pool_scale, swa_sinks, od_w_in, od_w_out, s5_a_re, s5_a_im, s5_log_dt, s5_b_re, s5_b_im, s5_c_re, s5_c_im, s5_d, s5_glu_w, s5_glu_b, nsa_cmp_pos, nsa_cmp_w1, nsa_cmp_b1, nsa_cmp_w2):
    batch, seq, _ = x.shape
    m = batch * seq
    depth = norm_gains.shape[0]
    xs = x.reshape(m, D_MODEL)
    even_tables = _rope_tables(seq, SWA_HEAD_DIM)
    ffn_w = [(_cast_bf16(wg, cols_out=D_FF_PAD, col_tile=FF_TILE), _cast_bf16(wu, cols_out=D_FF_PAD, col_tile=FF_TILE),
              _cast_bf16(wd, rows_out=D_FF_PAD))
             for wg, wu, wd in ((ffn1_w_gate, ffn1_w_up, ffn1_w_down), (ffn2_w_gate, ffn2_w_up, ffn2_w_down))]
    ev_out, od_out = _cast_bf16(ev_w_out), _cast_bf16(od_w_out)
    for layer in range(depth):
        g = norm_gains[layer].astype(F32).reshape(6, 1, D_MODEL)
        i = layer // 2
        xs, h = _ffn(xs, g[0], *ffn_w[0], g[1], layer, g_next=g[2])
        if layer % 2 == 0:
            xs = _even_mixer(xs, h, g[3], *_even_weights(ev_w_in[i]), ev_out, i,
                             pool_w[i].astype(BF16), pool_scale[i].astype(F32).reshape(1, POOL_WIDTH),
                             swa_sinks[i].astype(F32), even_tables, batch, seq)
        else:
            p = _odd_params(od_w_in[i], s5_a_re[i], s5_a_im[i], s5_log_dt[i], s5_b_re[i], s5_b_im[i],
                            s5_c_re[i], s5_c_im[i], s5_d[i], s5_glu_w[i], s5_glu_b[i], nsa_cmp_pos[i],
                            nsa_cmp_w1[i], nsa_cmp_b1[i], nsa_cmp_w2[i], seq)
            xs = _odd_mixer(xs, h, g[3], p, od_out, i, batch, seq)
        xs = _ffn(xs, g[4], *ffn_w[1], g[5], layer)
    return xs.reshape(batch, seq, D_MODEL)
```

```python
import functools
import math

import numpy as np
import jax
import jax.numpy as jnp
from jax import lax
from jax.experimental import pallas as pl
from jax.experimental.pallas import tpu as pltpu

F32 = jnp.float32
BF16 = jnp.bfloat16

D_MODEL = 4096
D_FF = 5504
NORM_EPS = 1e-6
ROPE_THETA = 500000.0
ROPE_FRACTION = 4
POOL_WINDOWS = (2, 4, 8, 16)
POOL_WIDTH = D_MODEL // 2
POOL_GROUP = POOL_WIDTH // len(POOL_WINDOWS)
POOL_HALO = 16
SWA_HEAD_DIM = 64
SWA_HEADS = 32
SWA_KV_HEADS = 4
SWA_GROUP = SWA_HEADS // SWA_KV_HEADS
SWA_WINDOW = 128
ATTN_BLOCK = 128
S5_WIDTH = D_MODEL // 4
S5_GROUP_CH = 16
S5_GROUPS = S5_WIDTH // S5_GROUP_CH
S5_STATE = 64
NSA_HEAD_DIM = 128
NSA_HEADS = 24
NSA_KV_HEADS = 6
NSA_GROUP = 4
NSA_CMP_BLOCK = 32
NSA_CMP_STRIDE = 16
NSA_SLC_BLOCK = 64
NSA_TOP_N = 8
NSA_WINDOW = 512
NSA_KV_WIDTH = NSA_KV_HEADS * NSA_HEAD_DIM

LANES = 128
SUBLANES = 8
MXU_DIM = 256
VMEM_LIMIT_BYTES = 56 * 1024 * 1024

ROW_TILE = 512
FF_TILE = 256
D_FF_PAD = 5632
PROJ_ROW_TILE = 256
OUT_ROW_TILE = 256
OUT_COL_CHUNK = 512
OUT_VMEM_LIMIT_BYTES = 60 * 1024 * 1024
FFN_NEXT_VMEM_LIMIT_BYTES = 60 * 1024 * 1024
NORM_ROWS = 32
NORM_UNROLL = 2
CAST_ROWS = 256
NEG_BIG = -1e30


def _params(*sem, vmem=VMEM_LIMIT_BYTES):
    return pltpu.CompilerParams(dimension_semantics=sem, vmem_limit_bytes=vmem)


def _rms_scaled(v, g):
    ms = jnp.mean(v * v, axis=-1, keepdims=True)
    return v * lax.rsqrt(ms + NORM_EPS) * g


def _norm_chunk(c):
    return pl.ds(pl.multiple_of(c * NORM_ROWS, NORM_ROWS), NORM_ROWS)


def _norm_rows(dst_ref, src_ref, g_ref, rows):
    g = g_ref[...]

    def body(c, carry):
        rs = _norm_chunk(c)
        dst_ref[rs, :] = _rms_scaled(src_ref[rs, :].astype(F32), g).astype(dst_ref.dtype)
        return carry

    lax.fori_loop(0, rows // NORM_ROWS, body, 0, unroll=NORM_UNROLL)


def _norm_residual_rows(acc_ref, g_ref, res_ref, res_scale, inv_ref, rows, next_ref=None, gnext_ref=None):
    g = g_ref[...]
    g_next = None if next_ref is None else gnext_ref[...]
    reps = acc_ref.shape[1] // LANES

    def stats(c, carry):
        rs = _norm_chunk(c)
        v = acc_ref[rs, :]
        ms = jnp.mean(v * v, axis=-1, keepdims=True)
        inv_ref[rs, :] = jnp.broadcast_to(lax.rsqrt(ms + NORM_EPS), (NORM_ROWS, LANES))
        return carry

    def scale(c, carry):
        rs = _norm_chunk(c)
        y = res_ref[rs, :] + res_scale * (acc_ref[rs, :] * jnp.tile(inv_ref[rs, :], (1, reps)) * g)
        acc_ref[rs, :] = y
        if next_ref is not None:
            next_ref[rs, :] = _rms_scaled(y, g_next).astype(next_ref.dtype)
        return carry

    lax.fori_loop(0, rows // NORM_ROWS, stats, 0, unroll=NORM_UNROLL)
    lax.fori_loop(0, rows // NORM_ROWS, scale, 0, unroll=NORM_UNROLL)


def _cast_kernel(x_ref, o_ref, *, tr, rows_in, cols_in, cols_out, col_tile):
    x = x_ref[...]
    if rows_in % tr:
        r = pl.program_id(1) * tr + lax.broadcasted_iota(jnp.int32, (tr, 1), 0)
        x = jnp.where(r < rows_in, x, 0.0)
    y = x.astype(o_ref.dtype)
    if cols_out > cols_in:
        y = jnp.concatenate([y, jnp.zeros((tr, cols_out - cols_in), o_ref.dtype)], axis=1)
    if col_tile is None:
        o_ref[...] = y
    else:
        for j in range(cols_out // col_tile):
            o_ref[j] = y[:, j * col_tile:(j + 1) * col_tile]


def _cast_bf16(w, rows_out=None, cols_out=None, col_tile=None):
    nl, rows_in, cols_in = w.shape
    rows_out = rows_out or rows_in
    cols_out = cols_out or cols_in
    tr = min(CAST_ROWS, rows_out)
    kern = functools.partial(_cast_kernel, tr=tr, rows_in=rows_in, cols_in=cols_in, cols_out=cols_out,
                             col_tile=col_tile)
    if col_tile is None:
        out_spec = pl.BlockSpec((None, tr, cols_out), lambda l, i: (l, i, 0))
        out_shape = (nl, rows_out, cols_out)
    else:
        nt = cols_out // col_tile
        out_spec = pl.BlockSpec((None, nt, tr, col_tile), lambda l, i: (l, 0, i, 0))
        out_shape = (nl, nt, rows_out, col_tile)
    return pl.pallas_call(
        kern,
        grid=(nl, rows_out // tr),
        in_specs=[pl.BlockSpec((None, tr, cols_in), lambda l, i: (l, i, 0))],
        out_specs=out_spec,
        out_shape=jax.ShapeDtypeStruct(out_shape, BF16),
        compiler_params=_params("parallel", "parallel"),
        name="cast",
    )(w)


def _ffn_kernel(x_ref, gpre_ref, wg_ref, wu_ref, wd_ref, gpost_ref, *rest, tm, nj):
    gnext_ref = rest[0] if len(rest) == 4 else None
    o_ref, h_ref, inv_ref = rest[-3:]
    j = pl.program_id(1)

    @pl.when(j == 0)
    def _():
        _norm_rows(h_ref, x_ref, gpre_ref, tm)
        o_ref[...] = jnp.zeros_like(o_ref)

    h = h_ref[...]
    gate = jnp.dot(h, wg_ref[...], preferred_element_type=F32)
    up = jnp.dot(h, wu_ref[...], preferred_element_type=F32)
    act = (gate * jax.nn.sigmoid(gate) * up).astype(BF16)
    nc = 512
    for n in range(D_MODEL // nc):
        sl = slice(n * nc, (n + 1) * nc)
        o_ref[:, sl] += jnp.dot(act, wd_ref[:, sl], preferred_element_type=F32)

    @pl.when(j == nj - 1)
    def _():
        _norm_residual_rows(o_ref, gpost_ref, x_ref, 0.5, inv_ref, tm,
                            next_ref=None if gnext_ref is None else h_ref, gnext_ref=gnext_ref)


def _ffn(x, g_pre, wg, wu, wd, g_post, layer, g_next=None):
    m = x.shape[0]
    tm = min(ROW_TILE, m)
    nj = D_FF_PAD // FF_TILE
    kern = functools.partial(_ffn_kernel, tm=tm, nj=nj)
    row_spec = pl.BlockSpec((tm, D_MODEL), lambda i, j: (i, 0))
    gain_spec = pl.BlockSpec((1, D_MODEL), lambda i, j: (0, 0))
    in_specs = [
        row_spec, gain_spec,
        pl.BlockSpec((None, None, D_MODEL, FF_TILE), lambda i, j: (layer, j, 0, 0)),
        pl.BlockSpec((None, None, D_MODEL, FF_TILE), lambda i, j: (layer, j, 0, 0)),
        pl.BlockSpec((None, FF_TILE, D_MODEL), lambda i, j: (layer, j, 0)),
        gain_spec,
    ]
    x_shape = jax.ShapeDtypeStruct((m, D_MODEL), F32)
    inv_scratch = pltpu.VMEM((tm, LANES), F32)
    if g_next is None:
        return pl.pallas_call(
            kern, grid=(m // tm, nj), in_specs=in_specs, out_specs=row_spec, out_shape=x_shape,
            scratch_shapes=[pltpu.VMEM((tm, D_MODEL), BF16), inv_scratch],
            compiler_params=_params("parallel", "arbitrary"), name="ffn",
        )(x, g_pre, wg, wu, wd, g_post)
    return pl.pallas_call(
        kern, grid=(m // tm, nj), in_specs=in_specs + [gain_spec], out_specs=(row_spec, row_spec),
        out_shape=(x_shape, jax.ShapeDtypeStruct((m, D_MODEL), BF16)), scratch_shapes=[inv_scratch],
        compiler_params=_params("parallel", "arbitrary", vmem=FFN_NEXT_VMEM_LIMIT_BYTES), name="ffn_next",
    )(x, g_pre, wg, wu, wd, g_post, g_next)


def _proj_kernel(h_ref, w_ref, cos_ref, sa_ref, sb_ref, o_ref, *, rope_chunks, half):
    h = h_ref[...]
    tc = MXU_DIM
    for c in range(o_ref.shape[1] // tc):
        cs = slice(c * tc, (c + 1) * tc)
        z = jnp.dot(h, w_ref[:, cs], preferred_element_type=F32)
        if c in rope_chunks:
            parts = []
            for p in range(tc // LANES):
                zp = z[:, p * LANES:(p + 1) * LANES]
                parts.append(zp * cos_ref[...]
                             + pltpu.roll(zp, LANES - half, 1) * sa_ref[...]
                             + pltpu.roll(zp, half, 1) * sb_ref[...])
            z = jnp.concatenate(parts, axis=1)
        o_ref[:, cs] = z.astype(o_ref.dtype)


def _rope_tables(seq, head_dim):
    rot = head_dim // ROPE_FRACTION
    half = rot // 2
    inv_freq = jnp.power(ROPE_THETA, -jnp.arange(half, dtype=F32) * 2.0 / rot)
    ang = jnp.arange(seq, dtype=jnp.int32).astype(F32)[:, None] * inv_freq[None, :]
    cos, sin = jnp.cos(ang), jnp.sin(ang)
    one = jnp.ones((seq, head_dim - rot), F32)
    zero_h = jnp.zeros((seq, half), F32)
    zero_r = jnp.zeros((seq, head_dim - rot), F32)
    c_head = jnp.concatenate([cos, cos, one], axis=1)
    sa_head = jnp.concatenate([-sin, zero_h, zero_r], axis=1)
    sb_head = jnp.concatenate([zero_h, sin, zero_r], axis=1)
    reps = LANES // head_dim
    return jnp.tile(c_head, (1, reps)), jnp.tile(sa_head, (1, reps)), jnp.tile(sb_head, (1, reps)), half


def _proj(h, w, tables, seq, rope_cols):
    cos_t, sa_t, sb_t, half = tables
    m = h.shape[0]
    gw = w.shape[1]
    tm = min(PROJ_ROW_TILE, seq)
    sblocks = seq // tm
    rope_chunks = frozenset(range(rope_cols[0] // MXU_DIM, rope_cols[1] // MXU_DIM))
    kern = functools.partial(_proj_kernel, rope_chunks=rope_chunks, half=half)
    tab_spec = pl.BlockSpec((tm, LANES), lambda i: (i % sblocks, 0))
    return pl.pallas_call(
        kern,
        grid=(m // tm,),
        in_specs=[
            pl.BlockSpec((tm, D_MODEL), lambda i: (i, 0)),
            pl.BlockSpec((D_MODEL, gw), lambda i: (0, 0), pipeline_mode=pl.Buffered(1)),
            tab_spec, tab_spec, tab_spec,
        ],
        out_specs=pl.BlockSpec((None, tm, gw), lambda i: (i // sblocks, i % sblocks, 0)),
        out_shape=jax.ShapeDtypeStruct((m // seq, seq, gw), BF16),
        compiler_params=_params("parallel"),
        name="proj",
    )(h, w, cos_t, sa_t, sb_t)


def _outproj_kernel(a1_ref, a2_ref, w_ref, g_ref, x_ref, o_ref, inv_ref, *, tm, k1):
    a1 = a1_ref[...]
    a2 = a2_ref[...]
    nc = OUT_COL_CHUNK
    for n in range(D_MODEL // nc):
        sl = slice(n * nc, (n + 1) * nc)
        o_ref[:, sl] = (jnp.dot(a1, w_ref[0:k1, sl], preferred_element_type=F32)
                        + jnp.dot(a2, w_ref[k1:, sl], preferred_element_type=F32))
    _norm_residual_rows(o_ref, g_ref, x_ref, 1.0, inv_ref, tm)


def _outproj(a1, a2, w, layer, x, g_post):
    m = x.shape[0]
    k1, k2 = a1.shape[-1], a2.shape[-1]
    if a1.ndim == 3:
        tm = min(OUT_ROW_TILE, a1.shape[1])
        sb = a1.shape[1] // tm
        a1_spec = pl.BlockSpec((None, tm, k1), lambda i: (i // sb, i % sb, 0))
    else:
        tm = min(OUT_ROW_TILE, m)
        a1_spec = pl.BlockSpec((tm, k1), lambda i: (i, 0))
    kern = functools.partial(_outproj_kernel, tm=tm, k1=k1)
    return pl.pallas_call(
        kern,
        grid=(m // tm,),
        in_specs=[
            a1_spec,
            pl.BlockSpec((tm, k2), lambda i: (i, 0)),
            pl.BlockSpec((None, k1 + k2, D_MODEL), lambda i: (layer, 0, 0), pipeline_mode=pl.Buffered(1)),
            pl.BlockSpec((1, D_MODEL), lambda i: (0, 0)),
            pl.BlockSpec((tm, D_MODEL), lambda i: (i, 0)),
        ],
        out_specs=pl.BlockSpec((tm, D_MODEL), lambda i: (i, 0)),
        out_shape=jax.ShapeDtypeStruct((m, D_MODEL), F32),
        scratch_shapes=[pltpu.VMEM((tm, LANES), F32)],
        compiler_params=_params("parallel", vmem=OUT_VMEM_LIMIT_BYTES),
        name="outproj",
    )(a1, a2, w, g_post, x)


def _swa_kernel(sink_ref, q_ref, kvp_ref, kvc_ref, o_ref):
    n = pl.program_id(1)
    blk = ATTN_BLOCK
    kvw = SWA_KV_HEADS * SWA_HEAD_DIM
    kv = jnp.concatenate([kvp_ref[...], kvc_ref[...]], axis=0)
    krow = lax.broadcasted_iota(jnp.int32, (2 * blk, 1), 0)
    qcol = lax.broadcasted_iota(jnp.int32, (1, blk), 1)
    diff = qcol - krow + blk
    vis = (diff >= 0) & (diff < SWA_WINDOW) & ((krow >= blk) | (n > 0))
    bias = jnp.where(vis, 0.0, NEG_BIG)
    scale = SWA_HEAD_DIM ** -0.5 * LOG2_E
    pair = LANES // SWA_HEAD_DIM
    for kh in range(SWA_KV_HEADS):
        k_h = kv[:, kh * SWA_HEAD_DIM:(kh + 1) * SWA_HEAD_DIM]
        v_h = kv[:, kvw + kh * SWA_HEAD_DIM:kvw + (kh + 1) * SWA_HEAD_DIM]
        heads = range(kh * SWA_GROUP, (kh + 1) * SWA_GROUP)
        scores = [_nt_dot(k_h, q_ref[:, h * SWA_HEAD_DIM:(h + 1) * SWA_HEAD_DIM]) for h in heads]
        weights, denoms = [], []
        for h, s in zip(heads, scores):
            s = s * scale + bias
            sk = sink_ref[h] * LOG2_E
            mx = jnp.maximum(jnp.max(s, axis=0, keepdims=True), sk)
            e = jnp.exp2(s - mx)
            denoms.append(jnp.sum(e, axis=0, keepdims=True) + jnp.exp2(sk - mx))
            weights.append(e.astype(BF16))
        outs = [_tn_dot(v_h, e) / d for e, d in zip(weights, denoms)]
        for g0 in range(0, SWA_GROUP, pair):
            h0 = kh * SWA_GROUP + g0
            o_ref[:, h0 * SWA_HEAD_DIM:(h0 + pair) * SWA_HEAD_DIM] = (
                jnp.concatenate(outs[g0:g0 + pair], axis=0).T.astype(o_ref.dtype))


def _swa(zq, za, sinks, batch, seq):
    blk = ATTN_BLOCK
    nblk = seq // blk
    qw = SWA_HEADS * SWA_HEAD_DIM
    kvw2 = 2 * SWA_KV_HEADS * SWA_HEAD_DIM
    kv_blk = POOL_WIDTH // kvw2
    return pl.pallas_call(
        _swa_kernel,
        grid=(batch, nblk),
        in_specs=[
            pl.BlockSpec(memory_space=pltpu.SMEM),
            pl.BlockSpec((None, blk, qw), lambda b, n: (b, n, 0)),
            pl.BlockSpec((None, blk, kvw2), lambda b, n: (b, jnp.maximum(n - 1, 0), kv_blk)),
            pl.BlockSpec((None, blk, kvw2), lambda b, n: (b, n, kv_blk)),
        ],
        out_specs=pl.BlockSpec((blk, qw), lambda b, n: (b * nblk + n, 0)),
        out_shape=jax.ShapeDtypeStruct((batch * seq, qw), BF16),
        compiler_params=_params("parallel", "arbitrary"),
        name="swa",
    )(sinks, zq, za, za)


def _pool_kernel(up_ref, uc_ref, w_ref, scale_ref, o_ref, *, ts):
    i = pl.program_id(1)
    has_prev = (i > 0).astype(F32)
    pos = i * ts + lax.broadcasted_iota(jnp.int32, (ts, 1), 0)
    for gi, win in enumerate(POOL_WINDOWS):
        sl = slice(gi * POOL_GROUP, (gi + 1) * POOL_GROUP)
        cur = uc_ref[:, sl].astype(F32)
        prev = up_ref[:, sl].astype(F32) * has_prev
        acc = jnp.concatenate([prev, cur], axis=0)
        d = 1
        while d < win:
            acc = acc + pltpu.roll(acc, d, 0)
            d *= 2
        wsum = acc[POOL_HALO:, :]
        count = jnp.minimum(pos + 1, win).astype(F32)
        zz = (wsum / count - cur).astype(BF16)
        a = jnp.dot(zz, w_ref[gi], preferred_element_type=F32) * scale_ref[:, sl]
        o_ref[:, sl] = a.astype(o_ref.dtype)


def _pool(za, w_pool, pool_scale, batch, seq):
    ts = min(256, seq)
    nts = seq // ts
    halo_blocks = ts // POOL_HALO
    kern = functools.partial(_pool_kernel, ts=ts)
    return pl.pallas_call(
        kern,
        grid=(batch, nts),
        in_specs=[
            pl.BlockSpec((None, POOL_HALO, POOL_WIDTH), lambda b, i: (b, jnp.maximum(i * halo_blocks - 1, 0), 0)),
            pl.BlockSpec((None, ts, POOL_WIDTH), lambda b, i: (b, i, 0)),
            pl.BlockSpec((len(POOL_WINDOWS), POOL_GROUP, POOL_GROUP), lambda b, i: (0, 0, 0)),
            pl.BlockSpec((1, POOL_WIDTH), lambda b, i: (0, 0)),
        ],
        out_specs=pl.BlockSpec((ts, POOL_WIDTH), lambda b, i: (b * nts + i, 0)),
        out_shape=jax.ShapeDtypeStruct((batch * seq, POOL_WIDTH), BF16),
        compiler_params=_params("parallel", "arbitrary"),
        name="pool",
    )(za, za, w_pool, pool_scale)


def _even_weights(w_in):
    qw = SWA_HEADS * SWA_HEAD_DIM
    w_a = jnp.concatenate([w_in[:, :POOL_WIDTH], w_in[:, POOL_WIDTH + qw:]], axis=1)
    return w_a, w_in[:, POOL_WIDTH:POOL_WIDTH + qw]


def _even_mixer(x, h, g_post, w_a, w_q, w_out, layer, w_pool, pool_scale, sinks, tables, batch, seq):
    qw = SWA_HEADS * SWA_HEAD_DIM
    k_cols = (POOL_WIDTH, POOL_WIDTH + SWA_KV_HEADS * SWA_HEAD_DIM)
    za = _proj(h, w_a, tables, seq, k_cols)
    zq = _proj(h, w_q, tables, seq, (0, qw))
    o = _swa(zq, za, sinks, batch, seq)
    a = _pool(za, w_pool, pool_scale, batch, seq)
    return _outproj(a, o, w_out, layer, x, g_post)


S5_SLAB = LANES
S5_SLABS = S5_WIDTH // S5_SLAB
S5_SLAB_STATE = (S5_SLAB // S5_GROUP_CH) * S5_STATE
S5_TIME_TILE = 128
S5_SCAN_UNROLL = 8


def _gelu(y):
    return 0.5 * y * (1.0 + jnp.tanh(math.sqrt(2.0 / math.pi) * (y + 0.044715 * (y * y * y))))


def _s5_kernel(u_ref, b_ref, tab_ref, c_ref, d_ref, gw_ref, gb_ref, o_ref, carry_ref, x_ref, y_ref, *, ts):
    i = pl.program_id(0)
    ns = S5_SLAB_STATE
    nb = SUBLANES
    nl = ns // LANES

    @pl.when(i == 0)
    def _():
        carry_ref[...] = jnp.zeros_like(carry_ref)

    u_all = u_ref[...].reshape(nb * ts, S5_WIDTH)
    for j in range(S5_SLABS):
        ch = slice(j * S5_SLAB, (j + 1) * S5_SLAB)
        u_j = u_all[:, ch]
        bu = jnp.dot(u_j, b_ref[j], preferred_element_type=F32)
        for b in range(nb):
            for c in range(2 * nl):
                x_ref[c, pl.ds(b, ts, stride=nb), :] = bu[b * ts:(b + 1) * ts, c * LANES:(c + 1) * LANES]
        a_re, a_im = tab_ref[j, 0], tab_ref[j, 1]

        def step(t, carry):
            c_re, c_im = carry
            r = pl.multiple_of(t * nb, nb)
            v_re = jnp.concatenate([x_ref[c, pl.ds(r, nb), :] for c in range(nl)], axis=1)
            v_im = jnp.concatenate([x_ref[nl + c, pl.ds(r, nb), :] for c in range(nl)], axis=1)
            n_re = v_re + a_re * c_re - a_im * c_im
            n_im = v_im + a_re * c_im + a_im * c_re
            for c in range(nl):
                x_ref[c, pl.ds(r, nb), :] = n_re[:, c * LANES:(c + 1) * LANES]
                x_ref[nl + c, pl.ds(r, nb), :] = n_im[:, c * LANES:(c + 1) * LANES]
            return n_re, n_im

        c_re, c_im = lax.fori_loop(0, ts, step, (carry_ref[j, :, 0:ns], carry_ref[j, :, ns:2 * ns]),
                                   unroll=S5_SCAN_UNROLL)
        carry_ref[j, :, 0:ns] = c_re
        carry_ref[j, :, ns:2 * ns] = c_im
        states = jnp.concatenate(
            [jnp.concatenate([x_ref[c, pl.ds(b, ts, stride=nb), :] for c in range(2 * nl)], axis=1)
             for b in range(nb)], axis=0)
        y_j = jnp.dot(states.astype(BF16), c_ref[j], preferred_element_type=F32)
        y_ref[:, ch] = _gelu(y_j + d_ref[:, ch] * u_j.astype(F32))

    y = y_ref[...]
    gate = jnp.dot(y.astype(BF16), gw_ref[...], preferred_element_type=F32) + gb_ref[...]
    o_ref[...] = (y * jax.nn.sigmoid(gate)).astype(o_ref.dtype).reshape(nb, ts, S5_WIDTH)


def _s5(z, p, batch, seq):
    assert batch == SUBLANES, "the S5 scan lays the batch out on the 8 sublanes"
    ts = min(S5_TIME_TILE, seq)
    kern = functools.partial(_s5_kernel, ts=ts)
    full = lambda shape: pl.BlockSpec(shape, lambda i: (0,) * len(shape))
    return pl.pallas_call(
        kern,
        grid=(seq // ts,),
        in_specs=[
            pl.BlockSpec((batch, ts, S5_WIDTH), lambda i: (0, i, 0)),
            full((S5_SLABS, S5_SLAB, 2 * S5_SLAB_STATE)),
            full((S5_SLABS, 2, SUBLANES, S5_SLAB_STATE)),
            full((S5_SLABS, 2 * S5_SLAB_STATE, S5_SLAB)),
            full((1, S5_WIDTH)),
            full((S5_WIDTH, S5_WIDTH)),
            full((1, S5_WIDTH)),
        ],
        out_specs=pl.BlockSpec((batch, ts, S5_WIDTH), lambda i: (0, i, 0)),
        out_shape=jax.ShapeDtypeStruct((batch, seq, S5_WIDTH), BF16),
        scratch_shapes=[
            pltpu.VMEM((S5_SLABS, SUBLANES, 2 * S5_SLAB_STATE), F32),
            pltpu.VMEM((2 * S5_SLAB_STATE // LANES, batch * ts, LANES), F32),
            pltpu.VMEM((batch * ts, S5_WIDTH), F32),
        ],
        compiler_params=_params("arbitrary"),
        name="s5",
    )(z, p["s5_b"], p["s5_tab"], p["s5_c"], p["s5_d"], p["glu_w"], p["glu_b"])


def _s5_params(a_re, a_im, log_dt, b_re, b_im, c_re, c_im, d_skip, glu_w, glu_b):
    lam_re = a_re.astype(F32)
    lam_im = a_im.astype(F32)
    dt = jnp.exp(log_dt.astype(F32))[:, None]
    decay = jnp.exp(lam_re * dt)
    abar_re = decay * jnp.cos(lam_im * dt)
    abar_im = decay * jnp.sin(lam_im * dt)
    inv_mag = 1.0 / (lam_re * lam_re + lam_im * lam_im)
    num_re = abar_re - 1.0
    f_re = (num_re * lam_re + abar_im * lam_im) * inv_mag
    f_im = (abar_im * lam_re - num_re * lam_im) * inv_mag
    br = b_re.astype(F32)
    bi = b_im.astype(F32)
    bbar_re = f_re[..., None] * br - f_im[..., None] * bi
    bbar_im = f_re[..., None] * bi + f_im[..., None] * br
    gps = S5_SLAB // S5_GROUP_CH
    eye = jnp.eye(gps, dtype=F32)

    def b_slab(t):
        t = t.reshape(S5_SLABS, gps, S5_STATE, S5_GROUP_CH)
        t = jnp.einsum('jgnh,gk->jghkn', t, eye)
        return t.reshape(S5_SLABS, S5_SLAB, S5_SLAB_STATE)

    def c_slab(t):
        t = t.astype(F32).reshape(S5_SLABS, gps, S5_GROUP_CH, S5_STATE)
        t = jnp.einsum('jghn,gk->jgnkh', t, eye)
        return t.reshape(S5_SLABS, S5_SLAB_STATE, S5_SLAB)

    b_mat = jnp.concatenate([b_slab(bbar_re), b_slab(bbar_im)], axis=2).astype(BF16)
    c_mat = jnp.concatenate([c_slab(c_re), -c_slab(c_im)], axis=1).astype(BF16)

    tab = jnp.stack([abar_re.reshape(S5_SLABS, 1, S5_SLAB_STATE), abar_im.reshape(S5_SLABS, 1, S5_SLAB_STATE)], axis=1)
    tab = jnp.broadcast_to(tab, (S5_SLABS, 2, SUBLANES, S5_SLAB_STATE))
    return {
        "s5_b": b_mat, "s5_c": c_mat, "s5_tab": tab,
        "s5_d": d_skip.astype(F32).reshape(1, S5_WIDTH),
        "glu_w": glu_w.astype(BF16), "glu_b": glu_b.astype(F32).reshape(1, S5_WIDTH),
    }


NSA_CMP_SLOTS = 128
NSA_Q_TILE = 256
NSA_K_TILE = 256
LOG2_E = math.log2(math.e)
ODD_A_KC = S5_WIDTH // LANES
ODD_A_KS = ODD_A_KC + NSA_KV_HEADS
ODD_A_KW = ODD_A_KS + NSA_KV_HEADS
ODD_A_END = ODD_A_KW + NSA_KV_HEADS
ODD_C_VS = NSA_KV_HEADS
ODD_C_VW = 2 * NSA_KV_HEADS
ODD_C_GL = 3 * NSA_KV_HEADS


def _compress_kernel(k_ref, v_ref, w1_ref, pos_ref, b1_ref, w2_ref, o_ref, xs_ref):
    n = o_ref.shape[1]
    st = NSA_CMP_STRIDE
    rows = lax.broadcasted_iota(jnp.int32, (n, NSA_HEAD_DIM), 0)
    for kind, src_ref in enumerate((k_ref, v_ref)):
        xs_ref[...] = src_ref[...].astype(F32)
        p1 = jnp.zeros((n, NSA_HEAD_DIM), F32)
        p2 = jnp.zeros((n, NSA_HEAD_DIM), F32)
        for l in range(st):
            xl = xs_ref[pl.ds(l, n, stride=st), :]
            p1 = p1 + jnp.dot((xl + pos_ref[kind, l:l + 1, :]).astype(BF16), w1_ref[kind, l],
                              preferred_element_type=F32)
            p2 = p2 + jnp.dot((xl + pos_ref[kind, st + l:st + l + 1, :]).astype(BF16), w1_ref[kind, st + l],
                              preferred_element_type=F32)
        p2_next = jnp.where(rows < n - 1, pltpu.roll(p2, n - 1, 0), 0.0)
        hid = _gelu(p1 + p2_next + b1_ref[kind])
        o_ref[kind] = jnp.dot(hid.astype(BF16), w2_ref[kind], preferred_element_type=F32).astype(o_ref.dtype)


def _compress(za, zc, p, batch, seq):
    nchunk = seq // NSA_CMP_STRIDE
    d = NSA_HEAD_DIM
    full = lambda shape: pl.BlockSpec(shape, lambda b, h: (0,) * len(shape))
    return pl.pallas_call(
        _compress_kernel,
        grid=(batch, NSA_KV_HEADS),
        in_specs=[
            pl.BlockSpec((None, seq, d), lambda b, h: (b, 0, ODD_A_KC + h)),
            pl.BlockSpec((None, seq, d), lambda b, h: (b, 0, h)),
            full((2, NSA_CMP_BLOCK, d, d)), full((2, NSA_CMP_BLOCK, d)), full((2, 1, d)), full((2, d, d)),
        ],
        out_specs=pl.BlockSpec((None, 2, None, nchunk, d), lambda b, h: (b, 0, h, 0, 0)),
        out_shape=jax.ShapeDtypeStruct((batch, 2, NSA_KV_HEADS, nchunk, d), BF16),
        scratch_shapes=[pltpu.VMEM((seq, d), F32)],
        compiler_params=_params("parallel", "arbitrary"),
        name="compress",
    )(za, zc, p["cmp_w1"], p["cmp_pos"], p["cmp_b1"], p["cmp_w2"])


def _nt_dot(a, b):
    return lax.dot_general(a, b, (((1,), (1,)), ((), ())), preferred_element_type=F32)


def _tn_dot(a, b):
    return lax.dot_general(a, b, (((0,), (0,)), ((), ())), preferred_element_type=F32)


def _nsa_kernel(q_ref, kc_ref, vc_ref, ks_ref, vs_ref, kw_ref, vw_ref, gl_ref, o_ref, acc_ref, accw_ref, *, ncmp, nslc):
    i = pl.program_id(2)
    tq, tk, grp, d = NSA_Q_TILE, NSA_K_TILE, NSA_GROUP, NSA_HEAD_DIM
    rows = grp * tq
    scale = d ** -0.5
    t0 = i * tq
    q = jnp.concatenate([q_ref[:, g * d:(g + 1) * d] for g in range(grp)], axis=0)
    t_q = t0 + lax.broadcasted_iota(jnp.int32, (1, tq), 1)
    t_all = jnp.concatenate([t_q] * grp, axis=1)

    s = _nt_dot(kc_ref[...], q) * scale
    cidx = lax.broadcasted_iota(jnp.int32, (ncmp, 1), 0)
    vis = (cidx * NSA_CMP_STRIDE + (NSA_CMP_BLOCK - 1)) <= t_all
    s = jnp.where(vis, s, NEG_BIG)
    mx = jnp.max(s, axis=0, keepdims=True)
    e = jnp.where(vis, jnp.exp(s - mx), 0.0)
    p_cmp = e / jnp.maximum(jnp.sum(e, axis=0, keepdims=True), 1.0)
    o_cmp = _tn_dot(vc_ref[...], p_cmp.astype(BF16))

    p_sum = p_cmp[:, 0:tq]
    for g in range(1, grp):
        p_sum = p_sum + p_cmp[:, g * tq:(g + 1) * tq]
    sj = lax.broadcasted_iota(jnp.int32, (nslc, ncmp), 0) * NSA_SLC_BLOCK
    ci = lax.broadcasted_iota(jnp.int32, (nslc, ncmp), 1) * NSA_CMP_STRIDE
    overlap = jnp.where((ci < sj + NSA_SLC_BLOCK) & (ci + NSA_CMP_BLOCK > sj), 1.0, 0.0).astype(BF16)
    p_hi = p_sum.astype(BF16)
    p_lo = (p_sum - p_hi.astype(F32)).astype(BF16)
    imp = (jnp.dot(overlap, p_hi, preferred_element_type=F32)
           + jnp.dot(overlap, p_lo, preferred_element_type=F32))

    blk = lax.broadcasted_iota(jnp.int32, (nslc, tq), 0)
    cur = t_q // NSA_SLC_BLOCK
    forced = (blk == 0) | (blk == cur) | (blk == cur - 1)
    score = jnp.where(blk > cur, -jnp.inf, jnp.where(forced, jnp.inf, imp))
    sel = jnp.zeros((nslc, tq), F32)
    for _ in range(NSA_TOP_N):
        best = jnp.max(score, axis=0, keepdims=True)
        pick = jnp.min(jnp.where(score == best, blk, nslc), axis=0, keepdims=True)
        hit = blk == pick
        sel = jnp.where(hit, 1.0, sel)
        score = jnp.where(hit, -jnp.inf, score)
    sel = sel.astype(BF16)

    krow = lax.broadcasted_iota(jnp.int32, (tk, 1), 0)
    bcol = lax.broadcasted_iota(jnp.int32, (tk, nslc), 1)

    def slc_mask(k0):
        kpos = k0 + krow
        expand = jnp.where((kpos // NSA_SLC_BLOCK) == bcol, 1.0, 0.0).astype(BF16)
        chosen = jnp.dot(expand, sel, preferred_element_type=F32) > 0.5
        return chosen & (kpos <= t_q)

    def win_mask(k0):
        kpos = k0 + krow
        return (kpos <= t_q) & (kpos > t_q - NSA_WINDOW)

    q_log2 = (q.astype(F32) * (scale * LOG2_E)).astype(BF16)

    def tile_scores(k_ref, mask_fn, k0):
        bias_q = jnp.where(mask_fn(k0), 0.0, NEG_BIG)
        bias = jnp.concatenate([bias_q] * grp, axis=1)
        return _nt_dot(k_ref[pl.ds(k0, tk), :], q_log2) + bias

    def tile_update(sc, carry, v_ref, acc, k0):
        m_old, l_old = carry
        m_new = jnp.maximum(m_old, jnp.max(sc, axis=0, keepdims=True))
        alpha = jnp.exp2(m_old - m_new)
        pr = jnp.exp2(sc - m_new)
        l_new = alpha * l_old + jnp.sum(pr, axis=0, keepdims=True)
        acc[...] = alpha * acc[...] + _tn_dot(v_ref[pl.ds(k0, tk), :], pr.astype(BF16))
        return m_new, l_new

    def slc_body(jt, carry):
        k0 = pl.multiple_of(jt * tk, tk)
        return tile_update(tile_scores(ks_ref, slc_mask, k0), carry, vs_ref, acc_ref, k0)

    def both_body(jt, carry):
        k0 = pl.multiple_of(jt * tk, tk)
        sc_s = tile_scores(ks_ref, slc_mask, k0)
        sc_w = tile_scores(kw_ref, win_mask, k0)
        return (tile_update(sc_s, carry[0], vs_ref, acc_ref, k0),
                tile_update(sc_w, carry[1], vw_ref, accw_ref, k0))

    acc_ref[...] = jnp.zeros_like(acc_ref)
    accw_ref[...] = jnp.zeros_like(accw_ref)
    init = (jnp.full((1, rows), NEG_BIG, F32), jnp.zeros((1, rows), F32))
    hi = (t0 + tq) // tk
    win_lo = jnp.maximum(t0 - NSA_WINDOW, 0) // tk
    state_s = lax.fori_loop(0, win_lo, slc_body, init)
    (_, l_s), (_, l_w) = lax.fori_loop(win_lo, hi, both_body, (state_s, init))
    o_slc = acc_ref[...] / l_s
    o_win = accw_ref[...] / l_w

    gates = jax.nn.sigmoid(gl_ref[...].astype(F32)).T
    for g in range(grp):
        cs = slice(g * tq, (g + 1) * tq)
        o_g = (gates[3 * g:3 * g + 1, :] * o_cmp[:, cs]
               + gates[3 * g + 1:3 * g + 2, :] * o_slc[:, cs]
               + gates[3 * g + 2:3 * g + 3, :] * o_win[:, cs])
        o_ref[:, g * d:(g + 1) * d] = o_g.T.astype(o_ref.dtype)


def _nsa(za, zq, zc, kvc, batch, seq):
    tq, d, grp = NSA_Q_TILE, NSA_HEAD_DIM, NSA_GROUP
    ntq = seq // tq
    ncmp = kvc.shape[3]
    kern = functools.partial(_nsa_kernel, ncmp=ncmp, nslc=seq // NSA_SLC_BLOCK)
    qw = grp * d

    def seq_spec(col0):
        return pl.BlockSpec((None, seq, d), lambda b, h, i: (b, 0, col0 + h))

    def cmp_spec(c):
        return pl.BlockSpec((None, None, None, ncmp, d), lambda b, h, i: (b, c, h, 0, 0))

    return pl.pallas_call(
        kern,
        grid=(batch, NSA_KV_HEADS, ntq),
        in_specs=[
            pl.BlockSpec((None, tq, qw), lambda b, h, i: (b, i, h)),
            cmp_spec(0), cmp_spec(1),
            seq_spec(ODD_A_KS), seq_spec(ODD_C_VS), seq_spec(ODD_A_KW), seq_spec(ODD_C_VW),
            pl.BlockSpec((None, tq, LANES), lambda b, h, i: (b, i, ODD_C_GL + h)),
        ],
        out_specs=pl.BlockSpec((tq, qw), lambda b, h, i: (b * ntq + i, h)),
        out_shape=jax.ShapeDtypeStruct((batch * seq, NSA_HEADS * d), BF16),
        scratch_shapes=[pltpu.VMEM((d, grp * tq), F32), pltpu.VMEM((d, grp * tq), F32)],
        compiler_params=_params("parallel", "parallel", "arbitrary"),
        name="nsa",
    )(zq, kvc, kvc, za, zc, za, zc, zc)


def _odd_params(w_in, a_re, a_im, log_dt, b_re, b_im, c_re, c_im, d_skip, glu_w, glu_b,
                cmp_pos, cmp_w1, cmp_b1, cmp_w2, seq):
    kvw = NSA_KV_WIDTH
    qw = NSA_HEADS * NSA_HEAD_DIM
    o_q = S5_WIDTH
    o_kv = o_q + qw
    parts = {name: w_in[:, o_kv + n * kvw:o_kv + (n + 1) * kvw]
             for n, name in enumerate(("kc", "vc", "ks", "vs", "kw", "vw"))}
    w_gl = w_in[:, o_kv + 6 * kvw:].reshape(D_MODEL, NSA_KV_HEADS, 3 * NSA_GROUP)
    w_gl = jnp.pad(w_gl, ((0, 0), (0, 0), (0, LANES - 3 * NSA_GROUP))).reshape(D_MODEL, NSA_KV_HEADS * LANES)
    w_a = jnp.concatenate([w_in[:, :o_q], parts["kc"], parts["ks"], parts["kw"]], axis=1)
    w_q = w_in[:, o_q:o_kv]
    w_c = jnp.concatenate([parts["vc"], parts["vs"], parts["vw"], w_gl], axis=1)
    p = _s5_params(a_re, a_im, log_dt, b_re, b_im, c_re, c_im, d_skip, glu_w, glu_b)
    p.update({
        "w_a": w_a, "w_q": w_q, "w_c": w_c,
        "cmp_w1": cmp_w1.astype(BF16),
        "cmp_pos": cmp_pos.astype(F32),
        "cmp_b1": cmp_b1.astype(F32).reshape(2, 1, NSA_HEAD_DIM),
        "cmp_w2": cmp_w2.astype(BF16),
        "tables": _rope_tables(seq, NSA_HEAD_DIM),
    })
    return p


def _odd_mixer(x, h, g_post, p, w_out, layer, batch, seq):
    tables = p["tables"]
    za = _proj(h, p["w_a"], tables, seq, (ODD_A_KC * LANES, ODD_A_END * LANES))
    zq = _proj(h, p["w_q"], tables, seq, (0, NSA_HEADS * NSA_HEAD_DIM))
    zc = _proj(h, p["w_c"], tables, seq, (0, 0))
    s5_out = _s5(za, p, batch, seq)
    kvc = _compress(za, zc, p, batch, seq)
    o = _nsa(za, zq, zc, kvc, batch, seq)
    return _outproj(s5_out, o, w_out, layer, x, g_post)


def kernel(x, norm_gains, ffn1_w_gate, ffn1_w_up, ffn1_w_down, ffn2_w_gate, ffn2_w_up, ffn2_w_down, ev_w_in, ev_w_out, pool_w, pool_scale, swa_sinks, od_w_in, od_w_out, s5_a_re, s5_a_im, s5_log_dt, s5_b_re, s5_b_im, s5_c_re, s5_c_im, s5_d, s5_glu_w, s5_glu_b, nsa_cmp_pos, nsa_cmp_w1, nsa_cmp_b1, nsa_cmp_w2):
    batch, seq, _ = x.shape
    m = batch * seq
    depth = norm_gains.shape[0]
    xs = x.reshape(m, D_MODEL)
    even_tables = _rope_tables(seq, SWA_HEAD_DIM)
    ffn_w = [(_cast_bf16(wg, cols_out=D_FF_PAD, col_tile=FF_TILE), _cast_bf16(wu, cols_out=D_FF_PAD, col_tile=FF_TILE),
              _cast_bf16(wd, rows_out=D_FF_PAD))
             for wg, wu, wd in ((ffn1_w_gate, ffn1_w_up, ffn1_w_down), (ffn2_w_gate, ffn2_w_up, ffn2_w_down))]
    ev_out, od_out = _cast_bf16(ev_w_out), _cast_bf16(od_w_out)
    ev_in, od_in = _cast_bf16(ev_w_in), _cast_bf16(od_w_in)
    for layer in range(depth):
        g = norm_gains[layer].astype(F32).reshape(6, 1, D_MODEL)
        i = layer // 2
        xs, h = _ffn(xs, g[0], *ffn_w[0], g[1], layer, g_next=g[2])
        if layer % 2 == 0:
            xs = _even_mixer(xs, h, g[3], *_even_weights(ev_in[i]), ev_out, i,
                             pool_w[i].astype(BF16), pool_scale[i].astype(F32).reshape(1, POOL_WIDTH),
                             swa_sinks[i].astype(F32), even_tables, batch, seq)
        else:
            p = _odd_params(od_in[i], s5_a_re[i], s5_a_im[i], s5_log_dt[i], s5_b_re[i], s5_b_im[i],
                            s5_c_re[i], s5_c_im[i], s5_d[i], s5_glu_w[i], s5_glu_b[i], nsa_cmp_pos[i],
                            nsa_cmp_w1[i], nsa_cmp_b1[i], nsa_cmp_w2[i], seq)
            xs = _odd_mixer(xs, h, g[3], p, od_out, i, batch, seq)
        xs = _ffn(xs, g[4], *ffn_w[1], g[5], layer)
    return xs.reshape(batch, seq, D_MODEL)
```

```python
import functools
import math

import numpy as np
import jax
import jax.numpy as jnp
from jax import lax
from jax.experimental import pallas as pl
from jax.experimental.pallas import tpu as pltpu

F32 = jnp.float32
BF16 = jnp.bfloat16

D_MODEL = 4096
D_FF = 5504
NORM_EPS = 1e-6
ROPE_THETA = 500000.0
ROPE_FRACTION = 4
POOL_WINDOWS = (2, 4, 8, 16)
POOL_WIDTH = D_MODEL // 2
POOL_GROUP = POOL_WIDTH // len(POOL_WINDOWS)
POOL_HALO = 16
SWA_HEAD_DIM = 64
SWA_HEADS = 32
SWA_KV_HEADS = 4
SWA_GROUP = SWA_HEADS // SWA_KV_HEADS
SWA_WINDOW = 128
ATTN_BLOCK = 128
S5_WIDTH = D_MODEL // 4
S5_GROUP_CH = 16
S5_GROUPS = S5_WIDTH // S5_GROUP_CH
S5_STATE = 64
NSA_HEAD_DIM = 128
NSA_HEADS = 24
NSA_KV_HEADS = 6
NSA_GROUP = 4
NSA_CMP_BLOCK = 32
NSA_CMP_STRIDE = 16
NSA_SLC_BLOCK = 64
NSA_TOP_N = 8
NSA_WINDOW = 512
NSA_KV_WIDTH = NSA_KV_HEADS * NSA_HEAD_DIM

LANES = 128
SUBLANES = 8
MXU_DIM = 256
VMEM_LIMIT_BYTES = 56 * 1024 * 1024

ROW_TILE = 512
FF_TILE = 256
D_FF_PAD = 5632
PROJ_ROW_TILE = 256
OUT_ROW_TILE = 256
OUT_COL_CHUNK = 512
OUT_VMEM_LIMIT_BYTES = 60 * 1024 * 1024
FFN_NEXT_VMEM_LIMIT_BYTES = 60 * 1024 * 1024
NORM_ROWS = 32
NORM_UNROLL = 2
CAST_ROWS = 256
NEG_BIG = -1e30


def _params(*sem, vmem=VMEM_LIMIT_BYTES):
    return pltpu.CompilerParams(dimension_semantics=sem, vmem_limit_bytes=vmem)


def _rms_scaled(v, g):
    ms = jnp.mean(v * v, axis=-1, keepdims=True)
    return v * lax.rsqrt(ms + NORM_EPS) * g


def _norm_chunk(c):
    return pl.ds(pl.multiple_of(c * NORM_ROWS, NORM_ROWS), NORM_ROWS)


def _norm_rows(dst_ref, src_ref, g_ref, rows):
    g = g_ref[...]

    def body(c, carry):
        rs = _norm_chunk(c)
        dst_ref[rs, :] = _rms_scaled(src_ref[rs, :].astype(F32), g).astype(dst_ref.dtype)
        return carry

    lax.fori_loop(0, rows // NORM_ROWS, body, 0, unroll=NORM_UNROLL)


def _norm_residual_rows(acc_ref, g_ref, res_ref, res_scale, inv_ref, rows, next_ref=None, gnext_ref=None):
    g = g_ref[...]
    g_next = None if next_ref is None else gnext_ref[...]
    reps = acc_ref.shape[1] // LANES

    def stats(c, carry):
        rs = _norm_chunk(c)
        v = acc_ref[rs, :]
        ms = jnp.mean(v * v, axis=-1, keepdims=True)
        inv_ref[rs, :] = jnp.broadcast_to(lax.rsqrt(ms + NORM_EPS), (NORM_ROWS, LANES))
        return carry

    def scale(c, carry):
        rs = _norm_chunk(c)
        y = res_ref[rs, :] + res_scale * (acc_ref[rs, :] * jnp.tile(inv_ref[rs, :], (1, reps)) * g)
        acc_ref[rs, :] = y
        if next_ref is not None:
            next_ref[rs, :] = _rms_scaled(y, g_next).astype(next_ref.dtype)
        return carry

    lax.fori_loop(0, rows // NORM_ROWS, stats, 0, unroll=NORM_UNROLL)
    lax.fori_loop(0, rows // NORM_ROWS, scale, 0, unroll=NORM_UNROLL)


def _cast_kernel(x_ref, o_ref, *, tr, rows_in, cols_in, cols_out, col_tile):
    x = x_ref[...]
    if rows_in % tr:
        r = pl.program_id(1) * tr + lax.broadcasted_iota(jnp.int32, (tr, 1), 0)
        x = jnp.where(r < rows_in, x, 0.0)
    y = x.astype(o_ref.dtype)
    if cols_out > cols_in:
        y = jnp.concatenate([y, jnp.zeros((tr, cols_out - cols_in), o_ref.dtype)], axis=1)
    if col_tile is None:
        o_ref[...] = y
    else:
        for j in range(cols_out // col_tile):
            o_ref[j] = y[:, j * col_tile:(j + 1) * col_tile]


def _cast_bf16(w, rows_out=None, cols_out=None, col_tile=None):
    nl, rows_in, cols_in = w.shape
    rows_out = rows_out or rows_in
    cols_out = cols_out or cols_in
    tr = min(CAST_ROWS, rows_out)
    kern = functools.partial(_cast_kernel, tr=tr, rows_in=rows_in, cols_in=cols_in, cols_out=cols_out,
                             col_tile=col_tile)
    if col_tile is None:
        out_spec = pl.BlockSpec((None, tr, cols_out), lambda l, i: (l, i, 0))
        out_shape = (nl, rows_out, cols_out)
    else:
        nt = cols_out // col_tile
        out_spec = pl.BlockSpec((None, nt, tr, col_tile), lambda l, i: (l, 0, i, 0))
        out_shape = (nl, nt, rows_out, col_tile)
    return pl.pallas_call(
        kern,
        grid=(nl, rows_out // tr),
        in_specs=[pl.BlockSpec((None, tr, cols_in), lambda l, i: (l, i, 0))],
        out_specs=out_spec,
        out_shape=jax.ShapeDtypeStruct(out_shape, BF16),
        compiler_params=_params("parallel", "parallel"),
        name="cast",
    )(w)


def _ffn_kernel(x_ref, gpre_ref, wg_ref, wu_ref, wd_ref, gpost_ref, *rest, tm, nj):
    gnext_ref = rest[0] if len(rest) == 4 else None
    o_ref, h_ref, inv_ref = rest[-3:]
    j = pl.program_id(1)

    @pl.when(j == 0)
    def _():
        _norm_rows(h_ref, x_ref, gpre_ref, tm)
        o_ref[...] = jnp.zeros_like(o_ref)

    h = h_ref[...]
    gate = jnp.dot(h, wg_ref[...], preferred_element_type=F32)
    up = jnp.dot(h, wu_ref[...], preferred_element_type=F32)
    act = (gate * jax.nn.sigmoid(gate) * up).astype(BF16)
    nc = 512
    for n in range(D_MODEL // nc):
        sl = slice(n * nc, (n + 1) * nc)
        o_ref[:, sl] += jnp.dot(act, wd_ref[:, sl], preferred_element_type=F32)

    @pl.when(j == nj - 1)
    def _():
        _norm_residual_rows(o_ref, gpost_ref, x_ref, 0.5, inv_ref, tm,
                            next_ref=None if gnext_ref is None else h_ref, gnext_ref=gnext_ref)


def _ffn(x, g_pre, wg, wu, wd, g_post, layer, g_next=None):
    m = x.shape[0]
    tm = min(ROW_TILE, m)
    nj = D_FF_PAD // FF_TILE
    kern = functools.partial(_ffn_kernel, tm=tm, nj=nj)
    row_spec = pl.BlockSpec((tm, D_MODEL), lambda i, j: (i, 0))
    gain_spec = pl.BlockSpec((1, D_MODEL), lambda i, j: (0, 0))
    in_specs = [
        row_spec, gain_spec,
        pl.BlockSpec((None, None, D_MODEL, FF_TILE), lambda i, j: (layer, j, 0, 0)),
        pl.BlockSpec((None, None, D_MODEL, FF_TILE), lambda i, j: (layer, j, 0, 0)),
        pl.BlockSpec((None, FF_TILE, D_MODEL), lambda i, j: (layer, j, 0)),
        gain_spec,
    ]
    x_shape = jax.ShapeDtypeStruct((m, D_MODEL), F32)
    inv_scratch = pltpu.VMEM((tm, LANES), F32)
    if g_next is None:
        return pl.pallas_call(
            kern, grid=(m // tm, nj), in_specs=in_specs, out_specs=row_spec, out_shape=x_shape,
            scratch_shapes=[pltpu.VMEM((tm, D_MODEL), BF16), inv_scratch],
            compiler_params=_params("parallel", "arbitrary"), name="ffn",
        )(x, g_pre, wg, wu, wd, g_post)
    return pl.pallas_call(
        kern, grid=(m // tm, nj), in_specs=in_specs + [gain_spec], out_specs=(row_spec, row_spec),
        out_shape=(x_shape, jax.ShapeDtypeStruct((m, D_MODEL), BF16)), scratch_shapes=[inv_scratch],
        compiler_params=_params("parallel", "arbitrary", vmem=FFN_NEXT_VMEM_LIMIT_BYTES), name="ffn_next",
    )(x, g_pre, wg, wu, wd, g_post, g_next)


def _proj_kernel(h_ref, w_ref, cos_ref, sa_ref, sb_ref, o_ref, *, rope_chunks, half):
    h = h_ref[...]
    tc = MXU_DIM
    for c in range(o_ref.shape[1] // tc):
        cs = slice(c * tc, (c + 1) * tc)
        z = jnp.dot(h, w_ref[:, cs], preferred_element_type=F32)
        if c in rope_chunks:
            parts = []
            for p in range(tc // LANES):
                zp = z[:, p * LANES:(p + 1) * LANES]
                parts.append(zp * cos_ref[...]
                             + pltpu.roll(zp, LANES - half, 1) * sa_ref[...]
                             + pltpu.roll(zp, half, 1) * sb_ref[...])
            z = jnp.concatenate(parts, axis=1)
        o_ref[:, cs] = z.astype(o_ref.dtype)


def _rope_tables(seq, head_dim):
    rot = head_dim // ROPE_FRACTION
    half = rot // 2
    inv_freq = jnp.power(ROPE_THETA, -jnp.arange(half, dtype=F32) * 2.0 / rot)
    ang = jnp.arange(seq, dtype=jnp.int32).astype(F32)[:, None] * inv_freq[None, :]
    cos, sin = jnp.cos(ang), jnp.sin(ang)
    one = jnp.ones((seq, head_dim - rot), F32)
    zero_h = jnp.zeros((seq, half), F32)
    zero_r = jnp.zeros((seq, head_dim - rot), F32)
    c_head = jnp.concatenate([cos, cos, one], axis=1)
    sa_head = jnp.concatenate([-sin, zero_h, zero_r], axis=1)
    sb_head = jnp.concatenate([zero_h, sin, zero_r], axis=1)
    reps = LANES // head_dim
    return jnp.tile(c_head, (1, reps)), jnp.tile(sa_head, (1, reps)), jnp.tile(sb_head, (1, reps)), half


def _proj(h, w, tables, seq, rope_cols):
    cos_t, sa_t, sb_t, half = tables
    m = h.shape[0]
    gw = w.shape[1]
    tm = min(PROJ_ROW_TILE, seq)
    sblocks = seq // tm
    rope_chunks = frozenset(range(rope_cols[0] // MXU_DIM, rope_cols[1] // MXU_DIM))
    kern = functools.partial(_proj_kernel, rope_chunks=rope_chunks, half=half)
    tab_spec = pl.BlockSpec((tm, LANES), lambda i: (i % sblocks, 0))
    return pl.pallas_call(
        kern,
        grid=(m // tm,),
        in_specs=[
            pl.BlockSpec((tm, D_MODEL), lambda i: (i, 0)),
            pl.BlockSpec((D_MODEL, gw), lambda i: (0, 0), pipeline_mode=pl.Buffered(1)),
            tab_spec, tab_spec, tab_spec,
        ],
        out_specs=pl.BlockSpec((None, tm, gw), lambda i: (i // sblocks, i % sblocks, 0)),
        out_shape=jax.ShapeDtypeStruct((m // seq, seq, gw), BF16),
        compiler_params=_params("parallel"),
        name="proj",
    )(h, w, cos_t, sa_t, sb_t)


def _outproj_kernel(a1_ref, a2_ref, w_ref, g_ref, x_ref, o_ref, inv_ref, *, tm, k1):
    a1 = a1_ref[...]
    a2 = a2_ref[...]
    nc = OUT_COL_CHUNK
    for n in range(D_MODEL // nc):
        sl = slice(n * nc, (n + 1) * nc)
        o_ref[:, sl] = (jnp.dot(a1, w_ref[0:k1, sl], preferred_element_type=F32)
                        + jnp.dot(a2, w_ref[k1:, sl], preferred_element_type=F32))
    _norm_residual_rows(o_ref, g_ref, x_ref, 1.0, inv_ref, tm)


def _outproj(a1, a2, w, layer, x, g_post):
    m = x.shape[0]
    k1, k2 = a1.shape[-1], a2.shape[-1]
    if a1.ndim == 3:
        tm = min(OUT_ROW_TILE, a1.shape[1])
        sb = a1.shape[1] // tm
        a1_spec = pl.BlockSpec((None, tm, k1), lambda i: (i // sb, i % sb, 0))
    else:
        tm = min(OUT_ROW_TILE, m)
        a1_spec = pl.BlockSpec((tm, k1), lambda i: (i, 0))
    kern = functools.partial(_outproj_kernel, tm=tm, k1=k1)
    return pl.pallas_call(
        kern,
        grid=(m // tm,),
        in_specs=[
            a1_spec,
            pl.BlockSpec((tm, k2), lambda i: (i, 0)),
            pl.BlockSpec((None, k1 + k2, D_MODEL), lambda i: (layer, 0, 0), pipeline_mode=pl.Buffered(1)),
            pl.BlockSpec((1, D_MODEL), lambda i: (0, 0)),
            pl.BlockSpec((tm, D_MODEL), lambda i: (i, 0)),
        ],
        out_specs=pl.BlockSpec((tm, D_MODEL), lambda i: (i, 0)),
        out_shape=jax.ShapeDtypeStruct((m, D_MODEL), F32),
        scratch_shapes=[pltpu.VMEM((tm, LANES), F32)],
        compiler_params=_params("parallel", vmem=OUT_VMEM_LIMIT_BYTES),
        name="outproj",
    )(a1, a2, w, g_post, x)


def _swa_kernel(sink_ref, q_ref, kvp_ref, kvc_ref, o_ref):
    n = pl.program_id(1)
    blk = ATTN_BLOCK
    kvw = SWA_KV_HEADS * SWA_HEAD_DIM
    kv = jnp.concatenate([kvp_ref[...], kvc_ref[...]], axis=0)
    krow = lax.broadcasted_iota(jnp.int32, (2 * blk, 1), 0)
    qcol = lax.broadcasted_iota(jnp.int32, (1, blk), 1)
    diff = qcol - krow + blk
    vis = (diff >= 0) & (diff < SWA_WINDOW) & ((krow >= blk) | (n > 0))
    bias = jnp.where(vis, 0.0, NEG_BIG)
    scale = SWA_HEAD_DIM ** -0.5 * LOG2_E
    pair = LANES // SWA_HEAD_DIM
    for kh in range(SWA_KV_HEADS):
        k_h = kv[:, kh * SWA_HEAD_DIM:(kh + 1) * SWA_HEAD_DIM]
        v_h = kv[:, kvw + kh * SWA_HEAD_DIM:kvw + (kh + 1) * SWA_HEAD_DIM]
        heads = range(kh * SWA_GROUP, (kh + 1) * SWA_GROUP)
        scores = [_nt_dot(k_h, q_ref[:, h * SWA_HEAD_DIM:(h + 1) * SWA_HEAD_DIM]) for h in heads]
        weights, denoms = [], []
        for h, s in zip(heads, scores):
            s = s * scale + bias
            sk = sink_ref[h] * LOG2_E
            mx = jnp.maximum(jnp.max(s, axis=0, keepdims=True), sk)
            e = jnp.exp2(s - mx)
            denoms.append(jnp.sum(e, axis=0, keepdims=True) + jnp.exp2(sk - mx))
            weights.append(e.astype(BF16))
        outs = [_tn_dot(v_h, e) / d for e, d in zip(weights, denoms)]
        for g0 in range(0, SWA_GROUP, pair):
            h0 = kh * SWA_GROUP + g0
            o_ref[:, h0 * SWA_HEAD_DIM:(h0 + pair) * SWA_HEAD_DIM] = (
                jnp.concatenate(outs[g0:g0 + pair], axis=0).T.astype(o_ref.dtype))


def _swa(zq, za, sinks, batch, seq):
    blk = ATTN_BLOCK
    nblk = seq // blk
    qw = SWA_HEADS * SWA_HEAD_DIM
    kvw2 = 2 * SWA_KV_HEADS * SWA_HEAD_DIM
    kv_blk = POOL_WIDTH // kvw2
    return pl.pallas_call(
        _swa_kernel,
        grid=(batch, nblk),
        in_specs=[
            pl.BlockSpec(memory_space=pltpu.SMEM),
            pl.BlockSpec((None, blk, qw), lambda b, n: (b, n, 0)),
            pl.BlockSpec((None, blk, kvw2), lambda b, n: (b, jnp.maximum(n - 1, 0), kv_blk)),
            pl.BlockSpec((None, blk, kvw2), lambda b, n: (b, n, kv_blk)),
        ],
        out_specs=pl.BlockSpec((blk, qw), lambda b, n: (b * nblk + n, 0)),
        out_shape=jax.ShapeDtypeStruct((batch * seq, qw), BF16),
        compiler_params=_params("parallel", "arbitrary"),
        name="swa",
    )(sinks, zq, za, za)


def _pool_kernel(up_ref, uc_ref, w_ref, scale_ref, o_ref, *, ts):
    i = pl.program_id(1)
    has_prev = (i > 0).astype(F32)
    pos = i * ts + lax.broadcasted_iota(jnp.int32, (ts, 1), 0)
    for gi, win in enumerate(POOL_WINDOWS):
        sl = slice(gi * POOL_GROUP, (gi + 1) * POOL_GROUP)
        cur = uc_ref[:, sl].astype(F32)
        prev = up_ref[:, sl].astype(F32) * has_prev
        acc = jnp.concatenate([prev, cur], axis=0)
        d = 1
        while d < win:
            acc = acc + pltpu.roll(acc, d, 0)
            d *= 2
        wsum = acc[POOL_HALO:, :]
        count = jnp.minimum(pos + 1, win).astype(F32)
        zz = (wsum / count - cur).astype(BF16)
        a = jnp.dot(zz, w_ref[gi], preferred_element_type=F32) * scale_ref[:, sl]
        o_ref[:, sl] = a.astype(o_ref.dtype)


def _pool(za, w_pool, pool_scale, batch, seq):
    ts = min(256, seq)
    nts = seq // ts
    halo_blocks = ts // POOL_HALO
    kern = functools.partial(_pool_kernel, ts=ts)
    return pl.pallas_call(
        kern,
        grid=(batch, nts),
        in_specs=[
            pl.BlockSpec((None, POOL_HALO, POOL_WIDTH), lambda b, i: (b, jnp.maximum(i * halo_blocks - 1, 0), 0)),
            pl.BlockSpec((None, ts, POOL_WIDTH), lambda b, i: (b, i, 0)),
            pl.BlockSpec((len(POOL_WINDOWS), POOL_GROUP, POOL_GROUP), lambda b, i: (0, 0, 0)),
            pl.BlockSpec((1, POOL_WIDTH), lambda b, i: (0, 0)),
        ],
        out_specs=pl.BlockSpec((ts, POOL_WIDTH), lambda b, i: (b * nts + i, 0)),
        out_shape=jax.ShapeDtypeStruct((batch * seq, POOL_WIDTH), BF16),
        compiler_params=_params("parallel", "arbitrary"),
        name="pool",
    )(za, za, w_pool, pool_scale)


def _even_weights(w_in):
    qw = SWA_HEADS * SWA_HEAD_DIM
    w_a = jnp.concatenate([w_in[:, :POOL_WIDTH], w_in[:, POOL_WIDTH + qw:]], axis=1).astype(BF16)
    return w_a, w_in[:, POOL_WIDTH:POOL_WIDTH + qw].astype(BF16)


def _even_mixer(x, h, g_post, w_a, w_q, w_out, layer, w_pool, pool_scale, sinks, tables, batch, seq):
    qw = SWA_HEADS * SWA_HEAD_DIM
    k_cols = (POOL_WIDTH, POOL_WIDTH + SWA_KV_HEADS * SWA_HEAD_DIM)
    za = _proj(h, w_a, tables, seq, k_cols)
    zq = _proj(h, w_q, tables, seq, (0, qw))
    o = _swa(zq, za, sinks, batch, seq)
    a = _pool(za, w_pool, pool_scale, batch, seq)
    return _outproj(a, o, w_out, layer, x, g_post)


S5_SLAB = LANES
S5_SLABS = S5_WIDTH // S5_SLAB
S5_SLAB_STATE = (S5_SLAB // S5_GROUP_CH) * S5_STATE
S5_TIME_TILE = 128
S5_SCAN_UNROLL = 8


def _gelu(y):
    return 0.5 * y * (1.0 + jnp.tanh(math.sqrt(2.0 / math.pi) * (y + 0.044715 * (y * y * y))))


def _s5_kernel(u_ref, b_ref, tab_ref, c_ref, d_ref, gw_ref, gb_ref, o_ref, carry_ref, x_ref, y_ref, *, ts):
    i = pl.program_id(0)
    ns = S5_SLAB_STATE
    nb = SUBLANES
    nl = ns // LANES

    @pl.when(i == 0)
    def _():
        carry_ref[...] = jnp.zeros_like(carry_ref)

    u_all = u_ref[...].reshape(nb * ts, S5_WIDTH)
    for j in range(S5_SLABS):
        ch = slice(j * S5_SLAB, (j + 1) * S5_SLAB)
        u_j = u_all[:, ch]
        bu = jnp.dot(u_j, b_ref[j], preferred_element_type=F32)
        for b in range(nb):
            for c in range(2 * nl):
                x_ref[c, pl.ds(b, ts, stride=nb), :] = bu[b * ts:(b + 1) * ts, c * LANES:(c + 1) * LANES]
        a_re, a_im = tab_ref[j, 0], tab_ref[j, 1]

        def step(t, carry):
            c_re, c_im = carry
            r = pl.multiple_of(t * nb, nb)
            v_re = jnp.concatenate([x_ref[c, pl.ds(r, nb), :] for c in range(nl)], axis=1)
            v_im = jnp.concatenate([x_ref[nl + c, pl.ds(r, nb), :] for c in range(nl)], axis=1)
            n_re = v_re + a_re * c_re - a_im * c_im
            n_im = v_im + a_re * c_im + a_im * c_re
            for c in range(nl):
                x_ref[c, pl.ds(r, nb), :] = n_re[:, c * LANES:(c + 1) * LANES]
                x_ref[nl + c, pl.ds(r, nb), :] = n_im[:, c * LANES:(c + 1) * LANES]
            return n_re, n_im

        c_re, c_im = lax.fori_loop(0, ts, step, (carry_ref[j, :, 0:ns], carry_ref[j, :, ns:2 * ns]),
                                   unroll=S5_SCAN_UNROLL)
        carry_ref[j, :, 0:ns] = c_re
        carry_ref[j, :, ns:2 * ns] = c_im
        states = jnp.concatenate(
            [jnp.concatenate([x_ref[c, pl.ds(b, ts, stride=nb), :] for c in range(2 * nl)], axis=1)
             for b in range(nb)], axis=0)
        y_j = jnp.dot(states.astype(BF16), c_ref[j], preferred_element_type=F32)
        y_ref[:, ch] = _gelu(y_j + d_ref[:, ch] * u_j.astype(F32))

    y = y_ref[...]
    gate = jnp.dot(y.astype(BF16), gw_ref[...], preferred_element_type=F32) + gb_ref[...]
    o_ref[...] = (y * jax.nn.sigmoid(gate)).astype(o_ref.dtype).reshape(nb, ts, S5_WIDTH)


def _s5(z, p, batch, seq):
    assert batch == SUBLANES, "the S5 scan lays the batch out on the 8 sublanes"
    ts = min(S5_TIME_TILE, seq)
    kern = functools.partial(_s5_kernel, ts=ts)
    full = lambda shape: pl.BlockSpec(shape, lambda i: (0,) * len(shape))
    return pl.pallas_call(
        kern,
        grid=(seq // ts,),
        in_specs=[
            pl.BlockSpec((batch, ts, S5_WIDTH), lambda i: (0, i, 0)),
            full((S5_SLABS, S5_SLAB, 2 * S5_SLAB_STATE)),
            full((S5_SLABS, 2, SUBLANES, S5_SLAB_STATE)),
            full((S5_SLABS, 2 * S5_SLAB_STATE, S5_SLAB)),
            full((1, S5_WIDTH)),
            full((S5_WIDTH, S5_WIDTH)),
            full((1, S5_WIDTH)),
        ],
        out_specs=pl.BlockSpec((batch, ts, S5_WIDTH), lambda i: (0, i, 0)),
        out_shape=jax.ShapeDtypeStruct((batch, seq, S5_WIDTH), BF16),
        scratch_shapes=[
            pltpu.VMEM((S5_SLABS, SUBLANES, 2 * S5_SLAB_STATE), F32),
            pltpu.VMEM((2 * S5_SLAB_STATE // LANES, batch * ts, LANES), F32),
            pltpu.VMEM((batch * ts, S5_WIDTH), F32),
        ],
        compiler_params=_params("arbitrary"),
        name="s5",
    )(z, p["s5_b"], p["s5_tab"], p["s5_c"], p["s5_d"], p["glu_w"], p["glu_b"])


def _s5_params(a_re, a_im, log_dt, b_re, b_im, c_re, c_im, d_skip, glu_w, glu_b):
    lam_re = a_re.astype(F32)
    lam_im = a_im.astype(F32)
    dt = jnp.exp(log_dt.astype(F32))[:, None]
    decay = jnp.exp(lam_re * dt)
    abar_re = decay * jnp.cos(lam_im * dt)
    abar_im = decay * jnp.sin(lam_im * dt)
    inv_mag = 1.0 / (lam_re * lam_re + lam_im * lam_im)
    num_re = abar_re - 1.0
    f_re = (num_re * lam_re + abar_im * lam_im) * inv_mag
    f_im = (abar_im * lam_re - num_re * lam_im) * inv_mag
    br = b_re.astype(F32)
    bi = b_im.astype(F32)
    bbar_re = f_re[..., None] * br - f_im[..., None] * bi
    bbar_im = f_re[..., None] * bi + f_im[..., None] * br
    gps = S5_SLAB // S5_GROUP_CH
    eye = jnp.eye(gps, dtype=F32)

    def b_slab(t):
        t = t.reshape(S5_SLABS, gps, S5_STATE, S5_GROUP_CH)
        t = jnp.einsum('jgnh,gk->jghkn', t, eye)
        return t.reshape(S5_SLABS, S5_SLAB, S5_SLAB_STATE)

    def c_slab(t):
        t = t.astype(F32).reshape(S5_SLABS, gps, S5_GROUP_CH, S5_STATE)
        t = jnp.einsum('jghn,gk->jgnkh', t, eye)
        return t.reshape(S5_SLABS, S5_SLAB_STATE, S5_SLAB)

    b_mat = jnp.concatenate([b_slab(bbar_re), b_slab(bbar_im)], axis=2).astype(BF16)
    c_mat = jnp.concatenate([c_slab(c_re), -c_slab(c_im)], axis=1).astype(BF16)

    tab = jnp.stack([abar_re.reshape(S5_SLABS, 1, S5_SLAB_STATE), abar_im.reshape(S5_SLABS, 1, S5_SLAB_STATE)], axis=1)
    tab = jnp.broadcast_to(tab, (S5_SLABS, 2, SUBLANES, S5_SLAB_STATE))
    return {
        "s5_b": b_mat, "s5_c": c_mat, "s5_tab": tab,
        "s5_d": d_skip.astype(F32).reshape(1, S5_WIDTH),
        "glu_w": glu_w.astype(BF16), "glu_b": glu_b.astype(F32).reshape(1, S5_WIDTH),
    }


NSA_CMP_SLOTS = 128
NSA_Q_TILE = 256
NSA_K_TILE = 256
LOG2_E = math.log2(math.e)
ODD_A_KC = S5_WIDTH // LANES
ODD_A_KS = ODD_A_KC + NSA_KV_HEADS
ODD_A_KW = ODD_A_KS + NSA_KV_HEADS
ODD_A_END = ODD_A_KW + NSA_KV_HEADS
ODD_C_VS = NSA_KV_HEADS
ODD_C_VW = 2 * NSA_KV_HEADS
ODD_C_GL = 3 * NSA_KV_HEADS


def _compress_kernel(k_ref, v_ref, w1_ref, pos_ref, b1_ref, w2_ref, o_ref, xs_ref):
    n = o_ref.shape[1]
    st = NSA_CMP_STRIDE
    rows = lax.broadcasted_iota(jnp.int32, (n, NSA_HEAD_DIM), 0)
    for kind, src_ref in enumerate((k_ref, v_ref)):
        xs_ref[...] = src_ref[...].astype(F32)
        p1 = jnp.zeros((n, NSA_HEAD_DIM), F32)
        p2 = jnp.zeros((n, NSA_HEAD_DIM), F32)
        for l in range(st):
            xl = xs_ref[pl.ds(l, n, stride=st), :]
            p1 = p1 + jnp.dot((xl + pos_ref[kind, l:l + 1, :]).astype(BF16), w1_ref[kind, l],
                              preferred_element_type=F32)
            p2 = p2 + jnp.dot((xl + pos_ref[kind, st + l:st + l + 1, :]).astype(BF16), w1_ref[kind, st + l],
                              preferred_element_type=F32)
        p2_next = jnp.where(rows < n - 1, pltpu.roll(p2, n - 1, 0), 0.0)
        hid = _gelu(p1 + p2_next + b1_ref[kind])
        o_ref[kind] = jnp.dot(hid.astype(BF16), w2_ref[kind], preferred_element_type=F32).astype(o_ref.dtype)


def _compress(za, zc, p, batch, seq):
    nchunk = seq // NSA_CMP_STRIDE
    d = NSA_HEAD_DIM
    full = lambda shape: pl.BlockSpec(shape, lambda b, h: (0,) * len(shape))
    return pl.pallas_call(
        _compress_kernel,
        grid=(batch, NSA_KV_HEADS),
        in_specs=[
            pl.BlockSpec((None, seq, d), lambda b, h: (b, 0, ODD_A_KC + h)),
            pl.BlockSpec((None, seq, d), lambda b, h: (b, 0, h)),
            full((2, NSA_CMP_BLOCK, d, d)), full((2, NSA_CMP_BLOCK, d)), full((2, 1, d)), full((2, d, d)),
        ],
        out_specs=pl.BlockSpec((None, 2, None, nchunk, d), lambda b, h: (b, 0, h, 0, 0)),
        out_shape=jax.ShapeDtypeStruct((batch, 2, NSA_KV_HEADS, nchunk, d), BF16),
        scratch_shapes=[pltpu.VMEM((seq, d), F32)],
        compiler_params=_params("parallel", "arbitrary"),
        name="compress",
    )(za, zc, p["cmp_w1"], p["cmp_pos"], p["cmp_b1"], p["cmp_w2"])


def _nt_dot(a, b):
    return lax.dot_general(a, b, (((1,), (1,)), ((), ())), preferred_element_type=F32)


def _tn_dot(a, b):
    return lax.dot_general(a, b, (((0,), (0,)), ((), ())), preferred_element_type=F32)


def _nsa_kernel(q_ref, kc_ref, vc_ref, ks_ref, vs_ref, kw_ref, vw_ref, gl_ref, o_ref, acc_ref, accw_ref, *, ncmp, nslc):
    i = pl.program_id(2)
    tq, tk, grp, d = NSA_Q_TILE, NSA_K_TILE, NSA_GROUP, NSA_HEAD_DIM
    rows = grp * tq
    scale = d ** -0.5
    t0 = i * tq
    q = jnp.concatenate([q_ref[:, g * d:(g + 1) * d] for g in range(grp)], axis=0)
    t_q = t0 + lax.broadcasted_iota(jnp.int32, (1, tq), 1)
    t_all = jnp.concatenate([t_q] * grp, axis=1)

    s = _nt_dot(kc_ref[...], q) * scale
    cidx = lax.broadcasted_iota(jnp.int32, (ncmp, 1), 0)
    vis = (cidx * NSA_CMP_STRIDE + (NSA_CMP_BLOCK - 1)) <= t_all
    s = jnp.where(vis, s, NEG_BIG)
    mx = jnp.max(s, axis=0, keepdims=True)
    e = jnp.where(vis, jnp.exp(s - mx), 0.0)
    p_cmp = e / jnp.maximum(jnp.sum(e, axis=0, keepdims=True), 1.0)
    o_cmp = _tn_dot(vc_ref[...], p_cmp.astype(BF16))

    p_sum = p_cmp[:, 0:tq]
    for g in range(1, grp):
        p_sum = p_sum + p_cmp[:, g * tq:(g + 1) * tq]
    sj = lax.broadcasted_iota(jnp.int32, (nslc, ncmp), 0) * NSA_SLC_BLOCK
    ci = lax.broadcasted_iota(jnp.int32, (nslc, ncmp), 1) * NSA_CMP_STRIDE
    overlap = jnp.where((ci < sj + NSA_SLC_BLOCK) & (ci + NSA_CMP_BLOCK > sj), 1.0, 0.0).astype(BF16)
    p_hi = p_sum.astype(BF16)
    p_lo = (p_sum - p_hi.astype(F32)).astype(BF16)
    imp = (jnp.dot(overlap, p_hi, preferred_element_type=F32)
           + jnp.dot(overlap, p_lo, preferred_element_type=F32))

    blk = lax.broadcasted_iota(jnp.int32, (nslc, tq), 0)
    cur = t_q // NSA_SLC_BLOCK
    forced = (blk == 0) | (blk == cur) | (blk == cur - 1)
    score = jnp.where(blk > cur, -jnp.inf, jnp.where(forced, jnp.inf, imp))
    sel = jnp.zeros((nslc, tq), F32)
    for _ in range(NSA_TOP_N):
        best = jnp.max(score, axis=0, keepdims=True)
        pick = jnp.min(jnp.where(score == best, blk, nslc), axis=0, keepdims=True)
        hit = blk == pick
        sel = jnp.where(hit, 1.0, sel)
        score = jnp.where(hit, -jnp.inf, score)
    sel = sel.astype(BF16)

    krow = lax.broadcasted_iota(jnp.int32, (tk, 1), 0)
    bcol = lax.broadcasted_iota(jnp.int32, (tk, nslc), 1)

    def slc_mask(k0):
        kpos = k0 + krow
        expand = jnp.where((kpos // NSA_SLC_BLOCK) == bcol, 1.0, 0.0).astype(BF16)
        chosen = jnp.dot(expand, sel, preferred_element_type=F32) > 0.5
        return chosen & (kpos <= t_q)

    def win_mask(k0):
        kpos = k0 + krow
        return (kpos <= t_q) & (kpos > t_q - NSA_WINDOW)

    q_log2 = (q.astype(F32) * (scale * LOG2_E)).astype(BF16)

    def tile_scores(k_ref, mask_fn, k0):
        bias_q = jnp.where(mask_fn(k0), 0.0, NEG_BIG)
        bias = jnp.concatenate([bias_q] * grp, axis=1)
        return _nt_dot(k_ref[pl.ds(k0, tk), :], q_log2) + bias

    def tile_update(sc, carry, v_ref, acc, k0):
        m_old, l_old = carry
        m_new = jnp.maximum(m_old, jnp.max(sc, axis=0, keepdims=True))
        alpha = jnp.exp2(m_old - m_new)
        pr = jnp.exp2(sc - m_new)
        l_new = alpha * l_old + jnp.sum(pr, axis=0, keepdims=True)
        acc[...] = alpha * acc[...] + _tn_dot(v_ref[pl.ds(k0, tk), :], pr.astype(BF16))
        return m_new, l_new

    def slc_body(jt, carry):
        k0 = pl.multiple_of(jt * tk, tk)
        return tile_update(tile_scores(ks_ref, slc_mask, k0), carry, vs_ref, acc_ref, k0)

    def both_body(jt, carry):
        k0 = pl.multiple_of(jt * tk, tk)
        sc_s = tile_scores(ks_ref, slc_mask, k0)
        sc_w = tile_scores(kw_ref, win_mask, k0)
        return (tile_update(sc_s, carry[0], vs_ref, acc_ref, k0),
                tile_update(sc_w, carry[1], vw_ref, accw_ref, k0))

    acc_ref[...] = jnp.zeros_like(acc_ref)
    accw_ref[...] = jnp.zeros_like(accw_ref)
    init = (jnp.full((1, rows), NEG_BIG, F32), jnp.zeros((1, rows), F32))
    hi = (t0 + tq) // tk
    win_lo = jnp.maximum(t0 - NSA_WINDOW, 0) // tk
    state_s = lax.fori_loop(0, win_lo, slc_body, init)
    (_, l_s), (_, l_w) = lax.fori_loop(win_lo, hi, both_body, (state_s, init))
    o_slc = acc_ref[...] / l_s
    o_win = accw_ref[...] / l_w

    gates = jax.nn.sigmoid(gl_ref[...].astype(F32)).T
    for g in range(grp):
        cs = slice(g * tq, (g + 1) * tq)
        o_g = (gates[3 * g:3 * g + 1, :] * o_cmp[:, cs]
               + gates[3 * g + 1:3 * g + 2, :] * o_slc[:, cs]
               + gates[3 * g + 2:3 * g + 3, :] * o_win[:, cs])
        o_ref[:, g * d:(g + 1) * d] = o_g.T.astype(o_ref.dtype)


def _nsa(za, zq, zc, kvc, batch, seq):
    tq, d, grp = NSA_Q_TILE, NSA_HEAD_DIM, NSA_GROUP
    ntq = seq // tq
    ncmp = kvc.shape[3]
    kern = functools.partial(_nsa_kernel, ncmp=ncmp, nslc=seq // NSA_SLC_BLOCK)
    qw = grp * d

    def seq_spec(col0):
        return pl.BlockSpec((None, seq, d), lambda b, h, i: (b, 0, col0 + h))

    def cmp_spec(c):
        return pl.BlockSpec((None, None, None, ncmp, d), lambda b, h, i: (b, c, h, 0, 0))

    return pl.pallas_call(
        kern,
        grid=(batch, NSA_KV_HEADS, ntq),
        in_specs=[
            pl.BlockSpec((None, tq, qw), lambda b, h, i: (b, i, h)),
            cmp_spec(0), cmp_spec(1),
            seq_spec(ODD_A_KS), seq_spec(ODD_C_VS), seq_spec(ODD_A_KW), seq_spec(ODD_C_VW),
            pl.BlockSpec((None, tq, LANES), lambda b, h, i: (b, i, ODD_C_GL + h)),
        ],
        out_specs=pl.BlockSpec((tq, qw), lambda b, h, i: (b * ntq + i, h)),
        out_shape=jax.ShapeDtypeStruct((batch * seq, NSA_HEADS * d), BF16),
        scratch_shapes=[pltpu.VMEM((d, grp * tq), F32), pltpu.VMEM((d, grp * tq), F32)],
        compiler_params=_params("parallel", "parallel", "arbitrary"),
        name="nsa",
    )(zq, kvc, kvc, za, zc, za, zc, zc)


def _odd_params(w_in, a_re, a_im, log_dt, b_re, b_im, c_re, c_im, d_skip, glu_w, glu_b,
                cmp_pos, cmp_w1, cmp_b1, cmp_w2, seq):
    kvw = NSA_KV_WIDTH
    qw = NSA_HEADS * NSA_HEAD_DIM
    o_q = S5_WIDTH
    o_kv = o_q + qw
    parts = {name: w_in[:, o_kv + n * kvw:o_kv + (n + 1) * kvw]
             for n, name in enumerate(("kc", "vc", "ks", "vs", "kw", "vw"))}
    w_gl = w_in[:, o_kv + 6 * kvw:].reshape(D_MODEL, NSA_KV_HEADS, 3 * NSA_GROUP)
    w_gl = jnp.pad(w_gl, ((0, 0), (0, 0), (0, LANES - 3 * NSA_GROUP))).reshape(D_MODEL, NSA_KV_HEADS * LANES)
    w_a = jnp.concatenate([w_in[:, :o_q], parts["kc"], parts["ks"], parts["kw"]], axis=1).astype(BF16)
    w_q = w_in[:, o_q:o_kv].astype(BF16)
    w_c = jnp.concatenate([parts["vc"], parts["vs"], parts["vw"], w_gl], axis=1).astype(BF16)
    p = _s5_params(a_re, a_im, log_dt, b_re, b_im, c_re, c_im, d_skip, glu_w, glu_b)
    p.update({
        "w_a": w_a, "w_q": w_q, "w_c": w_c,
        "cmp_w1": cmp_w1.astype(BF16),
        "cmp_pos": cmp_pos.astype(F32),
        "cmp_b1": cmp_b1.astype(F32).reshape(2, 1, NSA_HEAD_DIM),
        "cmp_w2": cmp_w2.astype(BF16),
        "tables": _rope_tables(seq, NSA_HEAD_DIM),
    })
    return p


def _odd_mixer(x, h, g_post, p, w_out, layer, batch, seq):
    tables = p["tables"]
    za = _proj(h, p["w_a"], tables, seq, (ODD_A_KC * LANES, ODD_A_END * LANES))
    zq = _proj(h, p["w_q"], tables, seq, (0, NSA_HEADS * NSA_HEAD_DIM))
    zc = _proj(h, p["w_c"], tables, seq, (0, 0))
    s5_out = _s5(za, p, batch, seq)
    kvc = _compress(za, zc, p, batch, seq)
    o = _nsa(za, zq, zc, kvc, batch, seq)
    return _outproj(s5_out, o, w_out, layer, x, g_post)


def kernel(x, norm_gains, ffn1_w_gate, ffn1_w_up, ffn1_w_down, ffn2_w_gate, ffn2_w_up, ffn2_w_down, ev_w_in, ev_w_out, pool_w, pool_scale, swa_sinks, od_w_in, od_w_out, s5_a_re, s5_a_im, s5_log_dt, s5_b_re, s5_b_im, s5_c_re, s5_c_im, s5_d, s5_glu_w, s5_glu_b, nsa_cmp_pos, nsa_cmp_w1, nsa_cmp_b1, nsa_cmp_w2):
    batch, seq, _ = x.shape
    m = batch * seq
    depth = norm_gains.shape[0]
    xs = x.reshape(m, D_MODEL)
    even_tables = _rope_tables(seq, SWA_HEAD_DIM)
    ffn_w = [(_cast_bf16(wg, cols_out=D_FF_PAD, col_tile=FF_TILE), _cast_bf16(wu, cols_out=D_FF_PAD, col_tile=FF_TILE),
              _cast_bf16(wd, rows_out=D_FF_PAD))
             for wg, wu, wd in ((ffn1_w_gate, ffn1_w_up, ffn1_w_down), (ffn2_w_gate, ffn2_w_up, ffn2_w_down))]
    ev_out, od_out = _cast_bf16(ev_w_out), _cast_bf16(od_w_out)
    for layer in range(depth):
        g = norm_gains[layer].astype(F32).reshape(6, 1, D_MODEL)
        i = layer // 2
        xs, h = _ffn(xs, g[0], *ffn_w[0], g[1], layer, g_next=g[2])
        if layer % 2 == 0:
            xs = _even_mixer(xs, h, g[3], *_even_weights(ev_w_in[i]), ev_out, i,
                             pool_w[i].astype(BF16), pool_scale[i].astype(F32).reshape(1, POOL_WIDTH),
                             swa_sinks[i].astype(F32), even_tables, batch, seq)
        else:
            p = _odd_params(od_w_in[i], s5_a_re[i], s5_a_im[i], s5_log_dt[i], s5_b_re[i], s5_b_im[i],
                            s5_c_re[i], s5_c_im[i], s5_d[i], s5_glu_w[i], s5_glu_b[i], nsa_cmp_pos[i],
                            nsa_cmp_w1[i], nsa_cmp_b1[i], nsa_cmp_w2[i], seq)
            xs = _odd_mixer(xs, h, g[3], p, od_out, i, batch, seq)
        xs = _ffn(xs, g[4], *ffn_w[1], g[5], layer)
    return xs.reshape(batch, seq, D_MODEL)
```

```python
import functools
import math

import jax
import jax.numpy as jnp
from jax import lax
from jax.experimental import pallas as pl
from jax.experimental.pallas import tpu as pltpu

F32 = jnp.float32
BF16 = jnp.bfloat16

D_MODEL = 4096
D_FF = 5504
NORM_EPS = 1e-6
ROPE_THETA = 500000.0
ROPE_FRACTION = 4
POOL_WINDOWS = (2, 4, 8, 16)
POOL_WIDTH = D_MODEL // 2
POOL_GROUP = POOL_WIDTH // len(POOL_WINDOWS)
POOL_HALO = 16
SWA_HEAD_DIM = 64
SWA_HEADS = 32
SWA_KV_HEADS = 4
SWA_GROUP = SWA_HEADS // SWA_KV_HEADS
SWA_WINDOW = 128
ATTN_BLOCK = 128
S5_WIDTH = D_MODEL // 4
S5_GROUP_CH = 16
S5_STATE = 64
NSA_HEAD_DIM = 128
NSA_HEADS = 24
NSA_KV_HEADS = 6
NSA_GROUP = 4
NSA_CMP_BLOCK = 32
NSA_CMP_STRIDE = 16
NSA_SLC_BLOCK = 64
NSA_TOP_N = 8
NSA_WINDOW = 512
NSA_KV_WIDTH = NSA_KV_HEADS * NSA_HEAD_DIM

LANES = 128
SUBLANES = 8
MXU_DIM = 256
VMEM_LIMIT_BYTES = 56 * 1024 * 1024

ROW_TILE = 512
FF_TILE = 256
D_FF_PAD = -(-D_FF // FF_TILE) * FF_TILE
PROJ_ROW_TILE = 512
OUT_ROW_TILE = 256
OUT_COL_CHUNK = 512
OUT_VMEM_LIMIT_BYTES = 60 * 1024 * 1024
FFN_NEXT_VMEM_LIMIT_BYTES = 60 * 1024 * 1024
NORM_ROWS = 32
NORM_UNROLL = 2
CAST_ROWS = 256
NEG_BIG = -1e30


def _params(*sem, vmem=VMEM_LIMIT_BYTES):
    return pltpu.CompilerParams(dimension_semantics=sem, vmem_limit_bytes=vmem)


def _rms_scaled(v, g):
    ms = jnp.mean(v * v, axis=-1, keepdims=True)
    return v * lax.rsqrt(ms + NORM_EPS) * g


def _norm_chunk(c):
    return pl.ds(pl.multiple_of(c * NORM_ROWS, NORM_ROWS), NORM_ROWS)


def _norm_rows(dst_ref, src_ref, g_ref, rows):
    g = g_ref[...]

    def body(c, carry):
        rs = _norm_chunk(c)
        dst_ref[rs, :] = _rms_scaled(src_ref[rs, :].astype(F32), g).astype(dst_ref.dtype)
        return carry

    lax.fori_loop(0, rows // NORM_ROWS, body, 0, unroll=NORM_UNROLL)


def _norm_residual_rows(acc_ref, g_ref, res_ref, res_scale, inv_ref, rows, next_ref=None, gnext_ref=None):
    g = g_ref[...]
    g_next = None if next_ref is None else gnext_ref[...]
    reps = acc_ref.shape[1] // LANES

    def stats(c, carry):
        rs = _norm_chunk(c)
        v = acc_ref[rs, :]
        ms = jnp.mean(v * v, axis=-1, keepdims=True)
        inv_ref[rs, :] = jnp.broadcast_to(lax.rsqrt(ms + NORM_EPS), (NORM_ROWS, LANES))
        return carry

    def scale(c, carry):
        rs = _norm_chunk(c)
        y = res_ref[rs, :] + res_scale * (acc_ref[rs, :] * jnp.tile(inv_ref[rs, :], (1, reps)) * g)
        acc_ref[rs, :] = y
        if next_ref is not None:
            next_ref[rs, :] = _rms_scaled(y, g_next).astype(next_ref.dtype)
        return carry

    lax.fori_loop(0, rows // NORM_ROWS, stats, 0, unroll=NORM_UNROLL)
    lax.fori_loop(0, rows // NORM_ROWS, scale, 0, unroll=NORM_UNROLL)


def _cast_kernel(x_ref, o_ref, *, tr, rows_in, cols_in):
    x = x_ref[...]
    if rows_in % tr:
        r = pl.program_id(1) * tr + lax.broadcasted_iota(jnp.int32, (tr, 1), 0)
        x = jnp.where(r < rows_in, x, 0.0)
    if cols_in == o_ref.shape[1]:
        o_ref[...] = x.astype(o_ref.dtype)
    else:
        o_ref[:, :cols_in] = x.astype(o_ref.dtype)
        o_ref[:, cols_in:] = jnp.zeros((tr, o_ref.shape[1] - cols_in), o_ref.dtype)


def _cast_bf16(w, rows_out=None, cols_out=None):
    nl, rows_in, cols_in = w.shape
    rows_out = rows_out or rows_in
    cols_out = cols_out or cols_in
    tr = min(CAST_ROWS, rows_out)
    kern = functools.partial(_cast_kernel, tr=tr, rows_in=rows_in, cols_in=cols_in)
    return pl.pallas_call(
        kern,
        grid=(nl, rows_out // tr),
        in_specs=[pl.BlockSpec((None, tr, cols_in), lambda l, i: (l, i, 0))],
        out_specs=pl.BlockSpec((None, tr, cols_out), lambda l, i: (l, i, 0)),
        out_shape=jax.ShapeDtypeStruct((nl, rows_out, cols_out), BF16),
        compiler_params=_params("parallel", "parallel"),
        name="cast",
    )(w)


def _ffn_kernel(x_ref, gpre_ref, wg_ref, wu_ref, wd_ref, gpost_ref, *rest, tm, nj):
    gnext_ref = rest[0] if len(rest) == 4 else None
    o_ref, h_ref, inv_ref = rest[-3:]
    j = pl.program_id(1)

    @pl.when(j == 0)
    def _():
        _norm_rows(h_ref, x_ref, gpre_ref, tm)
        o_ref[...] = jnp.zeros_like(o_ref)

    h = h_ref[...]
    gate = jnp.dot(h, wg_ref[...], preferred_element_type=F32)
    up = jnp.dot(h, wu_ref[...], preferred_element_type=F32)
    act = (gate * jax.nn.sigmoid(gate) * up).astype(BF16)
    nc = 512
    for n in range(D_MODEL // nc):
        sl = slice(n * nc, (n + 1) * nc)
        o_ref[:, sl] += jnp.dot(act, wd_ref[:, sl], preferred_element_type=F32)

    @pl.when(j == nj - 1)
    def _():
        _norm_residual_rows(o_ref, gpost_ref, x_ref, 0.5, inv_ref, tm,
                            next_ref=None if gnext_ref is None else h_ref, gnext_ref=gnext_ref)


def _ffn(x, g_pre, wg, wu, wd, g_post, layer, g_next=None):
    m = x.shape[0]
    tm = min(ROW_TILE, m)
    nj = D_FF_PAD // FF_TILE
    kern = functools.partial(_ffn_kernel, tm=tm, nj=nj)
    row_spec = pl.BlockSpec((tm, D_MODEL), lambda i, j: (i, 0))
    gain_spec = pl.BlockSpec((1, D_MODEL), lambda i, j: (0, 0))
    in_specs = [
        row_spec, gain_spec,
        pl.BlockSpec((None, D_MODEL, FF_TILE), lambda i, j: (layer, 0, j)),
        pl.BlockSpec((None, D_MODEL, FF_TILE), lambda i, j: (layer, 0, j)),
        pl.BlockSpec((None, FF_TILE, D_MODEL), lambda i, j: (layer, j, 0)),
        gain_spec,
    ]
    x_shape = jax.ShapeDtypeStruct((m, D_MODEL), F32)
    inv_scratch = pltpu.VMEM((tm, LANES), F32)
    if g_next is None:
        return pl.pallas_call(
            kern, grid=(m // tm, nj), in_specs=in_specs, out_specs=row_spec, out_shape=x_shape,
            scratch_shapes=[pltpu.VMEM((tm, D_MODEL), BF16), inv_scratch],
            compiler_params=_params("parallel", "arbitrary"), name="ffn",
        )(x, g_pre, wg, wu, wd, g_post)
    return pl.pallas_call(
        kern, grid=(m // tm, nj), in_specs=in_specs + [gain_spec], out_specs=(row_spec, row_spec),
        out_shape=(x_shape, jax.ShapeDtypeStruct((m, D_MODEL), BF16)), scratch_shapes=[inv_scratch],
        compiler_params=_params("parallel", "arbitrary", vmem=FFN_NEXT_VMEM_LIMIT_BYTES), name="ffn_next",
    )(x, g_pre, wg, wu, wd, g_post, g_next)


def _proj_kernel(h_ref, w_ref, cos_ref, sa_ref, sb_ref, o_ref, *, rope_chunks, half):
    h = h_ref[...]
    tc = MXU_DIM
    for c in range(o_ref.shape[1] // tc):
        cs = slice(c * tc, (c + 1) * tc)
        z = jnp.dot(h, w_ref[:, cs], preferred_element_type=F32)
        if c in rope_chunks:
            parts = []
            for p in range(tc // LANES):
                zp = z[:, p * LANES:(p + 1) * LANES]
                parts.append(zp * cos_ref[...]
                             + pltpu.roll(zp, LANES - half, 1) * sa_ref[...]
                             + pltpu.roll(zp, half, 1) * sb_ref[...])
            z = jnp.concatenate(parts, axis=1)
        o_ref[:, cs] = z.astype(o_ref.dtype)


def _rope_tables(seq, head_dim):
    rot = head_dim // ROPE_FRACTION
    half = rot // 2
    inv_freq = jnp.power(ROPE_THETA, -jnp.arange(half, dtype=F32) * 2.0 / rot)
    ang = jnp.arange(seq, dtype=jnp.int32).astype(F32)[:, None] * inv_freq[None, :]
    cos, sin = jnp.cos(ang), jnp.sin(ang)
    one = jnp.ones((seq, head_dim - rot), F32)
    zero_h = jnp.zeros((seq, half), F32)
    zero_r = jnp.zeros((seq, head_dim - rot), F32)
    c_head = jnp.concatenate([cos, cos, one], axis=1)
    sa_head = jnp.concatenate([-sin, zero_h, zero_r], axis=1)
    sb_head = jnp.concatenate([zero_h, sin, zero_r], axis=1)
    reps = LANES // head_dim
    return jnp.tile(c_head, (1, reps)), jnp.tile(sa_head, (1, reps)), jnp.tile(sb_head, (1, reps)), half


def _proj(h, w, tables, seq, rope_cols):
    cos_t, sa_t, sb_t, half = tables
    m = h.shape[0]
    gw = w.shape[1]
    tm = min(PROJ_ROW_TILE, seq)
    sblocks = seq // tm
    rope_chunks = frozenset(range(rope_cols[0] // MXU_DIM, rope_cols[1] // MXU_DIM))
    kern = functools.partial(_proj_kernel, rope_chunks=rope_chunks, half=half)
    tab_spec = pl.BlockSpec((tm, LANES), lambda i: (i % sblocks, 0))
    return pl.pallas_call(
        kern,
        grid=(m // tm,),
        in_specs=[
            pl.BlockSpec((tm, D_MODEL), lambda i: (i, 0)),
            pl.BlockSpec((D_MODEL, gw), lambda i: (0, 0), pipeline_mode=pl.Buffered(1)),
            tab_spec, tab_spec, tab_spec,
        ],
        out_specs=pl.BlockSpec((None, tm, gw), lambda i: (i // sblocks, i % sblocks, 0)),
        out_shape=jax.ShapeDtypeStruct((m // seq, seq, gw), BF16),
        compiler_params=_params("parallel"),
        name="proj",
    )(h, w, cos_t, sa_t, sb_t)


def _outproj_kernel(a1_ref, a2_ref, w_ref, g_ref, x_ref, o_ref, inv_ref, *, tm, k1):
    a1 = a1_ref[...]
    a2 = a2_ref[...]
    nc = OUT_COL_CHUNK
    for n in range(D_MODEL // nc):
        sl = slice(n * nc, (n + 1) * nc)
        o_ref[:, sl] = (jnp.dot(a1, w_ref[0:k1, sl], preferred_element_type=F32)
                        + jnp.dot(a2, w_ref[k1:, sl], preferred_element_type=F32))
    _norm_residual_rows(o_ref, g_ref, x_ref, 1.0, inv_ref, tm)


def _outproj(a1, a2, w, layer, x, g_post):
    m = x.shape[0]
    k1, k2 = a1.shape[-1], a2.shape[-1]
    if a1.ndim == 3:
        tm = min(OUT_ROW_TILE, a1.shape[1])
        sb = a1.shape[1] // tm
        a1_spec = pl.BlockSpec((None, tm, k1), lambda i: (i // sb, i % sb, 0))
    else:
        tm = min(OUT_ROW_TILE, m)
        a1_spec = pl.BlockSpec((tm, k1), lambda i: (i, 0))
    kern = functools.partial(_outproj_kernel, tm=tm, k1=k1)
    return pl.pallas_call(
        kern,
        grid=(m // tm,),
        in_specs=[
            a1_spec,
            pl.BlockSpec((tm, k2), lambda i: (i, 0)),
            pl.BlockSpec((None, k1 + k2, D_MODEL), lambda i: (layer, 0, 0), pipeline_mode=pl.Buffered(1)),
            pl.BlockSpec((1, D_MODEL), lambda i: (0, 0)),
            pl.BlockSpec((tm, D_MODEL), lambda i: (i, 0)),
        ],
        out_specs=pl.BlockSpec((tm, D_MODEL), lambda i: (i, 0)),
        out_shape=jax.ShapeDtypeStruct((m, D_MODEL), F32),
        scratch_shapes=[pltpu.VMEM((tm, LANES), F32)],
        compiler_params=_params("parallel", vmem=OUT_VMEM_LIMIT_BYTES),
        name="outproj",
    )(a1, a2, w, g_post, x)


def _swa_kernel(sink_ref, q_ref, kvp_ref, kvc_ref, o_ref):
    n = pl.program_id(1)
    blk = ATTN_BLOCK
    kvw = SWA_KV_HEADS * SWA_HEAD_DIM
    kv = jnp.concatenate([kvp_ref[...], kvc_ref[...]], axis=0)
    krow = lax.broadcasted_iota(jnp.int32, (2 * blk, 1), 0)
    qcol = lax.broadcasted_iota(jnp.int32, (1, blk), 1)
    diff = qcol - krow + blk
    vis = (diff >= 0) & (diff < SWA_WINDOW) & ((krow >= blk) | (n > 0))
    bias = jnp.where(vis, 0.0, NEG_BIG)
    scale = SWA_HEAD_DIM ** -0.5 * LOG2_E
    pair = LANES // SWA_HEAD_DIM
    for kh in range(SWA_KV_HEADS):
        k_h = kv[:, kh * SWA_HEAD_DIM:(kh + 1) * SWA_HEAD_DIM]
        v_h = kv[:, kvw + kh * SWA_HEAD_DIM:kvw + (kh + 1) * SWA_HEAD_DIM]
        heads = range(kh * SWA_GROUP, (kh + 1) * SWA_GROUP)
        scores = [_nt_dot(k_h, q_ref[:, h * SWA_HEAD_DIM:(h + 1) * SWA_HEAD_DIM]) for h in heads]
        weights, denoms = [], []
        for h, s in zip(heads, scores):
            s = s * scale + bias
            sk = sink_ref[h] * LOG2_E
            mx = jnp.maximum(jnp.max(s, axis=0, keepdims=True), sk)
            e = jnp.exp2(s - mx)
            denoms.append(jnp.sum(e, axis=0, keepdims=True) + jnp.exp2(sk - mx))
            weights.append(e.astype(BF16))
        outs = [_tn_dot(v_h, e) / d for e, d in zip(weights, denoms)]
        for g0 in range(0, SWA_GROUP, pair):
            h0 = kh * SWA_GROUP + g0
            o_ref[:, h0 * SWA_HEAD_DIM:(h0 + pair) * SWA_HEAD_DIM] = (
                jnp.concatenate(outs[g0:g0 + pair], axis=0).T.astype(o_ref.dtype))


def _swa(zq, za, sinks, batch, seq):
    blk = ATTN_BLOCK
    nblk = seq // blk
    qw = SWA_HEADS * SWA_HEAD_DIM
    kvw2 = 2 * SWA_KV_HEADS * SWA_HEAD_DIM
    kv_blk = POOL_WIDTH // kvw2
    return pl.pallas_call(
        _swa_kernel,
        grid=(batch, nblk),
        in_specs=[
            pl.BlockSpec(memory_space=pltpu.SMEM),
            pl.BlockSpec((None, blk, qw), lambda b, n: (b, n, 0)),
            pl.BlockSpec((None, blk, kvw2), lambda b, n: (b, jnp.maximum(n - 1, 0), kv_blk)),
            pl.BlockSpec((None, blk, kvw2), lambda b, n: (b, n, kv_blk)),
        ],
        out_specs=pl.BlockSpec((blk, qw), lambda b, n: (b * nblk + n, 0)),
        out_shape=jax.ShapeDtypeStruct((batch * seq, qw), BF16),
        compiler_params=_params("parallel", "arbitrary"),
        name="swa",
    )(sinks, zq, za, za)


def _pool_kernel(up_ref, uc_ref, w_ref, scale_ref, o_ref, *, ts):
    i = pl.program_id(1)
    has_prev = (i > 0).astype(F32)
    pos = i * ts + lax.broadcasted_iota(jnp.int32, (ts, 1), 0)
    for gi, win in enumerate(POOL_WINDOWS):
        sl = slice(gi * POOL_GROUP, (gi + 1) * POOL_GROUP)
        cur = uc_ref[:, sl].astype(F32)
        prev = up_ref[:, sl].astype(F32) * has_prev
        acc = jnp.concatenate([prev, cur], axis=0)
        d = 1
        while d < win:
            acc = acc + pltpu.roll(acc, d, 0)
            d *= 2
        wsum = acc[POOL_HALO:, :]
        count = jnp.minimum(pos + 1, win).astype(F32)
        zz = (wsum / count - cur).astype(BF16)
        a = jnp.dot(zz, w_ref[gi], preferred_element_type=F32) * scale_ref[:, sl]
        o_ref[:, sl] = a.astype(o_ref.dtype)


def _pool(za, w_pool, pool_scale, batch, seq):
    ts = min(256, seq)
    nts = seq // ts
    halo_blocks = ts // POOL_HALO
    kern = functools.partial(_pool_kernel, ts=ts)
    return pl.pallas_call(
        kern,
        grid=(batch, nts),
        in_specs=[
            pl.BlockSpec((None, POOL_HALO, POOL_WIDTH), lambda b, i: (b, jnp.maximum(i * halo_blocks - 1, 0), 0)),
            pl.BlockSpec((None, ts, POOL_WIDTH), lambda b, i: (b, i, 0)),
            pl.BlockSpec((len(POOL_WINDOWS), POOL_GROUP, POOL_GROUP), lambda b, i: (0, 0, 0)),
            pl.BlockSpec((1, POOL_WIDTH), lambda b, i: (0, 0)),
        ],
        out_specs=pl.BlockSpec((ts, POOL_WIDTH), lambda b, i: (b * nts + i, 0)),
        out_shape=jax.ShapeDtypeStruct((batch * seq, POOL_WIDTH), BF16),
        compiler_params=_params("parallel", "arbitrary"),
        name="pool",
    )(za, za, w_pool, pool_scale)


def _even_weights(w_in):
    qw = SWA_HEADS * SWA_HEAD_DIM
    w_a = jnp.concatenate([w_in[:, :POOL_WIDTH], w_in[:, POOL_WIDTH + qw:]], axis=1).astype(BF16)
    return w_a, w_in[:, POOL_WIDTH:POOL_WIDTH + qw].astype(BF16)


def _even_mixer(x, h, g_post, w_a, w_q, w_out, layer, w_pool, pool_scale, sinks, tables, batch, seq):
    qw = SWA_HEADS * SWA_HEAD_DIM
    k_cols = (POOL_WIDTH, POOL_WIDTH + SWA_KV_HEADS * SWA_HEAD_DIM)
    za = _proj(h, w_a, tables, seq, k_cols)
    zq = _proj(h, w_q, tables, seq, (0, qw))
    o = _swa(zq, za, sinks, batch, seq)
    a = _pool(za, w_pool, pool_scale, batch, seq)
    return _outproj(a, o, w_out, layer, x, g_post)


S5_SLAB = LANES
S5_SLABS = S5_WIDTH // S5_SLAB
S5_SLAB_STATE = (S5_SLAB // S5_GROUP_CH) * S5_STATE
S5_TIME_TILE = 128
S5_SCAN_UNROLL = 8


def _gelu(y):
    return 0.5 * y * (1.0 + jnp.tanh(math.sqrt(2.0 / math.pi) * (y + 0.044715 * (y * y * y))))


def _s5_kernel(u_ref, b_ref, tab_ref, c_ref, d_ref, gw_ref, gb_ref, o_ref, carry_ref, x_ref, y_ref, *, ts):
    i = pl.program_id(0)
    ns = S5_SLAB_STATE
    nb = SUBLANES
    nl = ns // LANES

    @pl.when(i == 0)
    def _():
        carry_ref[...] = jnp.zeros_like(carry_ref)

    u_all = u_ref[...].reshape(nb * ts, S5_WIDTH)
    for j in range(S5_SLABS):
        ch = slice(j * S5_SLAB, (j + 1) * S5_SLAB)
        u_j = u_all[:, ch]
        bu = jnp.dot(u_j, b_ref[j], preferred_element_type=F32)
        for b in range(nb):
            for c in range(2 * nl):
                x_ref[c, pl.ds(b, ts, stride=nb), :] = bu[b * ts:(b + 1) * ts, c * LANES:(c + 1) * LANES]
        a_re, a_im = tab_ref[j, 0], tab_ref[j, 1]

        def step(t, carry):
            c_re, c_im = carry
            r = pl.multiple_of(t * nb, nb)
            v_re = jnp.concatenate([x_ref[c, pl.ds(r, nb), :] for c in range(nl)], axis=1)
            v_im = jnp.concatenate([x_ref[nl + c, pl.ds(r, nb), :] for c in range(nl)], axis=1)
            n_re = v_re + a_re * c_re - a_im * c_im
            n_im = v_im + a_re * c_im + a_im * c_re
            for c in range(nl):
                x_ref[c, pl.ds(r, nb), :] = n_re[:, c * LANES:(c + 1) * LANES]
                x_ref[nl + c, pl.ds(r, nb), :] = n_im[:, c * LANES:(c + 1) * LANES]
            return n_re, n_im

        c_re, c_im = lax.fori_loop(0, ts, step, (carry_ref[j, :, 0:ns], carry_ref[j, :, ns:2 * ns]),
                                   unroll=S5_SCAN_UNROLL)
        carry_ref[j, :, 0:ns] = c_re
        carry_ref[j, :, ns:2 * ns] = c_im
        states = jnp.concatenate(
            [jnp.concatenate([x_ref[c, pl.ds(b, ts, stride=nb), :] for c in range(2 * nl)], axis=1)
             for b in range(nb)], axis=0)
        y_j = jnp.dot(states.astype(BF16), c_ref[j], preferred_element_type=F32)
        y_ref[:, ch] = _gelu(y_j + d_ref[:, ch] * u_j.astype(F32))

    y = y_ref[...]
    gate = jnp.dot(y.astype(BF16), gw_ref[...], preferred_element_type=F32) + gb_ref[...]
    o_ref[...] = (y * jax.nn.sigmoid(gate)).astype(o_ref.dtype).reshape(nb, ts, S5_WIDTH)


def _s5(z, p, batch, seq):
    assert batch == SUBLANES, "the S5 scan lays the batch out on the 8 sublanes"
    ts = min(S5_TIME_TILE, seq)
    kern = functools.partial(_s5_kernel, ts=ts)
    full = lambda shape: pl.BlockSpec(shape, lambda i: (0,) * len(shape))
    return pl.pallas_call(
        kern,
        grid=(seq // ts,),
        in_specs=[
            pl.BlockSpec((batch, ts, S5_WIDTH), lambda i: (0, i, 0)),
            full((S5_SLABS, S5_SLAB, 2 * S5_SLAB_STATE)),
            full((S5_SLABS, 2, SUBLANES, S5_SLAB_STATE)),
            full((S5_SLABS, 2 * S5_SLAB_STATE, S5_SLAB)),
            full((1, S5_WIDTH)),
            full((S5_WIDTH, S5_WIDTH)),
            full((1, S5_WIDTH)),
        ],
        out_specs=pl.BlockSpec((batch, ts, S5_WIDTH), lambda i: (0, i, 0)),
        out_shape=jax.ShapeDtypeStruct((batch, seq, S5_WIDTH), BF16),
        scratch_shapes=[
            pltpu.VMEM((S5_SLABS, SUBLANES, 2 * S5_SLAB_STATE), F32),
            pltpu.VMEM((2 * S5_SLAB_STATE // LANES, batch * ts, LANES), F32),
            pltpu.VMEM((batch * ts, S5_WIDTH), F32),
        ],
        compiler_params=_params("arbitrary"),
        name="s5",
    )(z, p["s5_b"], p["s5_tab"], p["s5_c"], p["s5_d"], p["glu_w"], p["glu_b"])


def _s5_params(a_re, a_im, log_dt, b_re, b_im, c_re, c_im, d_skip, glu_w, glu_b):
    lam_re = a_re.astype(F32)
    lam_im = a_im.astype(F32)
    dt = jnp.exp(log_dt.astype(F32))[:, None]
    decay = jnp.exp(lam_re * dt)
    abar_re = decay * jnp.cos(lam_im * dt)
    abar_im = decay * jnp.sin(lam_im * dt)
    inv_mag = 1.0 / (lam_re * lam_re + lam_im * lam_im)
    num_re = abar_re - 1.0
    f_re = (num_re * lam_re + abar_im * lam_im) * inv_mag
    f_im = (abar_im * lam_re - num_re * lam_im) * inv_mag
    br = b_re.astype(F32)
    bi = b_im.astype(F32)
    bbar_re = f_re[..., None] * br - f_im[..., None] * bi
    bbar_im = f_re[..., None] * bi + f_im[..., None] * br
    gps = S5_SLAB // S5_GROUP_CH
    eye = jnp.eye(gps, dtype=F32)

    def b_slab(t):
        t = t.reshape(S5_SLABS, gps, S5_STATE, S5_GROUP_CH)
        t = jnp.einsum('jgnh,gk->jghkn', t, eye)
        return t.reshape(S5_SLABS, S5_SLAB, S5_SLAB_STATE)

    def c_slab(t):
        t = t.astype(F32).reshape(S5_SLABS, gps, S5_GROUP_CH, S5_STATE)
        t = jnp.einsum('jghn,gk->jgnkh', t, eye)
        return t.reshape(S5_SLABS, S5_SLAB_STATE, S5_SLAB)

    b_mat = jnp.concatenate([b_slab(bbar_re), b_slab(bbar_im)], axis=2).astype(BF16)
    c_mat = jnp.concatenate([c_slab(c_re), -c_slab(c_im)], axis=1).astype(BF16)

    tab = jnp.stack([abar_re.reshape(S5_SLABS, 1, S5_SLAB_STATE), abar_im.reshape(S5_SLABS, 1, S5_SLAB_STATE)], axis=1)
    tab = jnp.broadcast_to(tab, (S5_SLABS, 2, SUBLANES, S5_SLAB_STATE))
    return {
        "s5_b": b_mat, "s5_c": c_mat, "s5_tab": tab,
        "s5_d": d_skip.astype(F32).reshape(1, S5_WIDTH),
        "glu_w": glu_w.astype(BF16), "glu_b": glu_b.astype(F32).reshape(1, S5_WIDTH),
    }


NSA_Q_TILE = 256
NSA_K_TILE = 256
LOG2_E = math.log2(math.e)
ODD_A_KC = S5_WIDTH // LANES
ODD_A_KS = ODD_A_KC + NSA_KV_HEADS
ODD_A_KW = ODD_A_KS + NSA_KV_HEADS
ODD_A_END = ODD_A_KW + NSA_KV_HEADS
ODD_C_VS = NSA_KV_HEADS
ODD_C_VW = 2 * NSA_KV_HEADS
ODD_C_GL = 3 * NSA_KV_HEADS


def _compress_kernel(k_ref, v_ref, w1_ref, pos_ref, b1_ref, w2_ref, o_ref, xs_ref):
    n = o_ref.shape[1]
    st = NSA_CMP_STRIDE
    rows = lax.broadcasted_iota(jnp.int32, (n, NSA_HEAD_DIM), 0)
    for kind, src_ref in enumerate((k_ref, v_ref)):
        xs_ref[...] = src_ref[...].astype(F32)
        p1 = jnp.zeros((n, NSA_HEAD_DIM), F32)
        p2 = jnp.zeros((n, NSA_HEAD_DIM), F32)
        for l in range(st):
            xl = xs_ref[pl.ds(l, n, stride=st), :]
            p1 = p1 + jnp.dot((xl + pos_ref[kind, l:l + 1, :]).astype(BF16), w1_ref[kind, l],
                              preferred_element_type=F32)
            p2 = p2 + jnp.dot((xl + pos_ref[kind, st + l:st + l + 1, :]).astype(BF16), w1_ref[kind, st + l],
                              preferred_element_type=F32)
        p2_next = jnp.where(rows < n - 1, pltpu.roll(p2, n - 1, 0), 0.0)
        hid = _gelu(p1 + p2_next + b1_ref[kind])
        o_ref[kind] = jnp.dot(hid.astype(BF16), w2_ref[kind], preferred_element_type=F32).astype(o_ref.dtype)


def _compress(za, zc, p, batch, seq):
    nchunk = seq // NSA_CMP_STRIDE
    d = NSA_HEAD_DIM
    full = lambda shape: pl.BlockSpec(shape, lambda b, h: (0,) * len(shape))
    return pl.pallas_call(
        _compress_kernel,
        grid=(batch, NSA_KV_HEADS),
        in_specs=[
            pl.BlockSpec((None, seq, d), lambda b, h: (b, 0, ODD_A_KC + h)),
            pl.BlockSpec((None, seq, d), lambda b, h: (b, 0, h)),
            full((2, NSA_CMP_BLOCK, d, d)), full((2, NSA_CMP_BLOCK, d)), full((2, 1, d)), full((2, d, d)),
        ],
        out_specs=pl.BlockSpec((None, 2, None, nchunk, d), lambda b, h: (b, 0, h, 0, 0)),
        out_shape=jax.ShapeDtypeStruct((batch, 2, NSA_KV_HEADS, nchunk, d), BF16),
        scratch_shapes=[pltpu.VMEM((seq, d), F32)],
        compiler_params=_params("parallel", "arbitrary"),
        name="compress",
    )(za, zc, p["cmp_w1"], p["cmp_pos"], p["cmp_b1"], p["cmp_w2"])


def _nt_dot(a, b):
    return lax.dot_general(a, b, (((1,), (1,)), ((), ())), preferred_element_type=F32)


def _tn_dot(a, b):
    return lax.dot_general(a, b, (((0,), (0,)), ((), ())), preferred_element_type=F32)


def _nsa_kernel(q_ref, kc_ref, vc_ref, ks_ref, vs_ref, kw_ref, vw_ref, gl_ref, o_ref, acc_ref, accw_ref, *, ncmp, nslc):
    i = pl.program_id(2)
    tq, tk, grp, d = NSA_Q_TILE, NSA_K_TILE, NSA_GROUP, NSA_HEAD_DIM
    rows = grp * tq
    scale = d ** -0.5
    t0 = i * tq
    q = jnp.concatenate([q_ref[:, g * d:(g + 1) * d] for g in range(grp)], axis=0)
    t_q = t0 + lax.broadcasted_iota(jnp.int32, (1, tq), 1)
    t_all = jnp.concatenate([t_q] * grp, axis=1)

    s = _nt_dot(kc_ref[...], q) * scale
    cidx = lax.broadcasted_iota(jnp.int32, (ncmp, 1), 0)
    vis = (cidx * NSA_CMP_STRIDE + (NSA_CMP_BLOCK - 1)) <= t_all
    s = jnp.where(vis, s, NEG_BIG)
    mx = jnp.max(s, axis=0, keepdims=True)
    e = jnp.where(vis, jnp.exp(s - mx), 0.0)
    p_cmp = e / jnp.maximum(jnp.sum(e, axis=0, keepdims=True), 1.0)
    o_cmp = _tn_dot(vc_ref[...], p_cmp.astype(BF16))

    p_sum = p_cmp[:, 0:tq]
    for g in range(1, grp):
        p_sum = p_sum + p_cmp[:, g * tq:(g + 1) * tq]
    sj = lax.broadcasted_iota(jnp.int32, (nslc, ncmp), 0) * NSA_SLC_BLOCK
    ci = lax.broadcasted_iota(jnp.int32, (nslc, ncmp), 1) * NSA_CMP_STRIDE
    overlap = jnp.where((ci < sj + NSA_SLC_BLOCK) & (ci + NSA_CMP_BLOCK > sj), 1.0, 0.0).astype(BF16)
    p_hi = p_sum.astype(BF16)
    p_lo = (p_sum - p_hi.astype(F32)).astype(BF16)
    imp = (jnp.dot(overlap, p_hi, preferred_element_type=F32)
           + jnp.dot(overlap, p_lo, preferred_element_type=F32))

    blk = lax.broadcasted_iota(jnp.int32, (nslc, tq), 0)
    cur = t_q // NSA_SLC_BLOCK
    forced = (blk == 0) | (blk == cur) | (blk == cur - 1)
    score = jnp.where(blk > cur, -jnp.inf, jnp.where(forced, jnp.inf, imp))
    sel = jnp.zeros((nslc, tq), F32)
    for _ in range(NSA_TOP_N):
        best = jnp.max(score, axis=0, keepdims=True)
        pick = jnp.min(jnp.where(score == best, blk, nslc), axis=0, keepdims=True)
        hit = blk == pick
        sel = jnp.where(hit, 1.0, sel)
        score = jnp.where(hit, -jnp.inf, score)
    sel = sel.astype(BF16)

    krow = lax.broadcasted_iota(jnp.int32, (tk, 1), 0)
    bcol = lax.broadcasted_iota(jnp.int32, (tk, nslc), 1)

    def slc_mask(k0):
        kpos = k0 + krow
        expand = jnp.where((kpos // NSA_SLC_BLOCK) == bcol, 1.0, 0.0).astype(BF16)
        chosen = jnp.dot(expand, sel, preferred_element_type=F32) > 0.5
        return chosen & (kpos <= t_q)

    def win_mask(k0):
        kpos = k0 + krow
        return (kpos <= t_q) & (kpos > t_q - NSA_WINDOW)

    q_log2 = (q.astype(F32) * (scale * LOG2_E)).astype(BF16)

    def tile_scores(k_ref, mask_fn, k0):
        bias_q = jnp.where(mask_fn(k0), 0.0, NEG_BIG)
        bias = jnp.concatenate([bias_q] * grp, axis=1)
        return _nt_dot(k_ref[pl.ds(k0, tk), :], q_log2) + bias

    def tile_update(sc, carry, v_ref, acc, k0):
        m_old, l_old = carry
        m_new = jnp.maximum(m_old, jnp.max(sc, axis=0, keepdims=True))
        alpha = jnp.exp2(m_old - m_new)
        pr = jnp.exp2(sc - m_new)
        l_new = alpha * l_old + jnp.sum(pr, axis=0, keepdims=True)
        acc[...] = alpha * acc[...] + _tn_dot(v_ref[pl.ds(k0, tk), :], pr.astype(BF16))
        return m_new, l_new

    def slc_body(jt, carry):
        k0 = pl.multiple_of(jt * tk, tk)
        return tile_update(tile_scores(ks_ref, slc_mask, k0), carry, vs_ref, acc_ref, k0)

    def both_body(jt, carry):
        k0 = pl.multiple_of(jt * tk, tk)
        sc_s = tile_scores(ks_ref, slc_mask, k0)
        sc_w = tile_scores(kw_ref, win_mask, k0)
        return (tile_update(sc_s, carry[0], vs_ref, acc_ref, k0),
                tile_update(sc_w, carry[1], vw_ref, accw_ref, k0))

    acc_ref[...] = jnp.zeros_like(acc_ref)
    accw_ref[...] = jnp.zeros_like(accw_ref)
    init = (jnp.full((1, rows), NEG_BIG, F32), jnp.zeros((1, rows), F32))
    hi = (t0 + tq) // tk
    win_lo = jnp.maximum(t0 - NSA_WINDOW, 0) // tk
    state_s = lax.fori_loop(0, win_lo, slc_body, init)
    (_, l_s), (_, l_w) = lax.fori_loop(win_lo, hi, both_body, (state_s, init))
    o_slc = acc_ref[...] / l_s
    o_win = accw_ref[...] / l_w

    gates = jax.nn.sigmoid(gl_ref[...].astype(F32)).T
    for g in range(grp):
        cs = slice(g * tq, (g + 1) * tq)
        o_g = (gates[3 * g:3 * g + 1, :] * o_cmp[:, cs]
               + gates[3 * g + 1:3 * g + 2, :] * o_slc[:, cs]
               + gates[3 * g + 2:3 * g + 3, :] * o_win[:, cs])
        o_ref[:, g * d:(g + 1) * d] = o_g.T.astype(o_ref.dtype)


def _nsa(za, zq, zc, kvc, batch, seq):
    tq, d, grp = NSA_Q_TILE, NSA_HEAD_DIM, NSA_GROUP
    ntq = seq // tq
    ncmp = kvc.shape[3]
    kern = functools.partial(_nsa_kernel, ncmp=ncmp, nslc=seq // NSA_SLC_BLOCK)
    qw = grp * d

    def seq_spec(col0):
        return pl.BlockSpec((None, seq, d), lambda b, h, i: (b, 0, col0 + h))

    def cmp_spec(c):
        return pl.BlockSpec((None, None, None, ncmp, d), lambda b, h, i: (b, c, h, 0, 0))

    return pl.pallas_call(
        kern,
        grid=(batch, NSA_KV_HEADS, ntq),
        in_specs=[
            pl.BlockSpec((None, tq, qw), lambda b, h, i: (b, i, h)),
            cmp_spec(0), cmp_spec(1),
            seq_spec(ODD_A_KS), seq_spec(ODD_C_VS), seq_spec(ODD_A_KW), seq_spec(ODD_C_VW),
            pl.BlockSpec((None, tq, LANES), lambda b, h, i: (b, i, ODD_C_GL + h)),
        ],
        out_specs=pl.BlockSpec((tq, qw), lambda b, h, i: (b * ntq + i, h)),
        out_shape=jax.ShapeDtypeStruct((batch * seq, NSA_HEADS * d), BF16),
        scratch_shapes=[pltpu.VMEM((d, grp * tq), F32), pltpu.VMEM((d, grp * tq), F32)],
        compiler_params=_params("parallel", "parallel", "arbitrary"),
        name="nsa",
    )(zq, kvc, kvc, za, zc, za, zc, zc)


def _odd_params(w_in, a_re, a_im, log_dt, b_re, b_im, c_re, c_im, d_skip, glu_w, glu_b,
                cmp_pos, cmp_w1, cmp_b1, cmp_w2, seq):
    kvw = NSA_KV_WIDTH
    qw = NSA_HEADS * NSA_HEAD_DIM
    o_q = S5_WIDTH
    o_kv = o_q + qw
    parts = {name: w_in[:, o_kv + n * kvw:o_kv + (n + 1) * kvw]
             for n, name in enumerate(("kc", "vc", "ks", "vs", "kw", "vw"))}
    w_gl = w_in[:, o_kv + 6 * kvw:].reshape(D_MODEL, NSA_KV_HEADS, 3 * NSA_GROUP)
    w_gl = jnp.pad(w_gl, ((0, 0), (0, 0), (0, LANES - 3 * NSA_GROUP))).reshape(D_MODEL, NSA_KV_HEADS * LANES)
    w_a = jnp.concatenate([w_in[:, :o_q], parts["kc"], parts["ks"], parts["kw"]], axis=1).astype(BF16)
    w_q = w_in[:, o_q:o_kv].astype(BF16)
    w_c = jnp.concatenate([parts["vc"], parts["vs"], parts["vw"], w_gl], axis=1).astype(BF16)
    p = _s5_params(a_re, a_im, log_dt, b_re, b_im, c_re, c_im, d_skip, glu_w, glu_b)
    p.update({
        "w_a": w_a, "w_q": w_q, "w_c": w_c,
        "cmp_w1": cmp_w1.astype(BF16),
        "cmp_pos": cmp_pos.astype(F32),
        "cmp_b1": cmp_b1.astype(F32).reshape(2, 1, NSA_HEAD_DIM),
        "cmp_w2": cmp_w2.astype(BF16),
        "tables": _rope_tables(seq, NSA_HEAD_DIM),
    })
    return p


def _odd_mixer(x, h, g_post, p, w_out, layer, batch, seq):
    tables = p["tables"]
    za = _proj(h, p["w_a"], tables, seq, (ODD_A_KC * LANES, ODD_A_END * LANES))
    zq = _proj(h, p["w_q"], tables, seq, (0, NSA_HEADS * NSA_HEAD_DIM))
    zc = _proj(h, p["w_c"], tables, seq, (0, 0))
    s5_out = _s5(za, p, batch, seq)
    kvc = _compress(za, zc, p, batch, seq)
    o = _nsa(za, zq, zc, kvc, batch, seq)
    return _outproj(s5_out, o, w_out, layer, x, g_post)


def kernel(x, norm_gains, ffn1_w_gate, ffn1_w_up, ffn1_w_down, ffn2_w_gate, ffn2_w_up, ffn2_w_down, ev_w_in, ev_w_out, pool_w, pool_scale, swa_sinks, od_w_in, od_w_out, s5_a_re, s5_a_im, s5_log_dt, s5_b_re, s5_b_im, s5_c_re, s5_c_im, s5_d, s5_glu_w, s5_glu_b, nsa_cmp_pos, nsa_cmp_w1, nsa_cmp_b1, nsa_cmp_w2):
    batch, seq, _ = x.shape
    m = batch * seq
    depth = norm_gains.shape[0]
    xs = x.reshape(m, D_MODEL)
    even_tables = _rope_tables(seq, SWA_HEAD_DIM)
    ffn_w = [(_cast_bf16(wg, cols_out=D_FF_PAD), _cast_bf16(wu, cols_out=D_FF_PAD), _cast_bf16(wd, rows_out=D_FF_PAD))
             for wg, wu, wd in ((ffn1_w_gate, ffn1_w_up, ffn1_w_down), (ffn2_w_gate, ffn2_w_up, ffn2_w_down))]
    ev_out, od_out = _cast_bf16(ev_w_out), _cast_bf16(od_w_out)
    for layer in range(depth):
        g = norm_gains[layer].astype(F32).reshape(6, 1, D_MODEL)
        i = layer // 2
        xs, h = _ffn(xs, g[0], *ffn_w[0], g[1], layer, g_next=g[2])
        if layer % 2 == 0:
            xs = _even_mixer(xs, h, g[3], *_even_weights(ev_w_in[i]), ev_out, i,
                             pool_w[i].astype(BF16), pool_scale[i].astype(F32).reshape(1, POOL_WIDTH),
                             swa_sinks[i].astype(F32), even_tables, batch, seq)
        else:
            p = _odd_params(od_w_in[i], s5_a_re[i], s5_a_im[i], s5_log_dt[i], s5_b_re[i], s5_b_im[i],
                            s5_c_re[i], s5_c_im[i], s5_d[i], s5_glu_w[i], s5_glu_b[i], nsa_cmp_pos[i],
                            nsa_cmp_w1[i], nsa_cmp_b1[i], nsa_cmp_w2[i], seq)
            xs = _odd_mixer(xs, h, g[3], p, od_out, i, batch, seq)
        xs = _ffn(xs, g[4], *ffn_w[1], g[5], layer)
    return xs.reshape(batch, seq, D_MODEL)
```

```python
import functools
import math

import jax
import jax.numpy as jnp
from jax import lax
from jax.experimental import pallas as pl
from jax.experimental.pallas import tpu as pltpu

F32 = jnp.float32
BF16 = jnp.bfloat16

D_MODEL = 4096
D_FF = 5504
NORM_EPS = 1e-6
ROPE_THETA = 500000.0
ROPE_FRACTION = 4
POOL_WINDOWS = (2, 4, 8, 16)
POOL_WIDTH = D_MODEL // 2
POOL_GROUP = POOL_WIDTH // len(POOL_WINDOWS)
POOL_HALO = 16
SWA_HEAD_DIM = 64
SWA_HEADS = 32
SWA_KV_HEADS = 4
SWA_GROUP = SWA_HEADS // SWA_KV_HEADS
SWA_WINDOW = 128
ATTN_BLOCK = 128
S5_WIDTH = D_MODEL // 4
S5_GROUP_CH = 16
S5_STATE = 64
NSA_HEAD_DIM = 128
NSA_HEADS = 24
NSA_KV_HEADS = 6
NSA_GROUP = 4
NSA_CMP_BLOCK = 32
NSA_CMP_STRIDE = 16
NSA_SLC_BLOCK = 64
NSA_TOP_N = 8
NSA_WINDOW = 512
NSA_KV_WIDTH = NSA_KV_HEADS * NSA_HEAD_DIM

LANES = 128
SUBLANES = 8
MXU_DIM = 256
VMEM_LIMIT_BYTES = 56 * 1024 * 1024

ROW_TILE = 512
FF_TILE = 256
D_FF_PAD = -(-D_FF // FF_TILE) * FF_TILE
PROJ_ROW_TILE = 512
OUT_ROW_TILE = 256
OUT_COL_CHUNK = 512
OUT_VMEM_LIMIT_BYTES = 60 * 1024 * 1024
FFN_NEXT_VMEM_LIMIT_BYTES = 60 * 1024 * 1024
NORM_ROWS = 32
NORM_UNROLL = 2
CAST_ROWS = 256
REGROUP_TILE = 256
REGROUP_ROWS = 1024
NEG_BIG = -1e30


def _params(*sem, vmem=VMEM_LIMIT_BYTES):
    return pltpu.CompilerParams(dimension_semantics=sem, vmem_limit_bytes=vmem)


def _rms_scaled(v, g):
    ms = jnp.mean(v * v, axis=-1, keepdims=True)
    return v * lax.rsqrt(ms + NORM_EPS) * g


def _norm_chunk(c):
    return pl.ds(pl.multiple_of(c * NORM_ROWS, NORM_ROWS), NORM_ROWS)


def _norm_rows(dst_ref, src_ref, g_ref, rows):
    g = g_ref[...]

    def body(c, carry):
        rs = _norm_chunk(c)
        dst_ref[rs, :] = _rms_scaled(src_ref[rs, :].astype(F32), g).astype(dst_ref.dtype)
        return carry

    lax.fori_loop(0, rows // NORM_ROWS, body, 0, unroll=NORM_UNROLL)


def _norm_residual_rows(acc_ref, g_ref, res_ref, res_scale, inv_ref, rows, next_ref=None, gnext_ref=None):
    g = g_ref[...]
    g_next = None if next_ref is None else gnext_ref[...]
    reps = acc_ref.shape[1] // LANES

    def stats(c, carry):
        rs = _norm_chunk(c)
        v = acc_ref[rs, :]
        ms = jnp.mean(v * v, axis=-1, keepdims=True)
        inv_ref[rs, :] = jnp.broadcast_to(lax.rsqrt(ms + NORM_EPS), (NORM_ROWS, LANES))
        return carry

    def scale(c, carry):
        rs = _norm_chunk(c)
        y = res_ref[rs, :] + res_scale * (acc_ref[rs, :] * jnp.tile(inv_ref[rs, :], (1, reps)) * g)
        acc_ref[rs, :] = y
        if next_ref is not None:
            next_ref[rs, :] = _rms_scaled(y, g_next).astype(next_ref.dtype)
        return carry

    lax.fori_loop(0, rows // NORM_ROWS, stats, 0, unroll=NORM_UNROLL)
    lax.fori_loop(0, rows // NORM_ROWS, scale, 0, unroll=NORM_UNROLL)


def _cast_kernel(x_ref, o_ref, *, tr, rows_in, cols_in):
    x = x_ref[...]
    if rows_in % tr:
        r = pl.program_id(1) * tr + lax.broadcasted_iota(jnp.int32, (tr, 1), 0)
        x = jnp.where(r < rows_in, x, 0.0)
    if cols_in == o_ref.shape[1]:
        o_ref[...] = x.astype(o_ref.dtype)
    else:
        o_ref[:, :cols_in] = x.astype(o_ref.dtype)
        o_ref[:, cols_in:] = jnp.zeros((tr, o_ref.shape[1] - cols_in), o_ref.dtype)


def _cast_bf16(w, rows_out=None, cols_out=None):
    nl, rows_in, cols_in = w.shape
    rows_out = rows_out or rows_in
    cols_out = cols_out or cols_in
    tr = min(CAST_ROWS, rows_out)
    kern = functools.partial(_cast_kernel, tr=tr, rows_in=rows_in, cols_in=cols_in)
    return pl.pallas_call(
        kern,
        grid=(nl, rows_out // tr),
        in_specs=[pl.BlockSpec((None, tr, cols_in), lambda l, i: (l, i, 0))],
        out_specs=pl.BlockSpec((None, tr, cols_out), lambda l, i: (l, i, 0)),
        out_shape=jax.ShapeDtypeStruct((nl, rows_out, cols_out), BF16),
        compiler_params=_params("parallel", "parallel"),
        name="cast",
    )(w)


def _regroup_kernel(w_ref, *rest, n_main):
    o_ref = rest[-1]
    if len(rest) == 1:
        o_ref[...] = w_ref[...].astype(o_ref.dtype)
        return
    t = pl.program_id(1)

    @pl.when(t < n_main)
    def _():
        o_ref[...] = w_ref[...].astype(o_ref.dtype)

    @pl.when(t >= n_main)
    def _():
        o_ref[...] = rest[0][...].astype(o_ref.dtype)


def _regroup_cast(w, layer, src_cols, extra=None):
    rows = w.shape[1]
    tile, tr = REGROUP_TILE, min(REGROUP_ROWS, rows)
    deltas = [c // tile - t for t, c in enumerate(src_cols)]
    n_main = len(deltas)
    n_extra = 0 if extra is None else extra.shape[1] // tile

    def src_tile(t):
        d = deltas[0]
        for k in range(1, n_main):
            if deltas[k] != deltas[k - 1]:
                d = jnp.where(t >= k, deltas[k], d)
        return jnp.minimum(t, n_main - 1) + d

    in_specs = [pl.BlockSpec((None, tr, tile), lambda i, t: (layer, i, src_tile(t)))]
    args = [w]
    if extra is not None:
        in_specs.append(pl.BlockSpec((tr, tile), lambda i, t: (i, jnp.maximum(t - n_main, 0))))
        args.append(extra)
    return pl.pallas_call(
        functools.partial(_regroup_kernel, n_main=n_main),
        grid=(rows // tr, n_main + n_extra),
        in_specs=in_specs,
        out_specs=pl.BlockSpec((tr, tile), lambda i, t: (i, t)),
        out_shape=jax.ShapeDtypeStruct((rows, (n_main + n_extra) * tile), BF16),
        compiler_params=_params("parallel", "arbitrary"),
        name="regroup",
    )(*args)


def _col_tiles(start, width):
    return list(range(start, start + width, REGROUP_TILE))


def _ffn_kernel(x_ref, gpre_ref, wg_ref, wu_ref, wd_ref, gpost_ref, *rest, tm, nj):
    gnext_ref = rest[0] if len(rest) == 4 else None
    o_ref, h_ref, inv_ref = rest[-3:]
    j = pl.program_id(1)

    @pl.when(j == 0)
    def _():
        _norm_rows(h_ref, x_ref, gpre_ref, tm)
        o_ref[...] = jnp.zeros_like(o_ref)

    h = h_ref[...]
    gate = jnp.dot(h, wg_ref[...], preferred_element_type=F32)
    up = jnp.dot(h, wu_ref[...], preferred_element_type=F32)
    act = (gate * jax.nn.sigmoid(gate) * up).astype(BF16)
    nc = 512
    for n in range(D_MODEL // nc):
        sl = slice(n * nc, (n + 1) * nc)
        o_ref[:, sl] += jnp.dot(act, wd_ref[:, sl], preferred_element_type=F32)

    @pl.when(j == nj - 1)
    def _():
        _norm_residual_rows(o_ref, gpost_ref, x_ref, 0.5, inv_ref, tm,
                            next_ref=None if gnext_ref is None else h_ref, gnext_ref=gnext_ref)


def _ffn(x, g_pre, wg, wu, wd, g_post, layer, g_next=None):
    m = x.shape[0]
    tm = min(ROW_TILE, m)
    nj = D_FF_PAD // FF_TILE
    kern = functools.partial(_ffn_kernel, tm=tm, nj=nj)
    row_spec = pl.BlockSpec((tm, D_MODEL), lambda i, j: (i, 0))
    gain_spec = pl.BlockSpec((1, D_MODEL), lambda i, j: (0, 0))
    in_specs = [
        row_spec, gain_spec,
        pl.BlockSpec((None, D_MODEL, FF_TILE), lambda i, j: (layer, 0, j)),
        pl.BlockSpec((None, D_MODEL, FF_TILE), lambda i, j: (layer, 0, j)),
        pl.BlockSpec((None, FF_TILE, D_MODEL), lambda i, j: (layer, j, 0)),
        gain_spec,
    ]
    x_shape = jax.ShapeDtypeStruct((m, D_MODEL), F32)
    inv_scratch = pltpu.VMEM((tm, LANES), F32)
    if g_next is None:
        return pl.pallas_call(
            kern, grid=(m // tm, nj), in_specs=in_specs, out_specs=row_spec, out_shape=x_shape,
            scratch_shapes=[pltpu.VMEM((tm, D_MODEL), BF16), inv_scratch],
            compiler_params=_params("parallel", "arbitrary"), name="ffn",
        )(x, g_pre, wg, wu, wd, g_post)
    return pl.pallas_call(
        kern, grid=(m // tm, nj), in_specs=in_specs + [gain_spec], out_specs=(row_spec, row_spec),
        out_shape=(x_shape, jax.ShapeDtypeStruct((m, D_MODEL), BF16)), scratch_shapes=[inv_scratch],
        compiler_params=_params("parallel", "arbitrary", vmem=FFN_NEXT_VMEM_LIMIT_BYTES), name="ffn_next",
    )(x, g_pre, wg, wu, wd, g_post, g_next)


def _proj_kernel(h_ref, w_ref, cos_ref, sa_ref, sb_ref, o_ref, *, rope_chunks, half):
    h = h_ref[...]
    tc = MXU_DIM
    for c in range(o_ref.shape[1] // tc):
        cs = slice(c * tc, (c + 1) * tc)
        z = jnp.dot(h, w_ref[:, cs], preferred_element_type=F32)
        if c in rope_chunks:
            parts = []
            for p in range(tc // LANES):
                zp = z[:, p * LANES:(p + 1) * LANES]
                parts.append(zp * cos_ref[...]
                             + pltpu.roll(zp, LANES - half, 1) * sa_ref[...]
                             + pltpu.roll(zp, half, 1) * sb_ref[...])
            z = jnp.concatenate(parts, axis=1)
        o_ref[:, cs] = z.astype(o_ref.dtype)


def _rope_tables(seq, head_dim):
    rot = head_dim // ROPE_FRACTION
    half = rot // 2
    inv_freq = jnp.power(ROPE_THETA, -jnp.arange(half, dtype=F32) * 2.0 / rot)
    ang = jnp.arange(seq, dtype=jnp.int32).astype(F32)[:, None] * inv_freq[None, :]
    cos, sin = jnp.cos(ang), jnp.sin(ang)
    one = jnp.ones((seq, head_dim - rot), F32)
    zero_h = jnp.zeros((seq, half), F32)
    zero_r = jnp.zeros((seq, head_dim - rot), F32)
    c_head = jnp.concatenate([cos, cos, one], axis=1)
    sa_head = jnp.concatenate([-sin, zero_h, zero_r], axis=1)
    sb_head = jnp.concatenate([zero_h, sin, zero_r], axis=1)
    reps = LANES // head_dim
    return jnp.tile(c_head, (1, reps)), jnp.tile(sa_head, (1, reps)), jnp.tile(sb_head, (1, reps)), half


def _proj(h, w, tables, seq, rope_cols):
    cos_t, sa_t, sb_t, half = tables
    m = h.shape[0]
    gw = w.shape[1]
    tm = min(PROJ_ROW_TILE, seq)
    sblocks = seq // tm
    rope_chunks = frozenset(range(rope_cols[0] // MXU_DIM, rope_cols[1] // MXU_DIM))
    kern = functools.partial(_proj_kernel, rope_chunks=rope_chunks, half=half)
    tab_spec = pl.BlockSpec((tm, LANES), lambda i: (i % sblocks, 0))
    return pl.pallas_call(
        kern,
        grid=(m // tm,),
        in_specs=[
            pl.BlockSpec((tm, D_MODEL), lambda i: (i, 0)),
            pl.BlockSpec((D_MODEL, gw), lambda i: (0, 0), pipeline_mode=pl.Buffered(1)),
            tab_spec, tab_spec, tab_spec,
        ],
        out_specs=pl.BlockSpec((None, tm, gw), lambda i: (i // sblocks, i % sblocks, 0)),
        out_shape=jax.ShapeDtypeStruct((m // seq, seq, gw), BF16),
        compiler_params=_params("parallel"),
        name="proj",
    )(h, w, cos_t, sa_t, sb_t)


def _outproj_kernel(a1_ref, a2_ref, w_ref, g_ref, x_ref, o_ref, inv_ref, *, tm, k1):
    a1 = a1_ref[...]
    a2 = a2_ref[...]
    nc = OUT_COL_CHUNK
    for n in range(D_MODEL // nc):
        sl = slice(n * nc, (n + 1) * nc)
        o_ref[:, sl] = (jnp.dot(a1, w_ref[0:k1, sl], preferred_element_type=F32)
                        + jnp.dot(a2, w_ref[k1:, sl], preferred_element_type=F32))
    _norm_residual_rows(o_ref, g_ref, x_ref, 1.0, inv_ref, tm)


def _outproj(a1, a2, w, layer, x, g_post):
    m = x.shape[0]
    k1, k2 = a1.shape[-1], a2.shape[-1]
    if a1.ndim == 3:
        tm = min(OUT_ROW_TILE, a1.shape[1])
        sb = a1.shape[1] // tm
        a1_spec = pl.BlockSpec((None, tm, k1), lambda i: (i // sb, i % sb, 0))
    else:
        tm = min(OUT_ROW_TILE, m)
        a1_spec = pl.BlockSpec((tm, k1), lambda i: (i, 0))
    kern = functools.partial(_outproj_kernel, tm=tm, k1=k1)
    return pl.pallas_call(
        kern,
        grid=(m // tm,),
        in_specs=[
            a1_spec,
            pl.BlockSpec((tm, k2), lambda i: (i, 0)),
            pl.BlockSpec((None, k1 + k2, D_MODEL), lambda i: (layer, 0, 0), pipeline_mode=pl.Buffered(1)),
            pl.BlockSpec((1, D_MODEL), lambda i: (0, 0)),
            pl.BlockSpec((tm, D_MODEL), lambda i: (i, 0)),
        ],
        out_specs=pl.BlockSpec((tm, D_MODEL), lambda i: (i, 0)),
        out_shape=jax.ShapeDtypeStruct((m, D_MODEL), F32),
        scratch_shapes=[pltpu.VMEM((tm, LANES), F32)],
        compiler_params=_params("parallel", vmem=OUT_VMEM_LIMIT_BYTES),
        name="outproj",
    )(a1, a2, w, g_post, x)


def _swa_kernel(sink_ref, q_ref, kvp_ref, kvc_ref, o_ref):
    n = pl.program_id(1)
    blk = ATTN_BLOCK
    kvw = SWA_KV_HEADS * SWA_HEAD_DIM
    kv = jnp.concatenate([kvp_ref[...], kvc_ref[...]], axis=0)
    krow = lax.broadcasted_iota(jnp.int32, (2 * blk, 1), 0)
    qcol = lax.broadcasted_iota(jnp.int32, (1, blk), 1)
    diff = qcol - krow + blk
    vis = (diff >= 0) & (diff < SWA_WINDOW) & ((krow >= blk) | (n > 0))
    bias = jnp.where(vis, 0.0, NEG_BIG)
    scale = SWA_HEAD_DIM ** -0.5 * LOG2_E
    pair = LANES // SWA_HEAD_DIM
    for kh in range(SWA_KV_HEADS):
        k_h = kv[:, kh * SWA_HEAD_DIM:(kh + 1) * SWA_HEAD_DIM]
        v_h = kv[:, kvw + kh * SWA_HEAD_DIM:kvw + (kh + 1) * SWA_HEAD_DIM]
        heads = range(kh * SWA_GROUP, (kh + 1) * SWA_GROUP)
        scores = [_nt_dot(k_h, q_ref[:, h * SWA_HEAD_DIM:(h + 1) * SWA_HEAD_DIM]) for h in heads]
        weights, denoms = [], []
        for h, s in zip(heads, scores):
            s = s * scale + bias
            sk = sink_ref[h] * LOG2_E
            mx = jnp.maximum(jnp.max(s, axis=0, keepdims=True), sk)
            e = jnp.exp2(s - mx)
            denoms.append(jnp.sum(e, axis=0, keepdims=True) + jnp.exp2(sk - mx))
            weights.append(e.astype(BF16))
        outs = [_tn_dot(v_h, e) / d for e, d in zip(weights, denoms)]
        for g0 in range(0, SWA_GROUP, pair):
            h0 = kh * SWA_GROUP + g0
            o_ref[:, h0 * SWA_HEAD_DIM:(h0 + pair) * SWA_HEAD_DIM] = (
                jnp.concatenate(outs[g0:g0 + pair], axis=0).T.astype(o_ref.dtype))


def _swa(zq, za, sinks, batch, seq):
    blk = ATTN_BLOCK
    nblk = seq // blk
    qw = SWA_HEADS * SWA_HEAD_DIM
    kvw2 = 2 * SWA_KV_HEADS * SWA_HEAD_DIM
    kv_blk = POOL_WIDTH // kvw2
    return pl.pallas_call(
        _swa_kernel,
        grid=(batch, nblk),
        in_specs=[
            pl.BlockSpec(memory_space=pltpu.SMEM),
            pl.BlockSpec((None, blk, qw), lambda b, n: (b, n, 0)),
            pl.BlockSpec((None, blk, kvw2), lambda b, n: (b, jnp.maximum(n - 1, 0), kv_blk)),
            pl.BlockSpec((None, blk, kvw2), lambda b, n: (b, n, kv_blk)),
        ],
        out_specs=pl.BlockSpec((blk, qw), lambda b, n: (b * nblk + n, 0)),
        out_shape=jax.ShapeDtypeStruct((batch * seq, qw), BF16),
        compiler_params=_params("parallel", "arbitrary"),
        name="swa",
    )(sinks, zq, za, za)


def _pool_kernel(up_ref, uc_ref, w_ref, scale_ref, o_ref, *, ts):
    i = pl.program_id(1)
    has_prev = (i > 0).astype(F32)
    pos = i * ts + lax.broadcasted_iota(jnp.int32, (ts, 1), 0)
    for gi, win in enumerate(POOL_WINDOWS):
        sl = slice(gi * POOL_GROUP, (gi + 1) * POOL_GROUP)
        cur = uc_ref[:, sl].astype(F32)
        prev = up_ref[:, sl].astype(F32) * has_prev
        acc = jnp.concatenate([prev, cur], axis=0)
        d = 1
        while d < win:
            acc = acc + pltpu.roll(acc, d, 0)
            d *= 2
        wsum = acc[POOL_HALO:, :]
        count = jnp.minimum(pos + 1, win).astype(F32)
        zz = (wsum / count - cur).astype(BF16)
        a = jnp.dot(zz, w_ref[gi], preferred_element_type=F32) * scale_ref[:, sl]
        o_ref[:, sl] = a.astype(o_ref.dtype)


def _pool(za, w_pool, pool_scale, batch, seq):
    ts = min(256, seq)
    nts = seq // ts
    halo_blocks = ts // POOL_HALO
    kern = functools.partial(_pool_kernel, ts=ts)
    return pl.pallas_call(
        kern,
        grid=(batch, nts),
        in_specs=[
            pl.BlockSpec((None, POOL_HALO, POOL_WIDTH), lambda b, i: (b, jnp.maximum(i * halo_blocks - 1, 0), 0)),
            pl.BlockSpec((None, ts, POOL_WIDTH), lambda b, i: (b, i, 0)),
            pl.BlockSpec((len(POOL_WINDOWS), POOL_GROUP, POOL_GROUP), lambda b, i: (0, 0, 0)),
            pl.BlockSpec((1, POOL_WIDTH), lambda b, i: (0, 0)),
        ],
        out_specs=pl.BlockSpec((ts, POOL_WIDTH), lambda b, i: (b * nts + i, 0)),
        out_shape=jax.ShapeDtypeStruct((batch * seq, POOL_WIDTH), BF16),
        compiler_params=_params("parallel", "arbitrary"),
        name="pool",
    )(za, za, w_pool, pool_scale)


def _even_weights(w_in, layer):
    qw = SWA_HEADS * SWA_HEAD_DIM
    kvw2 = 2 * SWA_KV_HEADS * SWA_HEAD_DIM
    w_a = _regroup_cast(w_in, layer, _col_tiles(0, POOL_WIDTH) + _col_tiles(POOL_WIDTH + qw, kvw2))
    return w_a, _regroup_cast(w_in, layer, _col_tiles(POOL_WIDTH, qw))


def _even_mixer(x, h, g_post, w_a, w_q, w_out, layer, w_pool, pool_scale, sinks, tables, batch, seq):
    qw = SWA_HEADS * SWA_HEAD_DIM
    k_cols = (POOL_WIDTH, POOL_WIDTH + SWA_KV_HEADS * SWA_HEAD_DIM)
    za = _proj(h, w_a, tables, seq, k_cols)
    zq = _proj(h, w_q, tables, seq, (0, qw))
    o = _swa(zq, za, sinks, batch, seq)
    a = _pool(za, w_pool, pool_scale, batch, seq)
    return _outproj(a, o, w_out, layer, x, g_post)


S5_SLAB = LANES
S5_SLABS = S5_WIDTH // S5_SLAB
S5_SLAB_STATE = (S5_SLAB // S5_GROUP_CH) * S5_STATE
S5_TIME_TILE = 128
S5_SCAN_UNROLL = 8


def _gelu(y):
    return 0.5 * y * (1.0 + jnp.tanh(math.sqrt(2.0 / math.pi) * (y + 0.044715 * (y * y * y))))


def _s5_kernel(u_ref, b_ref, tab_ref, c_ref, d_ref, gw_ref, gb_ref, o_ref, carry_ref, x_ref, y_ref, *, ts):
    i = pl.program_id(0)
    ns = S5_SLAB_STATE
    nb = SUBLANES
    nl = ns // LANES

    @pl.when(i == 0)
    def _():
        carry_ref[...] = jnp.zeros_like(carry_ref)

    u_all = u_ref[...].reshape(nb * ts, S5_WIDTH)
    for j in range(S5_SLABS):
        ch = slice(j * S5_SLAB, (j + 1) * S5_SLAB)
        u_j = u_all[:, ch]
        bu = jnp.dot(u_j, b_ref[j], preferred_element_type=F32)
        for b in range(nb):
            for c in range(2 * nl):
                x_ref[c, pl.ds(b, ts, stride=nb), :] = bu[b * ts:(b + 1) * ts, c * LANES:(c + 1) * LANES]
        a_re, a_im = tab_ref[j, 0], tab_ref[j, 1]

        def step(t, carry):
            c_re, c_im = carry
            r = pl.multiple_of(t * nb, nb)
            v_re = jnp.concatenate([x_ref[c, pl.ds(r, nb), :] for c in range(nl)], axis=1)
            v_im = jnp.concatenate([x_ref[nl + c, pl.ds(r, nb), :] for c in range(nl)], axis=1)
            n_re = v_re + a_re * c_re - a_im * c_im
            n_im = v_im + a_re * c_im + a_im * c_re
            for c in range(nl):
                x_ref[c, pl.ds(r, nb), :] = n_re[:, c * LANES:(c + 1) * LANES]
                x_ref[nl + c, pl.ds(r, nb), :] = n_im[:, c * LANES:(c + 1) * LANES]
            return n_re, n_im

        c_re, c_im = lax.fori_loop(0, ts, step, (carry_ref[j, :, 0:ns], carry_ref[j, :, ns:2 * ns]),
                                   unroll=S5_SCAN_UNROLL)
        carry_ref[j, :, 0:ns] = c_re
        carry_ref[j, :, ns:2 * ns] = c_im
        states = jnp.concatenate(
            [jnp.concatenate([x_ref[c, pl.ds(b, ts, stride=nb), :] for c in range(2 * nl)], axis=1)
             for b in range(nb)], axis=0)
        y_j = jnp.dot(states.astype(BF16), c_ref[j], preferred_element_type=F32)
        y_ref[:, ch] = _gelu(y_j + d_ref[:, ch] * u_j.astype(F32))

    y = y_ref[...]
    gate = jnp.dot(y.astype(BF16), gw_ref[...], preferred_element_type=F32) + gb_ref[...]
    o_ref[...] = (y * jax.nn.sigmoid(gate)).astype(o_ref.dtype).reshape(nb, ts, S5_WIDTH)


def _s5(z, p, batch, seq):
    assert batch == SUBLANES, "the S5 scan lays the batch out on the 8 sublanes"
    ts = min(S5_TIME_TILE, seq)
    kern = functools.partial(_s5_kernel, ts=ts)
    full = lambda shape: pl.BlockSpec(shape, lambda i: (0,) * len(shape))
    return pl.pallas_call(
        kern,
        grid=(seq // ts,),
        in_specs=[
            pl.BlockSpec((batch, ts, S5_WIDTH), lambda i: (0, i, 0)),
            full((S5_SLABS, S5_SLAB, 2 * S5_SLAB_STATE)),
            full((S5_SLABS, 2, SUBLANES, S5_SLAB_STATE)),
            full((S5_SLABS, 2 * S5_SLAB_STATE, S5_SLAB)),
            full((1, S5_WIDTH)),
            full((S5_WIDTH, S5_WIDTH)),
            full((1, S5_WIDTH)),
        ],
        out_specs=pl.BlockSpec((batch, ts, S5_WIDTH), lambda i: (0, i, 0)),
        out_shape=jax.ShapeDtypeStruct((batch, seq, S5_WIDTH), BF16),
        scratch_shapes=[
            pltpu.VMEM((S5_SLABS, SUBLANES, 2 * S5_SLAB_STATE), F32),
            pltpu.VMEM((2 * S5_SLAB_STATE // LANES, batch * ts, LANES), F32),
            pltpu.VMEM((batch * ts, S5_WIDTH), F32),
        ],
        compiler_params=_params("arbitrary"),
        name="s5",
    )(z, p["s5_b"], p["s5_tab"], p["s5_c"], p["s5_d"], p["glu_w"], p["glu_b"])


def _s5_params(a_re, a_im, log_dt, b_re, b_im, c_re, c_im, d_skip, glu_w, glu_b):
    lam_re = a_re.astype(F32)
    lam_im = a_im.astype(F32)
    dt = jnp.exp(log_dt.astype(F32))[:, None]
    decay = jnp.exp(lam_re * dt)
    abar_re = decay * jnp.cos(lam_im * dt)
    abar_im = decay * jnp.sin(lam_im * dt)
    inv_mag = 1.0 / (lam_re * lam_re + lam_im * lam_im)
    num_re = abar_re - 1.0
    f_re = (num_re * lam_re + abar_im * lam_im) * inv_mag
    f_im = (abar_im * lam_re - num_re * lam_im) * inv_mag
    br = b_re.astype(F32)
    bi = b_im.astype(F32)
    bbar_re = f_re[..., None] * br - f_im[..., None] * bi
    bbar_im = f_re[..., None] * bi + f_im[..., None] * br
    gps = S5_SLAB // S5_GROUP_CH
    eye = jnp.eye(gps, dtype=F32)

    def b_slab(t):
        t = t.reshape(S5_SLABS, gps, S5_STATE, S5_GROUP_CH)
        t = jnp.einsum('jgnh,gk->jghkn', t, eye)
        return t.reshape(S5_SLABS, S5_SLAB, S5_SLAB_STATE)

    def c_slab(t):
        t = t.astype(F32).reshape(S5_SLABS, gps, S5_GROUP_CH, S5_STATE)
        t = jnp.einsum('jghn,gk->jgnkh', t, eye)
        return t.reshape(S5_SLABS, S5_SLAB_STATE, S5_SLAB)

    b_mat = jnp.concatenate([b_slab(bbar_re), b_slab(bbar_im)], axis=2).astype(BF16)
    c_mat = jnp.concatenate([c_slab(c_re), -c_slab(c_im)], axis=1).astype(BF16)

    tab = jnp.stack([abar_re.reshape(S5_SLABS, 1, S5_SLAB_STATE), abar_im.reshape(S5_SLABS, 1, S5_SLAB_STATE)], axis=1)
    tab = jnp.broadcast_to(tab, (S5_SLABS, 2, SUBLANES, S5_SLAB_STATE))
    return {
        "s5_b": b_mat, "s5_c": c_mat, "s5_tab": tab,
        "s5_d": d_skip.astype(F32).reshape(1, S5_WIDTH),
        "glu_w": glu_w.astype(BF16), "glu_b": glu_b.astype(F32).reshape(1, S5_WIDTH),
    }


NSA_Q_TILE = 256
NSA_K_TILE = 256
LOG2_E = math.log2(math.e)
ODD_A_KC = S5_WIDTH // LANES
ODD_A_KS = ODD_A_KC + NSA_KV_HEADS
ODD_A_KW = ODD_A_KS + NSA_KV_HEADS
ODD_A_END = ODD_A_KW + NSA_KV_HEADS
ODD_C_VS = NSA_KV_HEADS
ODD_C_VW = 2 * NSA_KV_HEADS
ODD_C_GL = 3 * NSA_KV_HEADS


def _compress_kernel(k_ref, v_ref, w1_ref, pos_ref, b1_ref, w2_ref, o_ref, xs_ref):
    n = o_ref.shape[1]
    st = NSA_CMP_STRIDE
    rows = lax.broadcasted_iota(jnp.int32, (n, NSA_HEAD_DIM), 0)
    for kind, src_ref in enumerate((k_ref, v_ref)):
        xs_ref[...] = src_ref[...].astype(F32)
        p1 = jnp.zeros((n, NSA_HEAD_DIM), F32)
        p2 = jnp.zeros((n, NSA_HEAD_DIM), F32)
        for l in range(st):
            xl = xs_ref[pl.ds(l, n, stride=st), :]
            p1 = p1 + jnp.dot((xl + pos_ref[kind, l:l + 1, :]).astype(BF16), w1_ref[kind, l],
                              preferred_element_type=F32)
            p2 = p2 + jnp.dot((xl + pos_ref[kind, st + l:st + l + 1, :]).astype(BF16), w1_ref[kind, st + l],
                              preferred_element_type=F32)
        p2_next = jnp.where(rows < n - 1, pltpu.roll(p2, n - 1, 0), 0.0)
        hid = _gelu(p1 + p2_next + b1_ref[kind])
        o_ref[kind] = jnp.dot(hid.astype(BF16), w2_ref[kind], preferred_element_type=F32).astype(o_ref.dtype)


def _compress(za, zc, p, batch, seq):
    nchunk = seq // NSA_CMP_STRIDE
    d = NSA_HEAD_DIM
    full = lambda shape: pl.BlockSpec(shape, lambda b, h: (0,) * len(shape))
    return pl.pallas_call(
        _compress_kernel,
        grid=(batch, NSA_KV_HEADS),
        in_specs=[
            pl.BlockSpec((None, seq, d), lambda b, h: (b, 0, ODD_A_KC + h)),
            pl.BlockSpec((None, seq, d), lambda b, h: (b, 0, h)),
            full((2, NSA_CMP_BLOCK, d, d)), full((2, NSA_CMP_BLOCK, d)), full((2, 1, d)), full((2, d, d)),
        ],
        out_specs=pl.BlockSpec((None, 2, None, nchunk, d), lambda b, h: (b, 0, h, 0, 0)),
        out_shape=jax.ShapeDtypeStruct((batch, 2, NSA_KV_HEADS, nchunk, d), BF16),
        scratch_shapes=[pltpu.VMEM((seq, d), F32)],
        compiler_params=_params("parallel", "arbitrary"),
        name="compress",
    )(za, zc, p["cmp_w1"], p["cmp_pos"], p["cmp_b1"], p["cmp_w2"])


def _nt_dot(a, b):
    return lax.dot_general(a, b, (((1,), (1,)), ((), ())), preferred_element_type=F32)


def _tn_dot(a, b):
    return lax.dot_general(a, b, (((0,), (0,)), ((), ())), preferred_element_type=F32)


def _nsa_kernel(q_ref, kc_ref, vc_ref, ks_ref, vs_ref, kw_ref, vw_ref, gl_ref, o_ref, acc_ref, accw_ref, *, ncmp, nslc):
    i = pl.program_id(2)
    tq, tk, grp, d = NSA_Q_TILE, NSA_K_TILE, NSA_GROUP, NSA_HEAD_DIM
    rows = grp * tq
    scale = d ** -0.5
    t0 = i * tq
    q = jnp.concatenate([q_ref[:, g * d:(g + 1) * d] for g in range(grp)], axis=0)
    t_q = t0 + lax.broadcasted_iota(jnp.int32, (1, tq), 1)
    t_all = jnp.concatenate([t_q] * grp, axis=1)

    s = _nt_dot(kc_ref[...], q) * scale
    cidx = lax.broadcasted_iota(jnp.int32, (ncmp, 1), 0)
    vis = (cidx * NSA_CMP_STRIDE + (NSA_CMP_BLOCK - 1)) <= t_all
    s = jnp.where(vis, s, NEG_BIG)
    mx = jnp.max(s, axis=0, keepdims=True)
    e = jnp.where(vis, jnp.exp(s - mx), 0.0)
    p_cmp = e / jnp.maximum(jnp.sum(e, axis=0, keepdims=True), 1.0)
    o_cmp = _tn_dot(vc_ref[...], p_cmp.astype(BF16))

    p_sum = p_cmp[:, 0:tq]
    for g in range(1, grp):
        p_sum = p_sum + p_cmp[:, g * tq:(g + 1) * tq]
    sj = lax.broadcasted_iota(jnp.int32, (nslc, ncmp), 0) * NSA_SLC_BLOCK
    ci = lax.broadcasted_iota(jnp.int32, (nslc, ncmp), 1) * NSA_CMP_STRIDE
    overlap = jnp.where((ci < sj + NSA_SLC_BLOCK) & (ci + NSA_CMP_BLOCK > sj), 1.0, 0.0).astype(BF16)
    p_hi = p_sum.astype(BF16)
    p_lo = (p_sum - p_hi.astype(F32)).astype(BF16)
    imp = (jnp.dot(overlap, p_hi, preferred_element_type=F32)
           + jnp.dot(overlap, p_lo, preferred_element_type=F32))

    blk = lax.broadcasted_iota(jnp.int32, (nslc, tq), 0)
    cur = t_q // NSA_SLC_BLOCK
    forced = (blk == 0) | (blk == cur) | (blk == cur - 1)
    score = jnp.where(blk > cur, -jnp.inf, jnp.where(forced, jnp.inf, imp))
    sel = jnp.zeros((nslc, tq), F32)
    for _ in range(NSA_TOP_N):
        best = jnp.max(score, axis=0, keepdims=True)
        pick = jnp.min(jnp.where(score == best, blk, nslc), axis=0, keepdims=True)
        hit = blk == pick
        sel = jnp.where(hit, 1.0, sel)
        score = jnp.where(hit, -jnp.inf, score)
    sel = sel.astype(BF16)

    krow = lax.broadcasted_iota(jnp.int32, (tk, 1), 0)
    bcol = lax.broadcasted_iota(jnp.int32, (tk, nslc), 1)

    def slc_mask(k0):
        kpos = k0 + krow
        expand = jnp.where((kpos // NSA_SLC_BLOCK) == bcol, 1.0, 0.0).astype(BF16)
        chosen = jnp.dot(expand, sel, preferred_element_type=F32) > 0.5
        return chosen & (kpos <= t_q)

    def win_mask(k0):
        kpos = k0 + krow
        return (kpos <= t_q) & (kpos > t_q - NSA_WINDOW)

    q_log2 = (q.astype(F32) * (scale * LOG2_E)).astype(BF16)

    def tile_scores(k_ref, mask_fn, k0):
        bias_q = jnp.where(mask_fn(k0), 0.0, NEG_BIG)
        bias = jnp.concatenate([bias_q] * grp, axis=1)
        return _nt_dot(k_ref[pl.ds(k0, tk), :], q_log2) + bias

    def tile_update(sc, carry, v_ref, acc, k0):
        m_old, l_old = carry
        m_new = jnp.maximum(m_old, jnp.max(sc, axis=0, keepdims=True))
        alpha = jnp.exp2(m_old - m_new)
        pr = jnp.exp2(sc - m_new)
        l_new = alpha * l_old + jnp.sum(pr, axis=0, keepdims=True)
        acc[...] = alpha * acc[...] + _tn_dot(v_ref[pl.ds(k0, tk), :], pr.astype(BF16))
        return m_new, l_new

    def slc_body(jt, carry):
        k0 = pl.multiple_of(jt * tk, tk)
        return tile_update(tile_scores(ks_ref, slc_mask, k0), carry, vs_ref, acc_ref, k0)

    def both_body(jt, carry):
        k0 = pl.multiple_of(jt * tk, tk)
        sc_s = tile_scores(ks_ref, slc_mask, k0)
        sc_w = tile_scores(kw_ref, win_mask, k0)
        return (tile_update(sc_s, carry[0], vs_ref, acc_ref, k0),
                tile_update(sc_w, carry[1], vw_ref, accw_ref, k0))

    acc_ref[...] = jnp.zeros_like(acc_ref)
    accw_ref[...] = jnp.zeros_like(accw_ref)
    init = (jnp.full((1, rows), NEG_BIG, F32), jnp.zeros((1, rows), F32))
    hi = (t0 + tq) // tk
    win_lo = jnp.maximum(t0 - NSA_WINDOW, 0) // tk
    state_s = lax.fori_loop(0, win_lo, slc_body, init)
    (_, l_s), (_, l_w) = lax.fori_loop(win_lo, hi, both_body, (state_s, init))
    o_slc = acc_ref[...] / l_s
    o_win = accw_ref[...] / l_w

    gates = jax.nn.sigmoid(gl_ref[...].astype(F32)).T
    for g in range(grp):
        cs = slice(g * tq, (g + 1) * tq)
        o_g = (gates[3 * g:3 * g + 1, :] * o_cmp[:, cs]
               + gates[3 * g + 1:3 * g + 2, :] * o_slc[:, cs]
               + gates[3 * g + 2:3 * g + 3, :] * o_win[:, cs])
        o_ref[:, g * d:(g + 1) * d] = o_g.T.astype(o_ref.dtype)


def _nsa(za, zq, zc, kvc, batch, seq):
    tq, d, grp = NSA_Q_TILE, NSA_HEAD_DIM, NSA_GROUP
    ntq = seq // tq
    ncmp = kvc.shape[3]
    kern = functools.partial(_nsa_kernel, ncmp=ncmp, nslc=seq // NSA_SLC_BLOCK)
    qw = grp * d

    def seq_spec(col0):
        return pl.BlockSpec((None, seq, d), lambda b, h, i: (b, 0, col0 + h))

    def cmp_spec(c):
        return pl.BlockSpec((None, None, None, ncmp, d), lambda b, h, i: (b, c, h, 0, 0))

    return pl.pallas_call(
        kern,
        grid=(batch, NSA_KV_HEADS, ntq),
        in_specs=[
            pl.BlockSpec((None, tq, qw), lambda b, h, i: (b, i, h)),
            cmp_spec(0), cmp_spec(1),
            seq_spec(ODD_A_KS), seq_spec(ODD_C_VS), seq_spec(ODD_A_KW), seq_spec(ODD_C_VW),
            pl.BlockSpec((None, tq, LANES), lambda b, h, i: (b, i, ODD_C_GL + h)),
        ],
        out_specs=pl.BlockSpec((tq, qw), lambda b, h, i: (b * ntq + i, h)),
        out_shape=jax.ShapeDtypeStruct((batch * seq, NSA_HEADS * d), BF16),
        scratch_shapes=[pltpu.VMEM((d, grp * tq), F32), pltpu.VMEM((d, grp * tq), F32)],
        compiler_params=_params("parallel", "parallel", "arbitrary"),
        name="nsa",
    )(zq, kvc, kvc, za, zc, za, zc, zc)


def _odd_params(w_in, layer, a_re, a_im, log_dt, b_re, b_im, c_re, c_im, d_skip, glu_w, glu_b,
                cmp_pos, cmp_w1, cmp_b1, cmp_w2, seq):
    kvw = NSA_KV_WIDTH
    qw = NSA_HEADS * NSA_HEAD_DIM
    o_q = S5_WIDTH
    o_kv = o_q + qw
    kc, vc, ks, vs, kw, vw = (o_kv + n * kvw for n in range(6))
    w_gl = w_in[layer][:, o_kv + 6 * kvw:].reshape(D_MODEL, NSA_KV_HEADS, 3 * NSA_GROUP)
    w_gl = jnp.pad(w_gl, ((0, 0), (0, 0), (0, LANES - 3 * NSA_GROUP))).reshape(D_MODEL, NSA_KV_HEADS * LANES)
    w_a = _regroup_cast(w_in, layer, _col_tiles(0, o_q) + _col_tiles(kc, kvw) + _col_tiles(ks, kvw) + _col_tiles(kw, kvw))
    w_q = _regroup_cast(w_in, layer, _col_tiles(o_q, qw))
    w_c = _regroup_cast(w_in, layer, _col_tiles(vc, kvw) + _col_tiles(vs, kvw) + _col_tiles(vw, kvw), extra=w_gl)
    p = _s5_params(a_re, a_im, log_dt, b_re, b_im, c_re, c_im, d_skip, glu_w, glu_b)
    p.update({
        "w_a": w_a, "w_q": w_q, "w_c": w_c,
        "cmp_w1": cmp_w1.astype(BF16),
        "cmp_pos": cmp_pos.astype(F32),
        "cmp_b1": cmp_b1.astype(F32).reshape(2, 1, NSA_HEAD_DIM),
        "cmp_w2": cmp_w2.astype(BF16),
        "tables": _rope_tables(seq, NSA_HEAD_DIM),
    })
    return p


def _odd_mixer(x, h, g_post, p, w_out, layer, batch, seq):
    tables = p["tables"]
    za = _proj(h, p["w_a"], tables, seq, (ODD_A_KC * LANES, ODD_A_END * LANES))
    zq = _proj(h, p["w_q"], tables, seq, (0, NSA_HEADS * NSA_HEAD_DIM))
    zc = _proj(h, p["w_c"], tables, seq, (0, 0))
    s5_out = _s5(za, p, batch, seq)
    kvc = _compress(za, zc, p, batch, seq)
    o = _nsa(za, zq, zc, kvc, batch, seq)
    return _outproj(s5_out, o, w_out, layer, x, g_post)


def kernel(x, norm_gains, ffn1_w_gate, ffn1_w_up, ffn1_w_down, ffn2_w_gate, ffn2_w_up, ffn2_w_down, ev_w_in, ev_w_out, pool_w, pool_scale, swa_sinks, od_w_in, od_w_out, s5_a_re, s5_a_im, s5_log_dt, s5_b_re, s5_b_im, s5_c_re, s5_c_im, s5_d, s5_glu_w, s5_glu_b, nsa_cmp_pos, nsa_cmp_w1, nsa_cmp_b1, nsa_cmp_w2):
    batch, seq, _ = x.shape
    m = batch * seq
    depth = norm_gains.shape[0]
    xs = x.reshape(m, D_MODEL)
    even_tables = _rope_tables(seq, SWA_HEAD_DIM)
    ffn_w = [(_cast_bf16(wg, cols_out=D_FF_PAD), _cast_bf16(wu, cols_out=D_FF_PAD), _cast_bf16(wd, rows_out=D_FF_PAD))
             for wg, wu, wd in ((ffn1_w_gate, ffn1_w_up, ffn1_w_down), (ffn2_w_gate, ffn2_w_up, ffn2_w_down))]
    ev_out, od_out = _cast_bf16(ev_w_out), _cast_bf16(od_w_out)
    for layer in range(depth):
        g = norm_gains[layer].astype(F32).reshape(6, 1, D_MODEL)
        i = layer // 2
        xs, h = _ffn(xs, g[0], *ffn_w[0], g[1], layer, g_next=g[2])
        if layer % 2 == 0:
            xs = _even_mixer(xs, h, g[3], *_even_weights(ev_w_in, i), ev_out, i,
                             pool_w[i].astype(BF16), pool_scale[i].astype(F32).reshape(1, POOL_WIDTH),
                             swa_sinks[i].astype(F32), even_tables, batch, seq)
        else:
            p = _odd_params(od_w_in, i, s5_a_re[i], s5_a_im[i], s5_log_dt[i], s5_b_re[i], s5_b_im[i],
                            s5_c_re[i], s5_c_im[i], s5_d[i], s5_glu_w[i], s5_glu_b[i], nsa_cmp_pos[i],
                            nsa_cmp_w1[i], nsa_cmp_b1[i], nsa_cmp_w2[i], seq)
            xs = _odd_mixer(xs, h, g[3], p, od_out, i, batch, seq)
        xs = _ffn(xs, g[4], *ffn_w[1], g[5], layer)
    return xs.reshape(batch, seq, D_MODEL)
```
